```python
import jax, jax.numpy as jnp
from jax import lax
import numpy as np

D_MODEL = 1024
BATCH = 1
SEQ = 16384
DEPTH = 2
DEC_BATCH = 32
DEC_SEQ = 16
PAST_LEN = 1024

CHUNK = 64
Q_BLOCK = 128
EPS = 1e-6
MLA_HEADS = 8
NOPE_DIM = 64
ROPE_DIM = 32
QK_DIM = NOPE_DIM + ROPE_DIM
V_DIM = 64
Q_LORA = 256
KV_LORA = 128
ROPE_BASE = 10000.0
MLA_WIDTH = MLA_HEADS * V_DIM
LSTM_HEADS = 4
LSTM_DK = 128
LSTM_DV = 128
CONV_W = 4
LSTM_WIDTH = LSTM_HEADS * LSTM_DV
LSTM_QK = 2 * LSTM_HEADS * LSTM_DK
MIX_WIDTH = MLA_WIDTH + LSTM_WIDTH
D_FF = 4 * D_MODEL
SPLITS = (Q_LORA, KV_LORA, ROPE_DIM, LSTM_QK, LSTM_WIDTH, LSTM_WIDTH, LSTM_HEADS, LSTM_HEADS)
IN_COLS = Q_LORA + KV_LORA + ROPE_DIM + LSTM_QK + 2 * LSTM_WIDTH + 2 * LSTM_HEADS

kernel_name = "hybrid_mla_mlstm_streaming_step"


def rmsnorm(x, g):
    xf = x.astype(jnp.float32)
    y = xf * lax.rsqrt(jnp.mean(xf * xf, axis=-1, keepdims=True) + EPS)
    return (y * g.astype(jnp.float32)).astype(x.dtype)


def rope(x, pos):
    half = ROPE_DIM // 2
    inv = ROPE_BASE ** (-jnp.arange(half, dtype=jnp.float32) / half)
    ang = pos.astype(jnp.float32)[:, None] * inv[None, :]
    cos = jnp.cos(ang)[:, None, :]
    sin = jnp.sin(ang)[:, None, :]
    xf = x.astype(jnp.float32)
    x1, x2 = xf[..., :half], xf[..., half:]
    return jnp.concatenate([x1 * cos - x2 * sin, x2 * cos + x1 * sin], axis=-1).astype(x.dtype)


def split_projection(z):
    parts, start = [], 0
    for w in SPLITS:
        parts.append(z[..., start:start + w])
        start += w
    return parts


def mixer_inputs(x, pos, lw):
    B, S = x.shape[:2]
    xn = rmsnorm(x, lw["g_mix"])
    c_q, c_kv, k_pe, qk_raw, v_raw, o_raw, ig_raw, fg_raw = split_projection(xn @ lw["w_in"])
    q = (rmsnorm(c_q, lw["g_q_lat"]) @ lw["w_uq"]).reshape(B, S, MLA_HEADS, QK_DIM)
    q_nope = rmsnorm(q[..., :NOPE_DIM], lw["g_q_nope"])
    q_rope = rope(rmsnorm(q[..., NOPE_DIM:], lw["g_q_rope"]), pos)
    q = jnp.concatenate([q_nope, q_rope], axis=-1)
    lat = rmsnorm(c_kv, lw["g_kv_lat"])
    k_rope = rope(rmsnorm(k_pe, lw["g_k_rope"])[:, :, None, :], pos)[:, :, 0, :]
    return q, lat, k_rope, qk_raw, v_raw, o_raw, ig_raw, fg_raw


def mla_keys_values(lat, k_rope, lw):
    B, S = lat.shape[:2]
    k_nope = rmsnorm((lat @ lw["w_uk"]).reshape(B, S, MLA_HEADS, NOPE_DIM), lw["g_k_nope"])
    k_r = jnp.broadcast_to(k_rope[:, :, None, :], (B, S, MLA_HEADS, ROPE_DIM))
    k = jnp.concatenate([k_nope, k_r], axis=-1)
    v = (lat @ lw["w_uv"]).reshape(B, S, MLA_HEADS, V_DIM)
    return k, v


def mla_prompt_attention(q, k, v):
    B, S = q.shape[:2]
    nb = S // Q_BLOCK
    scale = QK_DIM ** -0.5
    qb = q.reshape(B, nb, Q_BLOCK, MLA_HEADS, QK_DIM).swapaxes(0, 1)
    k_chunk = jnp.arange(S) // CHUNK

    def block(args):
        qi, bi = args
        q_chunk = (bi * Q_BLOCK + jnp.arange(Q_BLOCK)) // CHUNK
        s = jnp.einsum('bqhd,bkhd->bhqk', qi, k, preferred_element_type=jnp.float32) * scale
        s = jnp.where(k_chunk[None, :] <= q_chunk[:, None], s, -1e30)
        p = jax.nn.softmax(s, axis=-1)
        return jnp.einsum('bhqk,bkhd->bqhd', p.astype(v.dtype), v)

    out = lax.map(block, (qb, jnp.arange(nb)))
    return out.swapaxes(0, 1).reshape(B, S, MLA_WIDTH)


def mla_sample_attention(q, k, v):
    B, L = q.shape[:2]
    s = jnp.einsum('bqhd,bkhd->bhqk', q, k, preferred_element_type=jnp.float32) * (QK_DIM ** -0.5)
    p = jax.nn.softmax(s, axis=-1)
    return jnp.einsum('bhqk,bkhd->bqhd', p.astype(v.dtype), v).reshape(B, L, MLA_WIDTH)


def causal_conv(u, buf, w, b):
    L = u.shape[1]
    full = jnp.concatenate([buf.astype(u.dtype), u], axis=1)
    y = b
    for j in range(CONV_W):
        y = y + full[:, j:j + L, :] * w[j]
    return jax.nn.silu(y), full[:, -(CONV_W - 1):, :]


def mlstm_inputs(qk_conv, v_raw, ig_raw, fg_raw, lw):
    B, L = qk_conv.shape[:2]
    qk = qk_conv.astype(jnp.float32).reshape(B, L, 2, LSTM_HEADS, LSTM_DK)
    q = qk[:, :, 0]
    k = qk[:, :, 1] * (LSTM_DK ** -0.5)
    v = v_raw.astype(jnp.float32).reshape(B, L, LSTM_HEADS, LSTM_DV)
    ig = ig_raw.astype(jnp.float32) + lw["b_igate"].astype(jnp.float32)
    lf = jax.nn.log_sigmoid(fg_raw.astype(jnp.float32) + lw["b_fgate"].astype(jnp.float32))
    return q, k, v, ig, lf


def mlstm_chunk(state, inp):
    C, n, m = state
    q, k, v, ig, lf = inp
    L = q.shape[1]
    b = jnp.cumsum(lf, axis=1)
    log_d = b[:, :, None, :] - b[:, None, :, :] + ig[:, None, :, :]
    causal = jnp.tril(jnp.ones((L, L), dtype=bool))
    log_d = jnp.where(causal[None, :, :, None], log_d, -jnp.inf)
    inter = b + m[:, None, :]
    m_t = jnp.maximum(inter, jnp.max(log_d, axis=2))
    w = jnp.einsum('bthd,bshd->btsh', q, k) * jnp.exp(log_d - m_t[:, :, None, :])
    inter_scale = jnp.exp(inter - m_t)
    num = jnp.einsum('btsh,bshv->bthv', w, v) + inter_scale[..., None] * jnp.einsum('bhvd,bthd->bthv', C, q)
    den = jnp.sum(w, axis=2) + inter_scale * jnp.einsum('bhd,bthd->bth', n, q)
    h = num / jnp.maximum(jnp.abs(den), jnp.exp(-m_t))[..., None]
    m_new = m_t[:, -1]
    carry = jnp.exp(b[:, -1] + m - m_new)
    d_end = jnp.exp(b[:, -1:, :] - b + ig - m_new[:, None, :])
    C_new = carry[..., None, None] * C + jnp.einsum('bsh,bshv,bshd->bhvd', d_end, v, k)
    n_new = carry[..., None] * n + jnp.einsum('bsh,bshd->bhd', d_end, k)
    return (C_new, n_new, m_new), h


def mlstm_prompt(q, k, v, ig, lf):
    B, S = q.shape[:2]
    nc = S // CHUNK

    def to_chunks(a):
        return a.reshape((B, nc, CHUNK) + a.shape[2:]).swapaxes(0, 1)

    state0 = (jnp.zeros((B, LSTM_HEADS, LSTM_DV, LSTM_DK), jnp.float32),
              jnp.zeros((B, LSTM_HEADS, LSTM_DK), jnp.float32),
              jnp.zeros((B, LSTM_HEADS), jnp.float32))
    state, h = lax.scan(mlstm_chunk, state0, (to_chunks(q), to_chunks(k), to_chunks(v), to_chunks(ig), to_chunks(lf)))
    return state, h.swapaxes(0, 1).reshape(B, S, LSTM_HEADS, LSTM_DV)


def layer_output(x, attn, h, o_raw, lw):
    B, L = x.shape[:2]
    h = rmsnorm(h.astype(x.dtype), lw["g_lstm_out"].reshape(LSTM_HEADS, LSTM_DV)).reshape(B, L, LSTM_WIDTH)
    lstm_out = h * jax.nn.sigmoid(o_raw.astype(jnp.float32)).astype(x.dtype)
    x = x + jnp.concatenate([attn, lstm_out], axis=-1) @ lw["w_out"]
    xn = rmsnorm(x, lw["g_ffn"])
    return x + jnp.square(jax.nn.relu(xn @ lw["w_up"])) @ lw["w_down"]


def setup_inputs(seed: int = 0) -> dict:
    key = jax.random.key(seed)
    ks = jax.random.split(key, 40)
    f = jnp.float32

    def nrm(i, shape, scale):
        return jax.random.normal(ks[i], shape, f) * scale

    def gain(i, shape):
        return 1.0 + nrm(i, shape, 0.01)

    return {
        "x_prompt": nrm(0, (BATCH, SEQ, D_MODEL), 1.0),
        "x_sample": nrm(1, (DEC_BATCH, DEC_SEQ, D_MODEL), 1.0),
        "cache_kv_latent": nrm(2, (DEPTH, DEC_BATCH, PAST_LEN, KV_LORA), 1.0),
        "cache_k_rope": nrm(3, (DEPTH, DEC_BATCH, PAST_LEN, ROPE_DIM), 1.0),
        "state_conv": nrm(4, (DEPTH, DEC_BATCH, CONV_W - 1, LSTM_QK), 1.0),
        "state_C": nrm(5, (DEPTH, DEC_BATCH, LSTM_HEADS, LSTM_DV, LSTM_DK), 0.1),
        "state_n": nrm(6, (DEPTH, DEC_BATCH, LSTM_HEADS, LSTM_DK), 0.1),
        "state_m": nrm(7, (DEPTH, DEC_BATCH, LSTM_HEADS), 0.5),
        "g_mix": gain(8, (DEPTH, D_MODEL)),
        "w_in": nrm(9, (DEPTH, D_MODEL, IN_COLS), D_MODEL ** -0.5),
        "g_q_lat": gain(10, (DEPTH, Q_LORA)),
        "w_uq": nrm(11, (DEPTH, Q_LORA, MLA_HEADS * QK_DIM), Q_LORA ** -0.5),
        "g_q_nope": gain(12, (DEPTH, NOPE_DIM)),
        "g_q_rope": gain(13, (DEPTH, ROPE_DIM)),
        "g_kv_lat": gain(14, (DEPTH, KV_LORA)),
        "g_k_rope": gain(15, (DEPTH, ROPE_DIM)),
        "w_uk": nrm(16, (DEPTH, KV_LORA, MLA_HEADS * NOPE_DIM), KV_LORA ** -0.5),
        "g_k_nope": gain(17, (DEPTH, NOPE_DIM)),
        "w_uv": nrm(18, (DEPTH, KV_LORA, MLA_HEADS * V_DIM), KV_LORA ** -0.5),
        "w_conv": nrm(19, (DEPTH, CONV_W, LSTM_QK), CONV_W ** -0.5),
        "b_conv": nrm(20, (DEPTH, LSTM_QK), 0.01),
        "b_igate": nrm(21, (DEPTH, LSTM_HEADS), 0.1),
        "b_fgate": jnp.broadcast_to(jnp.linspace(3.0, 6.0, LSTM_HEADS, dtype=f), (DEPTH, LSTM_HEADS)) + nrm(22, (DEPTH, LSTM_HEADS), 0.01),
        "g_lstm_out": gain(23, (DEPTH, LSTM_WIDTH)),
        "w_out": nrm(24, (DEPTH, MIX_WIDTH, D_MODEL), MIX_WIDTH ** -0.5),
        "g_ffn": gain(25, (DEPTH, D_MODEL)),
        "w_up": nrm(26, (DEPTH, D_MODEL, D_FF), D_MODEL ** -0.5),
        "w_down": nrm(27, (DEPTH, D_FF, D_MODEL), D_FF ** -0.5),
    }


def reference(x_prompt, x_sample, cache_kv_latent, cache_k_rope, state_conv, state_C, state_n, state_m,
              g_mix, w_in, g_q_lat, w_uq, g_q_nope, g_q_rope, g_kv_lat, g_k_rope, w_uk, g_k_nope, w_uv,
              w_conv, b_conv, b_igate, b_fgate, g_lstm_out, w_out, g_ffn, w_up, w_down):
    pos_p = jnp.arange(x_prompt.shape[1], dtype=jnp.int32)
    pos_s = PAST_LEN + jnp.arange(x_sample.shape[1], dtype=jnp.int32)
    xp, xs = x_prompt, x_sample
    p_lat, p_kr, p_conv, p_C, p_n, p_m = [], [], [], [], [], []
    s_lat, s_kr, s_conv, s_C, s_n, s_m = [], [], [], [], [], []
    for l in range(DEPTH):
        lw = {"g_mix": g_mix[l], "w_in": w_in[l], "g_q_lat": g_q_lat[l], "w_uq": w_uq[l],
              "g_q_nope": g_q_nope[l], "g_q_rope": g_q_rope[l], "g_kv_lat": g_kv_lat[l], "g_k_rope": g_k_rope[l],
              "w_uk": w_uk[l], "g_k_nope": g_k_nope[l], "w_uv": w_uv[l], "b_igate": b_igate[l], "b_fgate": b_fgate[l],
              "g_lstm_out": g_lstm_out[l], "w_out": w_out[l], "g_ffn": g_ffn[l], "w_up": w_up[l], "w_down": w_down[l]}
        q, lat, kr, qk_raw, v_raw, o_raw, ig_raw, fg_raw = mixer_inputs(xp, pos_p, lw)
        k, v = mla_keys_values(lat, kr, lw)
        attn = mla_prompt_attention(q, k, v)
        buf0 = jnp.zeros((xp.shape[0], CONV_W - 1, LSTM_QK), qk_raw.dtype)
        qk_c, conv_new = causal_conv(qk_raw, buf0, w_conv[l], b_conv[l])
        (C_new, n_new, m_new), h = mlstm_prompt(*mlstm_inputs(qk_c, v_raw, ig_raw, fg_raw, lw))
        xp = layer_output(xp, attn, h, o_raw, lw)
        p_lat.append(lat); p_kr.append(kr); p_conv.append(conv_new)
        p_C.append(C_new.astype(xp.dtype)); p_n.append(n_new.astype(xp.dtype)); p_m.append(m_new.astype(xp.dtype))
        q, lat, kr, qk_raw, v_raw, o_raw, ig_raw, fg_raw = mixer_inputs(xs, pos_s, lw)
        lat_all = jnp.concatenate([cache_kv_latent[l].astype(lat.dtype), lat], axis=1)
        kr_all = jnp.concatenate([cache_k_rope[l].astype(kr.dtype), kr], axis=1)
        k, v = mla_keys_values(lat_all, kr_all, lw)
        attn = mla_sample_attention(q, k, v)
        qk_c, conv_new = causal_conv(qk_raw, state_conv[l], w_conv[l], b_conv[l])
        st = (state_C[l].astype(jnp.float32), state_n[l].astype(jnp.float32), state_m[l].astype(jnp.float32))
        (C_new, n_new, m_new), h = mlstm_chunk(st, mlstm_inputs(qk_c, v_raw, ig_raw, fg_raw, lw))
        xs = layer_output(xs, attn, h, o_raw, lw)
        s_lat.append(lat); s_kr.append(kr); s_conv.append(conv_new)
        s_C.append(C_new.astype(state_C.dtype)); s_n.append(n_new.astype(state_n.dtype)); s_m.append(m_new.astype(state_m.dtype))
    return (xp, xs,
            jnp.stack(p_lat), jnp.stack(p_kr), jnp.stack(p_conv), jnp.stack(p_C), jnp.stack(p_n), jnp.stack(p_m),
            jnp.stack(s_lat), jnp.stack(s_kr), jnp.stack(s_conv), jnp.stack(s_C), jnp.stack(s_n), jnp.stack(s_m))
```

```python
import functools

import jax
import jax.numpy as jnp
from jax import lax
from jax.experimental import pallas as pl
from jax.experimental.pallas import tpu as pltpu

F32 = jnp.float32
BF16 = jnp.bfloat16

EPS = 1e-6
CHUNK = 64
MLA_HEADS = 8
NOPE_DIM = 64
ROPE_DIM = 32
QK_DIM = NOPE_DIM + ROPE_DIM
V_DIM = 64
Q_LORA = 256
KV_LORA = 128
ROPE_BASE = 10000.0
LSTM_HEADS = 4
LSTM_DK = 128
LSTM_DV = 128
CONV_W = 4
LSTM_WIDTH = LSTM_HEADS * LSTM_DV
LSTM_QK = 2 * LSTM_HEADS * LSTM_DK
MLA_WIDTH = MLA_HEADS * V_DIM

LANES = 128
SUBLANES = 8
HEAD_PAIR = 2 * LANES
VMEM_LIMIT = 52 * 1024 * 1024

MISC_KPE = 0
MISC_KPE_ROT = ROPE_DIM
MISC_IG = 2 * ROPE_DIM
MISC_FG = MISC_IG + LSTM_HEADS

COL_CQ = (0, Q_LORA)
COL_CKV = (COL_CQ[1], COL_CQ[1] + KV_LORA)
COL_QK = (COL_CKV[1], COL_CKV[1] + LSTM_QK)
COL_V = (COL_QK[1], COL_QK[1] + LSTM_WIDTH)
COL_O = (COL_V[1], COL_V[1] + LSTM_WIDTH)
COL_MISC = (COL_O[1], COL_O[1] + LANES)
PACKED_COLS = COL_MISC[1]


def _rot_half(a):
    half = ROPE_DIM // 2
    return jnp.concatenate([a[..., half:], a[..., :half]], axis=-1)


def _rms(x, g):
    return x * lax.rsqrt(jnp.mean(x * x, axis=-1, keepdims=True) + EPS) * g


def _segment_mean_sq(y, m_ref):
    sq = y * y
    hi = sq.astype(BF16)
    lo = (sq - hi.astype(F32)).astype(BF16)
    m = m_ref[...]
    return (jnp.dot(hi, m, preferred_element_type=F32)
            + jnp.dot(lo, m, preferred_element_type=F32))


def _proj_kernel(x_ref, cq_ref, sq_ref, ck_ref, sk_ref, g_mix_ref, w_main_ref, g_qlat_ref, w_uq_ref,
                 gq_ref, g_kvlat_ref, gkm_ref, w_uk_ref, gkn_ref, w_uv_ref, mseg_ref, mmisc_ref,
                 q_out, k_out, v_out, lat_out, kr_out, qk_out, vl_out, o_out, misc_out):
    x = x_ref[...]
    xb = _rms(x, g_mix_ref[...]).astype(BF16)

    def proj(col):
        return jnp.dot(xb, w_main_ref[:, col[0]:col[1]], preferred_element_type=F32)

    qk_out[...] = proj(COL_QK)
    vl_out[...] = proj(COL_V)
    o_out[...] = proj(COL_O)
    misc = proj(COL_MISC)
    misc_out[...] = misc

    cqn = _rms(proj(COL_CQ), g_qlat_ref[...]).astype(BF16)
    cq_tab = cq_ref[...]
    sq_tab = sq_ref[...]
    for p in range(MLA_HEADS // 2):
        cols = slice(p * HEAD_PAIR, (p + 1) * HEAD_PAIR)
        raw = jnp.dot(cqn, w_uq_ref[:, cols], preferred_element_type=F32)
        y = raw * lax.rsqrt(_segment_mean_sq(raw, mseg_ref) + EPS) * gq_ref[:, cols]
        for j in range(2):
            yh = y[:, j * LANES:(j + 1) * LANES]
            qh = yh * cq_tab + pltpu.roll(yh, LANES - ROPE_DIM, 1) * sq_tab
            h = 2 * p + j
            q_out[:, h * LANES:(h + 1) * LANES] = qh.astype(BF16)

    lat = _rms(proj(COL_CKV), g_kvlat_ref[...])
    lat_out[...] = lat
    latb = lat.astype(BF16)
    v_out[...] = jnp.dot(latb, w_uv_ref[...], preferred_element_type=F32).astype(BF16)

    ms = misc * lax.rsqrt(_segment_mean_sq(misc, mmisc_ref) + EPS) * gkm_ref[...]
    kr = ms * ck_ref[...] + pltpu.roll(ms, LANES - ROPE_DIM, 1) * sk_ref[...]
    kr_out[...] = kr[:, :ROPE_DIM]
    kr_placed = pltpu.roll(kr, NOPE_DIM, 1)
    kr_pair = jnp.concatenate([kr_placed, kr_placed], axis=1)
    for p in range(MLA_HEADS // 2):
        cols = slice(p * HEAD_PAIR, (p + 1) * HEAD_PAIR)
        raw = jnp.dot(latb, w_uk_ref[:, cols], preferred_element_type=F32)
        y = raw * lax.rsqrt(_segment_mean_sq(raw, mseg_ref) + EPS) * gkn_ref[:, cols]
        k_out[:, cols] = (y + kr_pair).astype(BF16)


def _full_spec(a):
    nd = a.ndim
    return pl.BlockSpec(a.shape, lambda *_: (0,) * nd)


def _projection(x2, tabs, w, tm):
    t = x2.shape[0]
    assert t % tm == 0
    row = lambda c: pl.BlockSpec((tm, c), lambda i: (i, 0))
    weights = [w["g_mix"], w["w_main"], w["g_q_lat"], w["w_uq"], w["gq"], w["g_kv_lat"], w["gkm"],
               w["w_uk"], w["gkn"], w["w_uv"], w["mseg"], w["mmisc"]]
    out_shape = (
        jax.ShapeDtypeStruct((t, MLA_HEADS * LANES), BF16),
        jax.ShapeDtypeStruct((t, MLA_HEADS * LANES), BF16),
        jax.ShapeDtypeStruct((t, MLA_WIDTH), BF16),
        jax.ShapeDtypeStruct((t, KV_LORA), F32),
        jax.ShapeDtypeStruct((t, ROPE_DIM), F32),
        jax.ShapeDtypeStruct((t, LSTM_QK), F32),
        jax.ShapeDtypeStruct((t, LSTM_WIDTH), F32),
        jax.ShapeDtypeStruct((t, LSTM_WIDTH), F32),
        jax.ShapeDtypeStruct((t, LANES), F32),
    )
    return pl.pallas_call(
        _proj_kernel,
        grid=(t // tm,),
        in_specs=[row(x2.shape[1])] + [row(LANES)] * 4 + [_full_spec(a) for a in weights],
        out_specs=tuple(row(s.shape[1]) for s in out_shape),
        out_shape=out_shape,
        compiler_params=pltpu.CompilerParams(dimension_semantics=("parallel",),
                                             vmem_limit_bytes=VMEM_LIMIT),
        name="projection",
    )(x2, *tabs, *weights)


def _attn_prompt_kernel(q_ref, k_ref, v_ref, o_ref, m_sc, l_sc, acc_sc, *, tq, tk):
    qi = pl.program_id(2)
    scale = QK_DIM ** -0.5
    m_sc[...] = jnp.full(m_sc.shape, -1e30, F32)
    l_sc[...] = jnp.zeros(l_sc.shape, F32)
    acc_sc[...] = jnp.zeros(acc_sc.shape, F32)
    lane = lax.broadcasted_iota(jnp.int32, (tq, LANES), 1)

    def tile(k0, masked):
        v = v_ref[pl.ds(k0, tk), :]
        for j in range(2):
            q = q_ref[:, j * LANES:(j + 1) * LANES]
            k = k_ref[pl.ds(k0, tk), j * LANES:(j + 1) * LANES]
            s = lax.dot_general(q, k, (((1,), (1,)), ((), ())), preferred_element_type=F32) * scale
            if masked:
                qc = lax.broadcasted_iota(jnp.int32, (tq, tk), 0) // CHUNK
                kc = lax.broadcasted_iota(jnp.int32, (tq, tk), 1) // CHUNK
                s = jnp.where(kc <= qc, s, -1e30)
            m_prev = m_sc[j]
            m_next = jnp.maximum(m_prev, jnp.max(s, axis=1, keepdims=True))
            alpha = jnp.exp(m_prev - m_next)
            p = jnp.exp(s - pltpu.repeat(m_next, tk // LANES, axis=1))
            l_sc[j] = alpha * l_sc[j] + jnp.sum(p, axis=1, keepdims=True)
            m_sc[j] = m_next
            acc_sc[j] = alpha * acc_sc[j] + jnp.dot(p.astype(BF16), v, preferred_element_type=F32)

    def body(ki, carry):
        tile(pl.multiple_of(ki * tk, tk), False)
        return carry

    lax.fori_loop(0, qi, body, 0)
    tile(pl.multiple_of(qi * tk, tk), True)

    out0 = acc_sc[0] / l_sc[0]
    out1 = acc_sc[1] / l_sc[1]
    o_ref[...] = jnp.where(lane < V_DIM, out0, out1).astype(o_ref.dtype)


def _attention_prompt(q, k, v, tq=256):
    b, s, _ = q.shape
    tk = tq
    assert s % tq == 0 and tq % CHUNK == 0
    pairs = MLA_HEADS // 2
    return pl.pallas_call(
        functools.partial(_attn_prompt_kernel, tq=tq, tk=tk),
        grid=(b, pairs, s // tq),
        in_specs=[
            pl.BlockSpec((None, tq, HEAD_PAIR), lambda bi, p, i: (bi, i, p)),
            pl.BlockSpec((None, s, HEAD_PAIR), lambda bi, p, i: (bi, 0, p)),
            pl.BlockSpec((None, s, LANES), lambda bi, p, i: (bi, 0, p)),
        ],
        out_specs=pl.BlockSpec((None, tq, LANES), lambda bi, p, i: (bi, i, p)),
        out_shape=jax.ShapeDtypeStruct((b, s, MLA_WIDTH), BF16),
        scratch_shapes=[pltpu.VMEM((2, tq, LANES), F32),
                        pltpu.VMEM((2, tq, LANES), F32),
                        pltpu.VMEM((2, tq, LANES), F32)],
        compiler_params=pltpu.CompilerParams(
            dimension_semantics=("parallel", "parallel", "arbitrary"),
            vmem_limit_bytes=VMEM_LIMIT),
        name="attention_prompt",
    )(q, k, v)


def _attn_sample_kernel(q_ref, kn_ref, vn_ref, lat_ref, kr_ref, w_uk_ref, gkn_ref, w_uv_ref, mseg_ref,
                        place_ref, o_ref, k_sc, *, n_new):
    scale = QK_DIM ** -0.5
    rows = MLA_HEADS * n_new
    latb = lat_ref[...].astype(BF16)
    kr_all = jnp.dot(kr_ref[...].astype(BF16), place_ref[...], preferred_element_type=F32)
    for p in range(MLA_HEADS // 2):
        cols = slice(p * HEAD_PAIR, (p + 1) * HEAD_PAIR)
        raw = jnp.dot(latb, w_uk_ref[:, cols], preferred_element_type=F32)
        y = raw * lax.rsqrt(_segment_mean_sq(raw, mseg_ref) + EPS) * gkn_ref[:, cols]
        k_sc[:, cols] = (y + kr_all[:, cols]).astype(BF16)
    v_all = jnp.dot(latb, w_uv_ref[...], preferred_element_type=F32).astype(BF16)

    q = q_ref[...]
    qt = jnp.concatenate([q] * MLA_HEADS, axis=0)
    r_head = lax.broadcasted_iota(jnp.int32, qt.shape, 0) // n_new
    c_head = lax.broadcasted_iota(jnp.int32, qt.shape, 1) // LANES
    qm = jnp.where(r_head == c_head, qt, jnp.zeros_like(qt))

    nt = (((1,), (1,)), ((), ()))
    s_old = lax.dot_general(k_sc[...], qm, nt, preferred_element_type=F32) * scale
    s_new = lax.dot_general(kn_ref[...], qm, nt, preferred_element_type=F32) * scale
    mx = jnp.maximum(jnp.max(s_old, axis=0, keepdims=True), jnp.max(s_new, axis=0, keepdims=True))
    p_old = jnp.exp(s_old - mx)
    p_new = jnp.exp(s_new - mx)
    inv = 1.0 / (jnp.sum(p_old, axis=0, keepdims=True) + jnp.sum(p_new, axis=0, keepdims=True))
    p_old = (p_old * inv).astype(BF16)
    p_new = (p_new * inv).astype(BF16)
    tn = (((0,), (0,)), ((), ()))
    full = (lax.dot_general(p_old, v_all, tn, preferred_element_type=F32)
            + lax.dot_general(p_new, vn_ref[...], tn, preferred_element_type=F32))
    v_head = lax.broadcasted_iota(jnp.int32, (n_new, MLA_WIDTH), 1) // V_DIM
    out = jnp.zeros((n_new, MLA_WIDTH), F32)
    for h in range(MLA_HEADS):
        out = out + jnp.where(v_head == h, full[h * n_new:(h + 1) * n_new, :], 0.0)
    o_ref[...] = out.astype(o_ref.dtype)


def _attention_sample(q, k_new, v_new, cache_lat, cache_kr, layer, w, n_new):
    _, b, past, _ = cache_lat.shape
    weights = [w["w_uk"], w["gkn"], w["w_uv"], w["mseg"], w["place"]]
    tok = lambda c: pl.BlockSpec((n_new, c), lambda i: (i, 0))
    return pl.pallas_call(
        functools.partial(_attn_sample_kernel, n_new=n_new),
        grid=(b,),
        in_specs=[tok(MLA_HEADS * LANES), tok(MLA_HEADS * LANES), tok(MLA_WIDTH),
                  pl.BlockSpec((None, None, past, KV_LORA), lambda i: (layer, i, 0, 0)),
                  pl.BlockSpec((None, None, past, ROPE_DIM), lambda i: (layer, i, 0, 0))]
                 + [_full_spec(a) for a in weights],
        out_specs=tok(MLA_WIDTH),
        out_shape=jax.ShapeDtypeStruct((b * n_new, MLA_WIDTH), BF16),
        scratch_shapes=[pltpu.VMEM((past, MLA_HEADS * LANES), BF16)],
        compiler_params=pltpu.CompilerParams(dimension_semantics=("parallel",),
                                             vmem_limit_bytes=VMEM_LIMIT),
        name="attention_sample",
    )(q, k_new, v_new, cache_lat, cache_kr, *weights)


CONV_PAD = SUBLANES


def _mlstm_kernel(qk_ref, v_ref, o_ref, misc_ref, conv0_ref, c0_ref, n0_ref, m0_ref, wconv_ref, bconv_ref,
                  bgate_ref, glstm_ref,
                  h_out, conv_out, c_out, n_out, m_out,
                  full_sc, c_sc, n_sc, m_sc, *, L):
    c = pl.program_id(1)
    last = pl.num_programs(1) - 1
    hist = CONV_W - 1
    lo = CONV_PAD - hist

    @pl.when(c == 0)
    def _init():
        full_sc[lo:CONV_PAD, :] = conv0_ref[...]
        c_sc[...] = c0_ref[...]
        n_sc[...] = n0_ref[...]
        m_sc[...] = m0_ref[...]

    full_sc[CONV_PAD:CONV_PAD + L, :] = qk_ref[...]
    y = bconv_ref[...]
    for j in range(CONV_W):
        y = y + full_sc[lo + j:lo + j + L, :] * wconv_ref[j:j + 1, :]
    qk = y * jax.nn.sigmoid(y)
    tail = full_sc[lo + L:CONV_PAD + L, :]
    full_sc[lo:CONV_PAD, :] = tail

    gs = misc_ref[...] + bgate_ref[...]
    lfs = jnp.minimum(gs, 0.0) - jnp.log1p(jnp.exp(-jnp.abs(gs)))
    row = lax.broadcasted_iota(jnp.int32, (L, L), 0)
    col = lax.broadcasted_iota(jnp.int32, (L, L), 1)
    causal = row >= col
    tri = jnp.where(causal, 1.0, 0.0).astype(BF16)
    lf_hi = lfs.astype(BF16)
    lf_mid = (lfs - lf_hi.astype(F32)).astype(BF16)
    lf_lo = (lfs - lf_hi.astype(F32) - lf_mid.astype(F32)).astype(BF16)
    b_slab = (jnp.dot(tri, lf_hi, preferred_element_type=F32)
              + jnp.dot(tri, lf_mid, preferred_element_type=F32)
              + jnp.dot(tri, lf_lo, preferred_element_type=F32))
    lane = lax.broadcasted_iota(jnp.int32, (L, LANES), 1)
    comb = jnp.where(lane < MISC_FG, gs, b_slab)
    comb_t = jnp.concatenate([comb, jnp.zeros((LANES - L, LANES), F32)], axis=0).T

    g_all = glstm_ref[...]
    for h in range(LSTM_HEADS):
        dk = slice(h * LSTM_DK, (h + 1) * LSTM_DK)
        dv = slice(h * LSTM_DV, (h + 1) * LSTM_DV)
        ig_col = gs[:, MISC_IG + h:MISC_IG + h + 1]
        b_col = b_slab[:, MISC_FG + h:MISC_FG + h + 1]
        ig_row = comb_t[MISC_IG + h:MISC_IG + h + 1, :L]
        b_row = comb_t[MISC_FG + h:MISC_FG + h + 1, :L]
        m_prev = m_sc[:, h:h + 1]

        log_d = jnp.where(causal, b_col - b_row + ig_row, -jnp.inf)
        inter = b_col + m_prev
        m_t = jnp.maximum(inter, jnp.max(log_d, axis=1, keepdims=True))
        decay = jnp.exp(log_d - m_t)
        inter_scale = jnp.exp(inter - m_t)

        qh = qk[:, dk]
        kh = qk[:, LSTM_HEADS * LSTM_DK + h * LSTM_DK:LSTM_HEADS * LSTM_DK + (h + 1) * LSTM_DK] * (LSTM_DK ** -0.5)
        vh = v_ref[:, dv]
        qb = qh.astype(BF16)
        kb = kh.astype(BF16)
        c_prev = c_sc[h]
        n_prev = n_sc[h:h + 1, :]
        wgt = lax.dot_general(qb, kb, (((1,), (1,)), ((), ())), preferred_element_type=F32) * decay
        cq = lax.dot_general(qb, c_prev.astype(BF16), (((1,), (1,)), ((), ())), preferred_element_type=F32)
        num = jnp.dot(wgt.astype(BF16), vh.astype(BF16), preferred_element_type=F32) + inter_scale * cq
        den = (jnp.sum(wgt, axis=1, keepdims=True)
               + inter_scale * jnp.sum(qh * n_prev, axis=1, keepdims=True))
        hid = num / jnp.maximum(jnp.abs(den), jnp.exp(-m_t))

        m_new = m_t[L - 1:L, :]
        b_last = b_col[L - 1:L, :]
        carry = jnp.exp(b_last + m_prev - m_new)
        d_end = jnp.exp(b_last - b_col + ig_col - m_new)
        upd = lax.dot_general((d_end * vh).astype(BF16), kb, (((0,), (0,)), ((), ())),
                              preferred_element_type=F32)
        c_sc[h] = carry * c_prev + upd
        n_sc[h:h + 1, :] = carry * n_prev + jnp.sum(d_end * kh, axis=0, keepdims=True)
        m_sc[:, h:h + 1] = m_new

        hn = _rms(hid, g_all[:, dv])
        h_out[:, dv] = (hn * jax.nn.sigmoid(o_ref[:, dv])).astype(h_out.dtype)

    @pl.when(c == last)
    def _finish():
        conv_out[...] = tail
        c_out[...] = c_sc[...]
        n_out[...] = n_sc[...]
        m_out[...] = m_sc[...]


def _mlstm(qk_raw, v_raw, o_raw, misc, conv0, c0, n0, m0, w, L):
    b = conv0.shape[0]
    t = qk_raw.shape[0]
    nc = t // (b * L)
    assert nc * b * L == t
    hist = CONV_W - 1
    tok = lambda cdim: pl.BlockSpec((L, cdim), lambda bi, ci: (bi * nc + ci, 0))
    m0 = m0.reshape(b, 1, LSTM_HEADS)
    weights = [w["w_conv"], w["b_conv"], w["bgate"], w["g_lstm_out"]]
    out_shape = (
        jax.ShapeDtypeStruct((t, LSTM_WIDTH), BF16),
        jax.ShapeDtypeStruct((b, hist, LSTM_QK), F32),
        jax.ShapeDtypeStruct((b, LSTM_HEADS, LSTM_DV, LSTM_DK), F32),
        jax.ShapeDtypeStruct((b, LSTM_HEADS, LSTM_DK), F32),
        jax.ShapeDtypeStruct((b, 1, LSTM_HEADS), F32),
    )
    state = lambda *dims: pl.BlockSpec((None,) + dims, lambda bi, ci: (bi,) + (0,) * len(dims))
    h, conv_new, c_new, n_new, m_new = pl.pallas_call(
        functools.partial(_mlstm_kernel, L=L),
        grid=(b, nc),
        in_specs=[tok(LSTM_QK), tok(LSTM_WIDTH), tok(LSTM_WIDTH), tok(LANES),
                  state(hist, LSTM_QK), state(LSTM_HEADS, LSTM_DV, LSTM_DK), state(LSTM_HEADS, LSTM_DK),
                  state(1, LSTM_HEADS)] + [_full_spec(a) for a in weights],
        out_specs=(tok(LSTM_WIDTH), state(hist, LSTM_QK), state(LSTM_HEADS, LSTM_DV, LSTM_DK),
                   state(LSTM_HEADS, LSTM_DK), state(1, LSTM_HEADS)),
        out_shape=out_shape,
        scratch_shapes=[pltpu.VMEM((CONV_PAD + L, LSTM_QK), F32),
                        pltpu.VMEM((LSTM_HEADS, LSTM_DV, LSTM_DK), F32),
                        pltpu.VMEM((LSTM_HEADS, LSTM_DK), F32),
                        pltpu.VMEM((1, LSTM_HEADS), F32)],
        compiler_params=pltpu.CompilerParams(dimension_semantics=("parallel", "arbitrary"),
                                             vmem_limit_bytes=VMEM_LIMIT),
        name="mlstm",
    )(qk_raw, v_raw, o_raw, misc, conv0, c0, n0, m0, *weights)
    return h, conv_new, c_new, n_new, m_new.reshape(b, LSTM_HEADS)


FF_CHUNK = 1024


def _out_ffn_kernel(x_ref, attn_ref, lstm_ref, w_out_ref, g_ffn_ref, w_up_ref, w_down_ref, y_ref):
    mix = jnp.concatenate([attn_ref[...], lstm_ref[...]], axis=1)
    x1 = x_ref[...] + jnp.dot(mix, w_out_ref[...], preferred_element_type=F32)
    xb = _rms(x1, g_ffn_ref[...]).astype(BF16)
    d_ff = w_up_ref.shape[1]
    acc = x1
    for f in range(d_ff // FF_CHUNK):
        cols = slice(f * FF_CHUNK, (f + 1) * FF_CHUNK)
        u = jnp.maximum(jnp.dot(xb, w_up_ref[:, cols], preferred_element_type=F32), 0.0)
        acc = acc + jnp.dot((u * u).astype(BF16), w_down_ref[cols, :], preferred_element_type=F32)
    y_ref[...] = acc


def _out_ffn(x2, attn, lstm, w, tm):
    t, d = x2.shape
    assert t % tm == 0
    row = lambda c: pl.BlockSpec((tm, c), lambda i: (i, 0))
    const = lambda a: pl.BlockSpec(a.shape, lambda i: (0, 0), pipeline_mode=pl.Buffered(1))
    weights = [w["w_out"], w["g_ffn"], w["w_up"], w["w_down"]]
    return pl.pallas_call(
        _out_ffn_kernel,
        grid=(t // tm,),
        in_specs=[row(d), row(MLA_WIDTH), row(LSTM_WIDTH)] + [const(a) for a in weights],
        out_specs=row(d),
        out_shape=jax.ShapeDtypeStruct((t, d), F32),
        compiler_params=pltpu.CompilerParams(dimension_semantics=("parallel",),
                                             vmem_limit_bytes=VMEM_LIMIT),
        name="out_ffn",
    )(x2, attn, lstm, *weights)


def _pack_layer(l, g_mix, w_in, g_q_lat, w_uq, g_q_nope, g_q_rope, g_kv_lat, g_k_rope, w_uk, g_k_nope, w_uv,
                w_conv, b_conv, b_igate, b_fgate, g_lstm_out, w_out, g_ffn, w_up, w_down):
    d = w_in.shape[1]
    wi = w_in[l]
    o = 0
    parts = {}
    for name, width in (("cq", Q_LORA), ("ckv", KV_LORA), ("kpe", ROPE_DIM), ("qk", LSTM_QK), ("v", LSTM_WIDTH),
                        ("o", LSTM_WIDTH), ("ig", LSTM_HEADS), ("fg", LSTM_HEADS)):
        parts[name] = wi[:, o:o + width]
        o += width
    misc = jnp.concatenate([parts["kpe"], _rot_half(parts["kpe"]), parts["ig"], parts["fg"],
                            jnp.zeros((d, LANES - MISC_FG - LSTM_HEADS), F32)], axis=1)
    w_main = jnp.concatenate([parts["cq"], parts["ckv"], parts["qk"], parts["v"], parts["o"], misc],
                             axis=1).astype(BF16)

    uq = w_uq[l].reshape(Q_LORA, MLA_HEADS, QK_DIM)
    uq = jnp.concatenate([uq, _rot_half(uq[..., NOPE_DIM:])], axis=-1).reshape(Q_LORA, MLA_HEADS * LANES)
    gq_head = jnp.concatenate([g_q_nope[l], g_q_rope[l], _rot_half(g_q_rope[l])])
    uk = w_uk[l].reshape(KV_LORA, MLA_HEADS, NOPE_DIM)
    uk = jnp.concatenate([uk, jnp.zeros_like(uk)], axis=-1).reshape(KV_LORA, MLA_HEADS * LANES)
    gkn_head = jnp.concatenate([g_k_nope[l], jnp.zeros((LANES - NOPE_DIM,), F32)])
    gkm = jnp.concatenate([g_k_rope[l], _rot_half(g_k_rope[l]), jnp.zeros((LANES - 2 * ROPE_DIM,), F32)])
    bgate = jnp.concatenate([jnp.zeros((MISC_IG,), F32), b_igate[l], b_fgate[l],
                             jnp.zeros((LANES - MISC_FG - LSTM_HEADS,), F32)])

    i = jnp.arange(HEAD_PAIR)
    same = (i[:, None] // LANES) == (i[None, :] // LANES)
    li, lj = i[:, None] % LANES, i[None, :] % LANES
    mseg = jnp.where(same & (li < NOPE_DIM) & (lj < NOPE_DIM), 1.0 / NOPE_DIM,
                     jnp.where(same & (li >= NOPE_DIM) & (li < QK_DIM) & (lj >= NOPE_DIM), 1.0 / ROPE_DIM, 0.0))
    a = jnp.arange(LANES)
    mmisc = jnp.where((a[:, None] < ROPE_DIM) & (a[None, :] < 2 * ROPE_DIM), 1.0 / ROPE_DIM, 0.0)
    r = jnp.arange(ROPE_DIM)
    c = jnp.arange(MLA_HEADS * LANES)
    place = ((c[None, :] % LANES) == (NOPE_DIM + r[:, None])).astype(BF16)

    row = lambda v: v.reshape(1, -1).astype(F32)
    return {
        "g_mix": row(g_mix[l]), "w_main": w_main, "g_q_lat": row(g_q_lat[l]), "w_uq": uq.astype(BF16),
        "gq": row(jnp.tile(gq_head, MLA_HEADS)), "g_kv_lat": row(g_kv_lat[l]), "gkm": row(gkm),
        "w_uk": uk.astype(BF16), "gkn": row(jnp.tile(gkn_head, MLA_HEADS)), "w_uv": w_uv[l].astype(BF16),
        "mseg": mseg.astype(BF16), "mmisc": mmisc.astype(BF16), "place": place,
        "w_conv": w_conv[l], "b_conv": row(b_conv[l]), "bgate": row(bgate), "g_lstm_out": row(g_lstm_out[l]),
        "w_out": w_out[l].astype(BF16), "g_ffn": row(g_ffn[l]), "w_up": w_up[l].astype(BF16),
        "w_down": w_down[l].astype(BF16),
    }


def _rope_tables(pos):
    half = ROPE_DIM // 2
    inv = ROPE_BASE ** (-jnp.arange(half, dtype=F32) / half)
    ang = pos.astype(F32)[:, None] * inv[None, :]
    cos2 = jnp.concatenate([jnp.cos(ang), jnp.cos(ang)], axis=1)
    sin2 = jnp.concatenate([-jnp.sin(ang), jnp.sin(ang)], axis=1)
    n = pos.shape[0]
    z = lambda c: jnp.zeros((n, c), F32)
    cq = jnp.concatenate([jnp.ones((n, NOPE_DIM), F32), cos2, z(LANES - QK_DIM)], axis=1)
    sq = jnp.concatenate([z(NOPE_DIM), sin2, z(LANES - QK_DIM)], axis=1)
    ck = jnp.concatenate([cos2, z(LANES - ROPE_DIM)], axis=1)
    sk = jnp.concatenate([sin2, z(LANES - ROPE_DIM)], axis=1)
    return cq, sq, ck, sk


def kernel(x_prompt, x_sample, cache_kv_latent, cache_k_rope, state_conv, state_C, state_n, state_m,
           g_mix, w_in, g_q_lat, w_uq, g_q_nope, g_q_rope, g_kv_lat, g_k_rope, w_uk, g_k_nope, w_uv,
           w_conv, b_conv, b_igate, b_fgate, g_lstm_out, w_out, g_ffn, w_up, w_down):
    depth = w_in.shape[0]
    bp, sp, d = x_prompt.shape
    bs, ls, _ = x_sample.shape
    past = cache_kv_latent.shape[2]
    hist = CONV_W - 1

    tabs_p = tuple(jnp.tile(t, (bp, 1)) for t in _rope_tables(jnp.arange(sp, dtype=jnp.int32)))
    tabs_s = tuple(jnp.tile(t, (bs, 1)) for t in _rope_tables(past + jnp.arange(ls, dtype=jnp.int32)))
    xp = x_prompt.reshape(bp * sp, d)
    xs = x_sample.reshape(bs * ls, d)
    zero_conv = jnp.zeros((bp, hist, LSTM_QK), F32)
    zero_c = jnp.zeros((bp, LSTM_HEADS, LSTM_DV, LSTM_DK), F32)
    zero_n = jnp.zeros((bp, LSTM_HEADS, LSTM_DK), F32)
    zero_m = jnp.zeros((bp, LSTM_HEADS), F32)

    outs = {k: [] for k in ("p_lat", "p_kr", "p_conv", "p_c", "p_n", "p_m",
                            "s_lat", "s_kr", "s_conv", "s_c", "s_n", "s_m")}
    for l in range(depth):
        w = _pack_layer(l, g_mix, w_in, g_q_lat, w_uq, g_q_nope, g_q_rope, g_kv_lat, g_k_rope, w_uk, g_k_nope,
                        w_uv, w_conv, b_conv, b_igate, b_fgate, g_lstm_out, w_out, g_ffn, w_up, w_down)
        q, k, v, lat, kr, qk_raw, v_raw, o_raw, misc = _projection(xp, tabs_p, w, tm=512)
        attn = _attention_prompt(q.reshape(bp, sp, -1), k.reshape(bp, sp, -1), v.reshape(bp, sp, -1))
        h, conv_new, c_new, n_new, m_new = _mlstm(qk_raw, v_raw, o_raw, misc, zero_conv, zero_c, zero_n, zero_m,
                                                  w, L=CHUNK)
        xp = _out_ffn(xp, attn.reshape(bp * sp, -1), h, w, tm=512)
        outs["p_lat"].append(lat.reshape(bp, sp, KV_LORA))
        outs["p_kr"].append(kr.reshape(bp, sp, ROPE_DIM))
        outs["p_conv"].append(conv_new)
        outs["p_c"].append(c_new)
        outs["p_n"].append(n_new)
        outs["p_m"].append(m_new)
        q, k, v, lat, kr, qk_raw, v_raw, o_raw, misc = _projection(xs, tabs_s, w, tm=bs * ls)
        attn = _attention_sample(q, k, v, cache_kv_latent, cache_k_rope, l, w, n_new=ls)
        h, conv_new, c_new, n_new, m_new = _mlstm(qk_raw, v_raw, o_raw, misc, state_conv[l], state_C[l],
                                                  state_n[l], state_m[l], w, L=ls)
        xs = _out_ffn(xs, attn, h, w, tm=bs * ls)
        outs["s_lat"].append(lat.reshape(bs, ls, KV_LORA))
        outs["s_kr"].append(kr.reshape(bs, ls, ROPE_DIM))
        outs["s_conv"].append(conv_new)
        outs["s_c"].append(c_new)
        outs["s_n"].append(n_new)
        outs["s_m"].append(m_new)

    st = lambda key: jnp.stack(outs[key])
    return (xp.reshape(bp, sp, d), xs.reshape(bs, ls, d),
            st("p_lat"), st("p_kr"), st("p_conv"), st("p_c"), st("p_n"), st("p_m"),
            st("s_lat"), st("s_kr"), st("s_conv"), st("s_c"), st("s_n"), st("s_m"))
```

```python
import functools

import jax
import jax.numpy as jnp
from jax import lax
from jax.experimental import pallas as pl
from jax.experimental.pallas import tpu as pltpu

F32 = jnp.float32
BF16 = jnp.bfloat16

EPS = 1e-6
CHUNK = 64
MLA_HEADS = 8
NOPE_DIM = 64
ROPE_DIM = 32
QK_DIM = NOPE_DIM + ROPE_DIM
V_DIM = 64
Q_LORA = 256
KV_LORA = 128
ROPE_BASE = 10000.0
LSTM_HEADS = 4
LSTM_DK = 128
LSTM_DV = 128
CONV_W = 4
LSTM_WIDTH = LSTM_HEADS * LSTM_DV
LSTM_QK = 2 * LSTM_HEADS * LSTM_DK
MLA_WIDTH = MLA_HEADS * V_DIM

LANES = 128
SUBLANES = 8
HEAD_PAIR = 2 * LANES
VMEM_LIMIT = 52 * 1024 * 1024
ATTN_TQ = 512
ATTN_TK = 512

MISC_KPE = 0
MISC_KPE_ROT = ROPE_DIM
MISC_IG = 2 * ROPE_DIM
MISC_FG = MISC_IG + LSTM_HEADS

COL_CQ = (0, Q_LORA)
COL_CKV = (COL_CQ[1], COL_CQ[1] + KV_LORA)
COL_QK = (COL_CKV[1], COL_CKV[1] + LSTM_QK)
COL_V = (COL_QK[1], COL_QK[1] + LSTM_WIDTH)
COL_O = (COL_V[1], COL_V[1] + LSTM_WIDTH)
COL_MISC = (COL_O[1], COL_O[1] + LANES)
PACKED_COLS = COL_MISC[1]


def _rot_half(a):
    half = ROPE_DIM // 2
    return jnp.concatenate([a[..., half:], a[..., :half]], axis=-1)


def _rms(x, g):
    return x * lax.rsqrt(jnp.mean(x * x, axis=-1, keepdims=True) + EPS) * g


def _segment_mean_sq(y, m_ref):
    sq = y * y
    hi = sq.astype(BF16)
    lo = (sq - hi.astype(F32)).astype(BF16)
    m = m_ref[...]
    return (jnp.dot(hi, m, preferred_element_type=F32)
            + jnp.dot(lo, m, preferred_element_type=F32))


def _proj_kernel(x_ref, cq_ref, sq_ref, ck_ref, sk_ref, g_mix_ref, w_main_ref, g_qlat_ref, w_uq_ref,
                 gq_ref, g_kvlat_ref, gkm_ref, w_uk_ref, gkn_ref, w_uv_ref, vones_ref, mseg_ref, mmisc_ref,
                 q_out, k_out, v_out, lat_out, kr_out, qk_out, vl_out, o_out, misc_out):
    x = x_ref[...]
    xb = _rms(x, g_mix_ref[...]).astype(BF16)

    def proj(col):
        return jnp.dot(xb, w_main_ref[:, col[0]:col[1]], preferred_element_type=F32)

    qk_out[...] = proj(COL_QK)
    vl_out[...] = proj(COL_V)
    o_out[...] = proj(COL_O)
    misc = proj(COL_MISC)
    misc_out[...] = misc

    cqn = _rms(proj(COL_CQ), g_qlat_ref[...]).astype(BF16)
    cq_tab = cq_ref[...]
    sq_tab = sq_ref[...]
    for p in range(MLA_HEADS // 2):
        cols = slice(p * HEAD_PAIR, (p + 1) * HEAD_PAIR)
        raw = jnp.dot(cqn, w_uq_ref[:, cols], preferred_element_type=F32)
        y = raw * lax.rsqrt(_segment_mean_sq(raw, mseg_ref) + EPS) * gq_ref[:, cols]
        for j in range(2):
            yh = y[:, j * LANES:(j + 1) * LANES]
            qh = yh * cq_tab + pltpu.roll(yh, LANES - ROPE_DIM, 1) * sq_tab
            h = 2 * p + j
            q_out[:, h * LANES:(h + 1) * LANES] = qh.astype(BF16)

    lat = _rms(proj(COL_CKV), g_kvlat_ref[...])
    lat_out[...] = lat
    latb = lat.astype(BF16)
    v_out[...] = (jnp.dot(latb, w_uv_ref[...], preferred_element_type=F32) + vones_ref[...]).astype(BF16)

    ms = misc * lax.rsqrt(_segment_mean_sq(misc, mmisc_ref) + EPS) * gkm_ref[...]
    kr = ms * ck_ref[...] + pltpu.roll(ms, LANES - ROPE_DIM, 1) * sk_ref[...]
    kr_out[...] = kr[:, :ROPE_DIM]
    kr_placed = pltpu.roll(kr, NOPE_DIM, 1)
    kr_pair = jnp.concatenate([kr_placed, kr_placed], axis=1)
    for p in range(MLA_HEADS // 2):
        cols = slice(p * HEAD_PAIR, (p + 1) * HEAD_PAIR)
        raw = jnp.dot(latb, w_uk_ref[:, cols], preferred_element_type=F32)
        y = raw * lax.rsqrt(_segment_mean_sq(raw, mseg_ref) + EPS) * gkn_ref[:, cols]
        k_out[:, cols] = (y + kr_pair).astype(BF16)


def _full_spec(a):
    nd = a.ndim
    return pl.BlockSpec(a.shape, lambda *_: (0,) * nd)


def _projection(x2, tabs, w, tm):
    t = x2.shape[0]
    assert t % tm == 0
    row = lambda c: pl.BlockSpec((tm, c), lambda i: (i, 0))
    weights = [w["g_mix"], w["w_main"], w["g_q_lat"], w["w_uq"], w["gq"], w["g_kv_lat"], w["gkm"],
               w["w_uk"], w["gkn"], w["w_uv_pad"], w["vones"], w["mseg"], w["mmisc"]]
    out_shape = (
        jax.ShapeDtypeStruct((t, MLA_HEADS * LANES), BF16),
        jax.ShapeDtypeStruct((t, MLA_HEADS * LANES), BF16),
        jax.ShapeDtypeStruct((t, MLA_HEADS * LANES), BF16),
        jax.ShapeDtypeStruct((t, KV_LORA), F32),
        jax.ShapeDtypeStruct((t, ROPE_DIM), F32),
        jax.ShapeDtypeStruct((t, LSTM_QK), F32),
        jax.ShapeDtypeStruct((t, LSTM_WIDTH), F32),
        jax.ShapeDtypeStruct((t, LSTM_WIDTH), F32),
        jax.ShapeDtypeStruct((t, LANES), F32),
    )
    return pl.pallas_call(
        _proj_kernel,
        grid=(t // tm,),
        in_specs=[row(x2.shape[1])] + [row(LANES)] * 4 + [_full_spec(a) for a in weights],
        out_specs=tuple(row(s.shape[1]) for s in out_shape),
        out_shape=out_shape,
        compiler_params=pltpu.CompilerParams(dimension_semantics=("parallel",),
                                             vmem_limit_bytes=VMEM_LIMIT),
        name="projection",
    )(x2, *tabs, *weights)


def _attn_prompt_kernel(q_ref, k_ref, v_ref, o_ref, m_sc, acc_sc, *, tq, tk):
    qi = pl.program_id(2)
    log2_scale = (QK_DIM ** -0.5) * 1.4426950408889634
    m_sc[...] = jnp.full(m_sc.shape, -1e30, F32)
    acc_sc[...] = jnp.zeros(acc_sc.shape, F32)
    q_row0 = qi * tq

    def tile(k0, masked):
        for j in range(2):
            q = q_ref[:, j * LANES:(j + 1) * LANES]
            k = k_ref[pl.ds(k0, tk), j * LANES:(j + 1) * LANES]
            v = v_ref[pl.ds(k0, tk), j * LANES:(j + 1) * LANES]
            s = lax.dot_general(q, k, (((1,), (1,)), ((), ())), preferred_element_type=F32) * log2_scale
            if masked:
                qc = (q_row0 + lax.broadcasted_iota(jnp.int32, (tq, tk), 0)) // CHUNK
                kc = (k0 + lax.broadcasted_iota(jnp.int32, (tq, tk), 1)) // CHUNK
                s = jnp.where(kc <= qc, s, -1e30)
            m_prev = m_sc[j]
            m_next = jnp.maximum(m_prev, jnp.max(s, axis=1, keepdims=True))
            alpha = jnp.exp2(m_prev - m_next)
            p = jnp.exp2(s - pltpu.repeat(m_next, tk // LANES, axis=1))
            m_sc[j] = m_next
            acc_sc[j] = alpha * acc_sc[j] + jnp.dot(p.astype(BF16), v, preferred_element_type=F32)

    n_full = q_row0 // tk
    n_all = (q_row0 + tq + tk - 1) // tk

    def body(ki, carry):
        tile(pl.multiple_of(ki * tk, tk), False)
        return carry

    lax.fori_loop(0, n_full, body, 0)
    for d in range((tq + tk - 1) // tk):
        @pl.when(n_full + d < n_all)
        def _diag():
            tile(pl.multiple_of((n_full + d) * tk, tk), True)

    acc0 = acc_sc[0]
    acc1 = acc_sc[1]
    out0 = acc0 / pltpu.roll(acc0, V_DIM, 1)
    out1 = pltpu.roll(acc1, V_DIM, 1) / acc1
    lane = lax.broadcasted_iota(jnp.int32, (tq, LANES), 1)
    o_ref[...] = jnp.where(lane < V_DIM, out0, out1).astype(o_ref.dtype)


def _attention_prompt(q, k, v, tq, tk):
    b, s, _ = q.shape
    assert s % tq == 0 and s % tk == 0 and tq % CHUNK == 0 and tk % CHUNK == 0
    assert tq % tk == 0 or tk % tq == 0
    pairs = MLA_HEADS // 2
    return pl.pallas_call(
        functools.partial(_attn_prompt_kernel, tq=tq, tk=tk),
        grid=(b, pairs, s // tq),
        in_specs=[
            pl.BlockSpec((None, tq, HEAD_PAIR), lambda bi, p, i: (bi, i, p)),
            pl.BlockSpec((None, s, HEAD_PAIR), lambda bi, p, i: (bi, 0, p)),
            pl.BlockSpec((None, s, HEAD_PAIR), lambda bi, p, i: (bi, 0, p)),
        ],
        out_specs=pl.BlockSpec((None, tq, LANES), lambda bi, p, i: (bi, i, p)),
        out_shape=jax.ShapeDtypeStruct((b, s, MLA_WIDTH), BF16),
        scratch_shapes=[pltpu.VMEM((2, tq, LANES), F32),
                        pltpu.VMEM((2, tq, LANES), F32)],
        compiler_params=pltpu.CompilerParams(
            dimension_semantics=("parallel", "parallel", "arbitrary"),
            vmem_limit_bytes=VMEM_LIMIT),
        name="attention_prompt",
    )(q, k, v)


def _attn_sample_kernel(q_ref, kn_ref, latn_ref, lat_ref, kr_ref, w_uk_ref, gkn_ref, w_uv_ref, mseg_ref,
                        place_ref, o_ref, k_sc, *, n_new):
    scale = QK_DIM ** -0.5
    rows = MLA_HEADS * n_new
    latb = lat_ref[...].astype(BF16)
    kr_all = jnp.dot(kr_ref[...].astype(BF16), place_ref[...], preferred_element_type=F32)
    for p in range(MLA_HEADS // 2):
        cols = slice(p * HEAD_PAIR, (p + 1) * HEAD_PAIR)
        raw = jnp.dot(latb, w_uk_ref[:, cols], preferred_element_type=F32)
        y = raw * lax.rsqrt(_segment_mean_sq(raw, mseg_ref) + EPS) * gkn_ref[:, cols]
        k_sc[:, cols] = (y + kr_all[:, cols]).astype(BF16)
    v_all = jnp.dot(latb, w_uv_ref[...], preferred_element_type=F32).astype(BF16)
    v_new = jnp.dot(latn_ref[...].astype(BF16), w_uv_ref[...], preferred_element_type=F32).astype(BF16)

    q = q_ref[...]
    qt = jnp.concatenate([q] * MLA_HEADS, axis=0)
    r_head = lax.broadcasted_iota(jnp.int32, qt.shape, 0) // n_new
    c_head = lax.broadcasted_iota(jnp.int32, qt.shape, 1) // LANES
    qm = jnp.where(r_head == c_head, qt, jnp.zeros_like(qt))

    nt = (((1,), (1,)), ((), ()))
    s_old = lax.dot_general(k_sc[...], qm, nt, preferred_element_type=F32) * scale
    s_new = lax.dot_general(kn_ref[...], qm, nt, preferred_element_type=F32) * scale
    mx = jnp.maximum(jnp.max(s_old, axis=0, keepdims=True), jnp.max(s_new, axis=0, keepdims=True))
    p_old = jnp.exp(s_old - mx)
    p_new = jnp.exp(s_new - mx)
    inv = 1.0 / (jnp.sum(p_old, axis=0, keepdims=True) + jnp.sum(p_new, axis=0, keepdims=True))
    p_old = (p_old * inv).astype(BF16)
    p_new = (p_new * inv).astype(BF16)
    tn = (((0,), (0,)), ((), ()))
    full = (lax.dot_general(p_old, v_all, tn, preferred_element_type=F32)
            + lax.dot_general(p_new, v_new, tn, preferred_element_type=F32))
    v_head = lax.broadcasted_iota(jnp.int32, (n_new, MLA_WIDTH), 1) // V_DIM
    out = jnp.zeros((n_new, MLA_WIDTH), F32)
    for h in range(MLA_HEADS):
        out = out + jnp.where(v_head == h, full[h * n_new:(h + 1) * n_new, :], 0.0)
    o_ref[...] = out.astype(o_ref.dtype)


def _attention_sample(q, k_new, lat_new, cache_lat, cache_kr, layer, w, n_new):
    _, b, past, _ = cache_lat.shape
    weights = [w["w_uk"], w["gkn"], w["w_uv"], w["mseg"], w["place"]]
    tok = lambda c: pl.BlockSpec((n_new, c), lambda i: (i, 0))
    return pl.pallas_call(
        functools.partial(_attn_sample_kernel, n_new=n_new),
        grid=(b,),
        in_specs=[tok(MLA_HEADS * LANES), tok(MLA_HEADS * LANES), tok(KV_LORA),
                  pl.BlockSpec((None, None, past, KV_LORA), lambda i: (layer, i, 0, 0)),
                  pl.BlockSpec((None, None, past, ROPE_DIM), lambda i: (layer, i, 0, 0))]
                 + [_full_spec(a) for a in weights],
        out_specs=tok(MLA_WIDTH),
        out_shape=jax.ShapeDtypeStruct((b * n_new, MLA_WIDTH), BF16),
        scratch_shapes=[pltpu.VMEM((past, MLA_HEADS * LANES), BF16)],
        compiler_params=pltpu.CompilerParams(dimension_semantics=("parallel",),
                                             vmem_limit_bytes=VMEM_LIMIT),
        name="attention_sample",
    )(q, k_new, lat_new, cache_lat, cache_kr, *weights)


CONV_PAD = SUBLANES


def _mlstm_kernel(qk_ref, v_ref, o_ref, misc_ref, conv0_ref, c0_ref, n0_ref, m0_ref, wconv_ref, bconv_ref,
                  bgate_ref, glstm_ref,
                  h_out, conv_out, c_out, n_out, m_out,
                  full_sc, c_sc, n_sc, m_sc, *, L):
    c = pl.program_id(1)
    last = pl.num_programs(1) - 1
    hist = CONV_W - 1
    lo = CONV_PAD - hist

    @pl.when(c == 0)
    def _init():
        full_sc[lo:CONV_PAD, :] = conv0_ref[...]
        c_sc[...] = c0_ref[...]
        n_sc[...] = n0_ref[...]
        m_sc[...] = m0_ref[...]

    full_sc[CONV_PAD:CONV_PAD + L, :] = qk_ref[...]
    y = bconv_ref[...]
    for j in range(CONV_W):
        y = y + full_sc[lo + j:lo + j + L, :] * wconv_ref[j:j + 1, :]
    qk = y * jax.nn.sigmoid(y)
    tail = full_sc[lo + L:CONV_PAD + L, :]
    full_sc[lo:CONV_PAD, :] = tail

    gs = misc_ref[...] + bgate_ref[...]
    lfs = jnp.minimum(gs, 0.0) - jnp.log1p(jnp.exp(-jnp.abs(gs)))
    row = lax.broadcasted_iota(jnp.int32, (L, L), 0)
    col = lax.broadcasted_iota(jnp.int32, (L, L), 1)
    causal = row >= col
    tri = jnp.where(causal, 1.0, 0.0).astype(BF16)
    lf_hi = lfs.astype(BF16)
    lf_mid = (lfs - lf_hi.astype(F32)).astype(BF16)
    lf_lo = (lfs - lf_hi.astype(F32) - lf_mid.astype(F32)).astype(BF16)
    b_slab = (jnp.dot(tri, lf_hi, preferred_element_type=F32)
              + jnp.dot(tri, lf_mid, preferred_element_type=F32)
              + jnp.dot(tri, lf_lo, preferred_element_type=F32))
    lane = lax.broadcasted_iota(jnp.int32, (L, LANES), 1)
    comb = jnp.where(lane < MISC_FG, gs, b_slab)
    comb_t = jnp.concatenate([comb, jnp.zeros((LANES - L, LANES), F32)], axis=0).T

    g_all = glstm_ref[...]
    for h in range(LSTM_HEADS):
        dk = slice(h * LSTM_DK, (h + 1) * LSTM_DK)
        dv = slice(h * LSTM_DV, (h + 1) * LSTM_DV)
        ig_col = gs[:, MISC_IG + h:MISC_IG + h + 1]
        b_col = b_slab[:, MISC_FG + h:MISC_FG + h + 1]
        ig_row = comb_t[MISC_IG + h:MISC_IG + h + 1, :L]
        b_row = comb_t[MISC_FG + h:MISC_FG + h + 1, :L]
        m_prev = m_sc[:, h:h + 1]

        log_d = jnp.where(causal, b_col - b_row + ig_row, -jnp.inf)
        inter = b_col + m_prev
        m_t = jnp.maximum(inter, jnp.max(log_d, axis=1, keepdims=True))
        decay = jnp.exp(log_d - m_t)
        inter_scale = jnp.exp(inter - m_t)

        qh = qk[:, dk]
        kh = qk[:, LSTM_HEADS * LSTM_DK + h * LSTM_DK:LSTM_HEADS * LSTM_DK + (h + 1) * LSTM_DK] * (LSTM_DK ** -0.5)
        vh = v_ref[:, dv]
        qb = qh.astype(BF16)
        kb = kh.astype(BF16)
        c_prev = c_sc[h]
        n_prev = n_sc[h:h + 1, :]
        wgt = lax.dot_general(qb, kb, (((1,), (1,)), ((), ())), preferred_element_type=F32) * decay
        cq = lax.dot_general(qb, c_prev.astype(BF16), (((1,), (1,)), ((), ())), preferred_element_type=F32)
        num = jnp.dot(wgt.astype(BF16), vh.astype(BF16), preferred_element_type=F32) + inter_scale * cq
        den = (jnp.sum(wgt, axis=1, keepdims=True)
               + inter_scale * jnp.sum(qh * n_prev, axis=1, keepdims=True))
        hid = num / jnp.maximum(jnp.abs(den), jnp.exp(-m_t))

        m_new = m_t[L - 1:L, :]
        b_last = b_col[L - 1:L, :]
        carry = jnp.exp(b_last + m_prev - m_new)
        d_end = jnp.exp(b_last - b_col + ig_col - m_new)
        upd = lax.dot_general((d_end * vh).astype(BF16), kb, (((0,), (0,)), ((), ())),
                              preferred_element_type=F32)
        c_sc[h] = carry * c_prev + upd
        n_sc[h:h + 1, :] = carry * n_prev + jnp.sum(d_end * kh, axis=0, keepdims=True)
        m_sc[:, h:h + 1] = m_new

        hn = _rms(hid, g_all[:, dv])
        h_out[:, dv] = (hn * jax.nn.sigmoid(o_ref[:, dv])).astype(h_out.dtype)

    @pl.when(c == last)
    def _finish():
        conv_out[...] = tail
        c_out[...] = c_sc[...]
        n_out[...] = n_sc[...]
        m_out[...] = m_sc[...]


def _mlstm(qk_raw, v_raw, o_raw, misc, conv0, c0, n0, m0, w, L):
    b = conv0.shape[0]
    t = qk_raw.shape[0]
    nc = t // (b * L)
    assert nc * b * L == t
    hist = CONV_W - 1
    tok = lambda cdim: pl.BlockSpec((L, cdim), lambda bi, ci: (bi * nc + ci, 0))
    m0 = m0.reshape(b, 1, LSTM_HEADS)
    weights = [w["w_conv"], w["b_conv"], w["bgate"], w["g_lstm_out"]]
    out_shape = (
        jax.ShapeDtypeStruct((t, LSTM_WIDTH), BF16),
        jax.ShapeDtypeStruct((b, hist, LSTM_QK), F32),
        jax.ShapeDtypeStruct((b, LSTM_HEADS, LSTM_DV, LSTM_DK), F32),
        jax.ShapeDtypeStruct((b, LSTM_HEADS, LSTM_DK), F32),
        jax.ShapeDtypeStruct((b, 1, LSTM_HEADS), F32),
    )
    state = lambda *dims: pl.BlockSpec((None,) + dims, lambda bi, ci: (bi,) + (0,) * len(dims))
    h, conv_new, c_new, n_new, m_new = pl.pallas_call(
        functools.partial(_mlstm_kernel, L=L),
        grid=(b, nc),
        in_specs=[tok(LSTM_QK), tok(LSTM_WIDTH), tok(LSTM_WIDTH), tok(LANES),
                  state(hist, LSTM_QK), state(LSTM_HEADS, LSTM_DV, LSTM_DK), state(LSTM_HEADS, LSTM_DK),
                  state(1, LSTM_HEADS)] + [_full_spec(a) for a in weights],
        out_specs=(tok(LSTM_WIDTH), state(hist, LSTM_QK), state(LSTM_HEADS, LSTM_DV, LSTM_DK),
                   state(LSTM_HEADS, LSTM_DK), state(1, LSTM_HEADS)),
        out_shape=out_shape,
        scratch_shapes=[pltpu.VMEM((CONV_PAD + L, LSTM_QK), F32),
                        pltpu.VMEM((LSTM_HEADS, LSTM_DV, LSTM_DK), F32),
                        pltpu.VMEM((LSTM_HEADS, LSTM_DK), F32),
                        pltpu.VMEM((1, LSTM_HEADS), F32)],
        compiler_params=pltpu.CompilerParams(dimension_semantics=("parallel", "arbitrary"),
                                             vmem_limit_bytes=VMEM_LIMIT),
        name="mlstm",
    )(qk_raw, v_raw, o_raw, misc, conv0, c0, n0, m0, *weights)
    return h, conv_new, c_new, n_new, m_new.reshape(b, LSTM_HEADS)


FF_CHUNK = 1024


def _out_ffn_kernel(x_ref, attn_ref, lstm_ref, w_out_ref, g_ffn_ref, w_up_ref, w_down_ref, y_ref):
    mix = jnp.concatenate([attn_ref[...], lstm_ref[...]], axis=1)
    x1 = x_ref[...] + jnp.dot(mix, w_out_ref[...], preferred_element_type=F32)
    xb = _rms(x1, g_ffn_ref[...]).astype(BF16)
    d_ff = w_up_ref.shape[1]
    acc = x1
    for f in range(d_ff // FF_CHUNK):
        cols = slice(f * FF_CHUNK, (f + 1) * FF_CHUNK)
        u = jnp.maximum(jnp.dot(xb, w_up_ref[:, cols], preferred_element_type=F32), 0.0)
        acc = acc + jnp.dot((u * u).astype(BF16), w_down_ref[cols, :], preferred_element_type=F32)
    y_ref[...] = acc


def _out_ffn(x2, attn, lstm, w, tm):
    t, d = x2.shape
    assert t % tm == 0
    row = lambda c: pl.BlockSpec((tm, c), lambda i: (i, 0))
    const = lambda a: pl.BlockSpec(a.shape, lambda i: (0, 0), pipeline_mode=pl.Buffered(1))
    weights = [w["w_out"], w["g_ffn"], w["w_up"], w["w_down"]]
    return pl.pallas_call(
        _out_ffn_kernel,
        grid=(t // tm,),
        in_specs=[row(d), row(MLA_WIDTH), row(LSTM_WIDTH)] + [const(a) for a in weights],
        out_specs=row(d),
        out_shape=jax.ShapeDtypeStruct((t, d), F32),
        compiler_params=pltpu.CompilerParams(dimension_semantics=("parallel",),
                                             vmem_limit_bytes=VMEM_LIMIT),
        name="out_ffn",
    )(x2, attn, lstm, *weights)


def _pack_layer(l, g_mix, w_in, g_q_lat, w_uq, g_q_nope, g_q_rope, g_kv_lat, g_k_rope, w_uk, g_k_nope, w_uv,
                w_conv, b_conv, b_igate, b_fgate, g_lstm_out, w_out, g_ffn, w_up, w_down):
    d = w_in.shape[1]
    wi = w_in[l]
    o = 0
    parts = {}
    for name, width in (("cq", Q_LORA), ("ckv", KV_LORA), ("kpe", ROPE_DIM), ("qk", LSTM_QK), ("v", LSTM_WIDTH),
                        ("o", LSTM_WIDTH), ("ig", LSTM_HEADS), ("fg", LSTM_HEADS)):
        parts[name] = wi[:, o:o + width]
        o += width
    misc = jnp.concatenate([parts["kpe"], _rot_half(parts["kpe"]), parts["ig"], parts["fg"],
                            jnp.zeros((d, LANES - MISC_FG - LSTM_HEADS), F32)], axis=1)
    w_main = jnp.concatenate([parts["cq"], parts["ckv"], parts["qk"], parts["v"], parts["o"], misc],
                             axis=1).astype(BF16)

    uq = w_uq[l].reshape(Q_LORA, MLA_HEADS, QK_DIM)
    uq = jnp.concatenate([uq, _rot_half(uq[..., NOPE_DIM:])], axis=-1).reshape(Q_LORA, MLA_HEADS * LANES)
    gq_head = jnp.concatenate([g_q_nope[l], g_q_rope[l], _rot_half(g_q_rope[l])])
    uk = w_uk[l].reshape(KV_LORA, MLA_HEADS, NOPE_DIM)
    uk = jnp.concatenate([uk, jnp.zeros_like(uk)], axis=-1).reshape(KV_LORA, MLA_HEADS * LANES)
    gkn_head = jnp.concatenate([g_k_nope[l], jnp.zeros((LANES - NOPE_DIM,), F32)])
    uv = w_uv[l].reshape(KV_LORA, MLA_HEADS, V_DIM)
    uv_pad = jnp.concatenate([uv, jnp.zeros_like(uv)], axis=-1).reshape(KV_LORA, MLA_HEADS * LANES)
    vones_head = jnp.concatenate([jnp.zeros((V_DIM,), F32), jnp.ones((LANES - V_DIM,), F32)])
    gkm = jnp.concatenate([g_k_rope[l], _rot_half(g_k_rope[l]), jnp.zeros((LANES - 2 * ROPE_DIM,), F32)])
    bgate = jnp.concatenate([jnp.zeros((MISC_IG,), F32), b_igate[l], b_fgate[l],
                             jnp.zeros((LANES - MISC_FG - LSTM_HEADS,), F32)])

    i = jnp.arange(HEAD_PAIR)
    same = (i[:, None] // LANES) == (i[None, :] // LANES)
    li, lj = i[:, None] % LANES, i[None, :] % LANES
    mseg = jnp.where(same & (li < NOPE_DIM) & (lj < NOPE_DIM), 1.0 / NOPE_DIM,
                     jnp.where(same & (li >= NOPE_DIM) & (li < QK_DIM) & (lj >= NOPE_DIM), 1.0 / ROPE_DIM, 0.0))
    a = jnp.arange(LANES)
    mmisc = jnp.where((a[:, None] < ROPE_DIM) & (a[None, :] < 2 * ROPE_DIM), 1.0 / ROPE_DIM, 0.0)
    r = jnp.arange(ROPE_DIM)
    c = jnp.arange(MLA_HEADS * LANES)
    place = ((c[None, :] % LANES) == (NOPE_DIM + r[:, None])).astype(BF16)

    row = lambda v: v.reshape(1, -1).astype(F32)
    return {
        "g_mix": row(g_mix[l]), "w_main": w_main, "g_q_lat": row(g_q_lat[l]), "w_uq": uq.astype(BF16),
        "gq": row(jnp.tile(gq_head, MLA_HEADS)), "g_kv_lat": row(g_kv_lat[l]), "gkm": row(gkm),
        "w_uk": uk.astype(BF16), "gkn": row(jnp.tile(gkn_head, MLA_HEADS)), "w_uv": w_uv[l].astype(BF16),
        "w_uv_pad": uv_pad.astype(BF16), "vones": row(jnp.tile(vones_head, MLA_HEADS)),
        "mseg": mseg.astype(BF16), "mmisc": mmisc.astype(BF16), "place": place,
        "w_conv": w_conv[l], "b_conv": row(b_conv[l]), "bgate": row(bgate), "g_lstm_out": row(g_lstm_out[l]),
        "w_out": w_out[l].astype(BF16), "g_ffn": row(g_ffn[l]), "w_up": w_up[l].astype(BF16),
        "w_down": w_down[l].astype(BF16),
    }


def _rope_tables(pos):
    half = ROPE_DIM // 2
    inv = ROPE_BASE ** (-jnp.arange(half, dtype=F32) / half)
    ang = pos.astype(F32)[:, None] * inv[None, :]
    cos2 = jnp.concatenate([jnp.cos(ang), jnp.cos(ang)], axis=1)
    sin2 = jnp.concatenate([-jnp.sin(ang), jnp.sin(ang)], axis=1)
    n = pos.shape[0]
    z = lambda c: jnp.zeros((n, c), F32)
    cq = jnp.concatenate([jnp.ones((n, NOPE_DIM), F32), cos2, z(LANES - QK_DIM)], axis=1)
    sq = jnp.concatenate([z(NOPE_DIM), sin2, z(LANES - QK_DIM)], axis=1)
    ck = jnp.concatenate([cos2, z(LANES - ROPE_DIM)], axis=1)
    sk = jnp.concatenate([sin2, z(LANES - ROPE_DIM)], axis=1)
    return cq, sq, ck, sk


def kernel(x_prompt, x_sample, cache_kv_latent, cache_k_rope, state_conv, state_C, state_n, state_m,
           g_mix, w_in, g_q_lat, w_uq, g_q_nope, g_q_rope, g_kv_lat, g_k_rope, w_uk, g_k_nope, w_uv,
           w_conv, b_conv, b_igate, b_fgate, g_lstm_out, w_out, g_ffn, w_up, w_down):
    depth = w_in.shape[0]
    bp, sp, d = x_prompt.shape
    bs, ls, _ = x_sample.shape
    past = cache_kv_latent.shape[2]
    hist = CONV_W - 1

    tabs_p = tuple(jnp.tile(t, (bp, 1)) for t in _rope_tables(jnp.arange(sp, dtype=jnp.int32)))
    tabs_s = tuple(jnp.tile(t, (bs, 1)) for t in _rope_tables(past + jnp.arange(ls, dtype=jnp.int32)))
    xp = x_prompt.reshape(bp * sp, d)
    xs = x_sample.reshape(bs * ls, d)
    zero_conv = jnp.zeros((bp, hist, LSTM_QK), F32)
    zero_c = jnp.zeros((bp, LSTM_HEADS, LSTM_DV, LSTM_DK), F32)
    zero_n = jnp.zeros((bp, LSTM_HEADS, LSTM_DK), F32)
    zero_m = jnp.zeros((bp, LSTM_HEADS), F32)

    outs = {k: [] for k in ("p_lat", "p_kr", "p_conv", "p_c", "p_n", "p_m",
                            "s_lat", "s_kr", "s_conv", "s_c", "s_n", "s_m")}
    for l in range(depth):
        w = _pack_layer(l, g_mix, w_in, g_q_lat, w_uq, g_q_nope, g_q_rope, g_kv_lat, g_k_rope, w_uk, g_k_nope,
                        w_uv, w_conv, b_conv, b_igate, b_fgate, g_lstm_out, w_out, g_ffn, w_up, w_down)
        q, k, v, lat, kr, qk_raw, v_raw, o_raw, misc = _projection(xp, tabs_p, w, tm=512)
        attn = _attention_prompt(q.reshape(bp, sp, -1), k.reshape(bp, sp, -1), v.reshape(bp, sp, -1),
                                 tq=ATTN_TQ, tk=ATTN_TK)
        h, conv_new, c_new, n_new, m_new = _mlstm(qk_raw, v_raw, o_raw, misc, zero_conv, zero_c, zero_n, zero_m,
                                                  w, L=CHUNK)
        xp = _out_ffn(xp, attn.reshape(bp * sp, -1), h, w, tm=512)
        outs["p_lat"].append(lat.reshape(bp, sp, KV_LORA))
        outs["p_kr"].append(kr.reshape(bp, sp, ROPE_DIM))
        outs["p_conv"].append(conv_new)
        outs["p_c"].append(c_new)
        outs["p_n"].append(n_new)
        outs["p_m"].append(m_new)
        q, k, v, lat, kr, qk_raw, v_raw, o_raw, misc = _projection(xs, tabs_s, w, tm=bs * ls)
        attn = _attention_sample(q, k, lat, cache_kv_latent, cache_k_rope, l, w, n_new=ls)
        h, conv_new, c_new, n_new, m_new = _mlstm(qk_raw, v_raw, o_raw, misc, state_conv[l], state_C[l],
                                                  state_n[l], state_m[l], w, L=ls)
        xs = _out_ffn(xs, attn, h, w, tm=bs * ls)
        outs["s_lat"].append(lat.reshape(bs, ls, KV_LORA))
        outs["s_kr"].append(kr.reshape(bs, ls, ROPE_DIM))
        outs["s_conv"].append(conv_new)
        outs["s_c"].append(c_new)
        outs["s_n"].append(n_new)
        outs["s_m"].append(m_new)

    st = lambda key: jnp.stack(outs[key])
    return (xp.reshape(bp, sp, d), xs.reshape(bs, ls, d),
            st("p_lat"), st("p_kr"), st("p_conv"), st("p_c"), st("p_n"), st("p_m"),
            st("s_lat"), st("s_kr"), st("s_conv"), st("s_c"), st("s_n"), st("s_m"))
```

```python
import functools

import jax
import jax.numpy as jnp
from jax import lax
from jax.experimental import pallas as pl
from jax.experimental.pallas import tpu as pltpu

F32 = jnp.float32
BF16 = jnp.bfloat16

EPS = 1e-6
CHUNK = 64
MLA_HEADS = 8
NOPE_DIM = 64
ROPE_DIM = 32
QK_DIM = NOPE_DIM + ROPE_DIM
V_DIM = 64
Q_LORA = 256
KV_LORA = 128
ROPE_BASE = 10000.0
LSTM_HEADS = 4
LSTM_DK = 128
LSTM_DV = 128
CONV_W = 4
LSTM_WIDTH = LSTM_HEADS * LSTM_DV
LSTM_QK = 2 * LSTM_HEADS * LSTM_DK
MLA_WIDTH = MLA_HEADS * V_DIM

LANES = 128
SUBLANES = 8
HEAD_PAIR = 2 * LANES
VMEM_LIMIT = 52 * 1024 * 1024
Q_LOG2_SCALE = (QK_DIM ** -0.5) * 1.4426950408889634
ATTN_TQ = 512
ATTN_TK = 512
MLSTM_TILE = 256

MISC_KPE = 0
MISC_KPE_ROT = ROPE_DIM
MISC_IG = 2 * ROPE_DIM
MISC_FG = MISC_IG + LSTM_HEADS

COL_CQ = (0, Q_LORA)
COL_CKV = (COL_CQ[1], COL_CQ[1] + KV_LORA)
COL_QK = (COL_CKV[1], COL_CKV[1] + LSTM_QK)
COL_V = (COL_QK[1], COL_QK[1] + LSTM_WIDTH)
COL_O = (COL_V[1], COL_V[1] + LSTM_WIDTH)
COL_MISC = (COL_O[1], COL_O[1] + LANES)
PACKED_COLS = COL_MISC[1]


def _rot_half(a):
    half = ROPE_DIM // 2
    return jnp.concatenate([a[..., half:], a[..., :half]], axis=-1)


def _rms(x, g):
    return x * lax.rsqrt(jnp.mean(x * x, axis=-1, keepdims=True) + EPS) * g


def _segment_mean_sq(y, m_ref):
    sq = y * y
    hi = sq.astype(BF16)
    lo = (sq - hi.astype(F32)).astype(BF16)
    m = m_ref[...]
    return (jnp.dot(hi, m, preferred_element_type=F32)
            + jnp.dot(lo, m, preferred_element_type=F32))


def _proj_kernel(x_ref, cq_ref, sq_ref, ck_ref, sk_ref, g_mix_ref, w_main_ref, g_qlat_ref, w_uq_ref,
                 gq_ref, g_kvlat_ref, gkm_ref, w_uk_ref, gkn_ref, w_uv_ref, vones_ref, mseg_ref, mmisc_ref,
                 q_out, k_out, v_out, lat_out, kr_out, qk_out, vl_out, o_out, misc_out):
    x = x_ref[...]
    xb = _rms(x, g_mix_ref[...]).astype(BF16)

    def proj(col):
        return jnp.dot(xb, w_main_ref[:, col[0]:col[1]], preferred_element_type=F32)

    qk_out[...] = proj(COL_QK)
    vl_out[...] = proj(COL_V)
    o_out[...] = proj(COL_O)
    misc = proj(COL_MISC)
    misc_out[...] = misc

    cqn = _rms(proj(COL_CQ), g_qlat_ref[...]).astype(BF16)
    cq_tab = cq_ref[...]
    sq_tab = sq_ref[...]
    for p in range(MLA_HEADS // 2):
        cols = slice(p * HEAD_PAIR, (p + 1) * HEAD_PAIR)
        raw = jnp.dot(cqn, w_uq_ref[:, cols], preferred_element_type=F32)
        y = raw * lax.rsqrt(_segment_mean_sq(raw, mseg_ref) + EPS) * gq_ref[:, cols]
        for j in range(2):
            yh = y[:, j * LANES:(j + 1) * LANES]
            qh = yh * cq_tab + pltpu.roll(yh, LANES - ROPE_DIM, 1) * sq_tab
            h = 2 * p + j
            q_out[:, h * LANES:(h + 1) * LANES] = (qh * Q_LOG2_SCALE).astype(BF16)

    lat = _rms(proj(COL_CKV), g_kvlat_ref[...])
    lat_out[...] = lat
    latb = lat.astype(BF16)
    v_out[...] = (jnp.dot(latb, w_uv_ref[...], preferred_element_type=F32) + vones_ref[...]).astype(BF16)

    ms = misc * lax.rsqrt(_segment_mean_sq(misc, mmisc_ref) + EPS) * gkm_ref[...]
    kr = ms * ck_ref[...] + pltpu.roll(ms, LANES - ROPE_DIM, 1) * sk_ref[...]
    kr_out[...] = kr[:, :ROPE_DIM]
    kr_placed = pltpu.roll(kr, NOPE_DIM, 1)
    kr_pair = jnp.concatenate([kr_placed, kr_placed], axis=1)
    for p in range(MLA_HEADS // 2):
        cols = slice(p * HEAD_PAIR, (p + 1) * HEAD_PAIR)
        raw = jnp.dot(latb, w_uk_ref[:, cols], preferred_element_type=F32)
        y = raw * lax.rsqrt(_segment_mean_sq(raw, mseg_ref) + EPS) * gkn_ref[:, cols]
        k_out[:, cols] = (y + kr_pair).astype(BF16)


def _full_spec(a):
    nd = a.ndim
    return pl.BlockSpec(a.shape, lambda *_: (0,) * nd)


def _projection(x2, tabs, w, tm):
    t = x2.shape[0]
    assert t % tm == 0
    row = lambda c: pl.BlockSpec((tm, c), lambda i: (i, 0))
    weights = [w["g_mix"], w["w_main"], w["g_q_lat"], w["w_uq"], w["gq"], w["g_kv_lat"], w["gkm"],
               w["w_uk"], w["gkn"], w["w_uv_pad"], w["vones"], w["mseg"], w["mmisc"]]
    out_shape = (
        jax.ShapeDtypeStruct((t, MLA_HEADS * LANES), BF16),
        jax.ShapeDtypeStruct((t, MLA_HEADS * LANES), BF16),
        jax.ShapeDtypeStruct((t, MLA_HEADS * LANES), BF16),
        jax.ShapeDtypeStruct((t, KV_LORA), F32),
        jax.ShapeDtypeStruct((t, ROPE_DIM), F32),
        jax.ShapeDtypeStruct((t, LSTM_QK), F32),
        jax.ShapeDtypeStruct((t, LSTM_WIDTH), F32),
        jax.ShapeDtypeStruct((t, LSTM_WIDTH), F32),
        jax.ShapeDtypeStruct((t, LANES), F32),
    )
    return pl.pallas_call(
        _proj_kernel,
        grid=(t // tm,),
        in_specs=[row(x2.shape[1])] + [row(LANES)] * 4 + [_full_spec(a) for a in weights],
        out_specs=tuple(row(s.shape[1]) for s in out_shape),
        out_shape=out_shape,
        compiler_params=pltpu.CompilerParams(dimension_semantics=("parallel",),
                                             vmem_limit_bytes=VMEM_LIMIT),
        name="projection",
    )(x2, *tabs, *weights)


def _attn_prompt_kernel(q_ref, k_ref, v_ref, o_ref, m_sc, acc_sc, sa_sc, sb_sc, *, tq, tk):
    qi = pl.program_id(2)
    m_sc[...] = jnp.full(m_sc.shape, -1e30, F32)
    acc_sc[...] = jnp.zeros(acc_sc.shape, F32)
    q_row0 = qi * tq

    def scores(k0, s_ref):
        for j in range(2):
            q = q_ref[:, j * LANES:(j + 1) * LANES]
            k = k_ref[pl.ds(k0, tk), j * LANES:(j + 1) * LANES]
            s_ref[j] = lax.dot_general(q, k, (((1,), (1,)), ((), ())), preferred_element_type=F32)

    def consume(k0, s_ref, masked):
        for j in range(2):
            v = v_ref[pl.ds(k0, tk), j * LANES:(j + 1) * LANES]
            s = s_ref[j]
            if masked:
                qc = (q_row0 + lax.broadcasted_iota(jnp.int32, (tq, tk), 0)) // CHUNK
                kc = (k0 + lax.broadcasted_iota(jnp.int32, (tq, tk), 1)) // CHUNK
                s = jnp.where(kc <= qc, s, -1e30)
            m_prev = m_sc[j]
            m_next = jnp.maximum(m_prev, jnp.max(s, axis=1, keepdims=True))
            alpha = jnp.exp2(m_prev - m_next)
            p = jnp.exp2(s - pltpu.repeat(m_next, tk // LANES, axis=1))
            m_sc[j] = m_next
            acc_sc[j] = alpha * acc_sc[j] + jnp.dot(p.astype(BF16), v, preferred_element_type=F32)

    n_full = q_row0 // tk
    at = lambda t: pl.multiple_of(t * tk, tk)
    scores(0, sa_sc)

    def body(i, carry):
        t = 2 * i
        scores(at(t + 1), sb_sc)
        consume(at(t), sa_sc, False)
        scores(at(t + 2), sa_sc)
        consume(at(t + 1), sb_sc, False)
        return carry

    lax.fori_loop(0, n_full // 2, body, 0)
    t0 = 2 * (n_full // 2)

    @pl.when(n_full % 2 == 1)
    def _odd_tail():
        scores(at(t0 + 1), sb_sc)
        consume(at(t0), sa_sc, False)
        consume(at(t0 + 1), sb_sc, True)

    @pl.when(n_full % 2 == 0)
    def _even_tail():
        consume(at(t0), sa_sc, True)

    acc0 = acc_sc[0]
    acc1 = acc_sc[1]
    out0 = acc0 / pltpu.roll(acc0, V_DIM, 1)
    out1 = pltpu.roll(acc1, V_DIM, 1) / acc1
    lane = lax.broadcasted_iota(jnp.int32, (tq, LANES), 1)
    o_ref[...] = jnp.where(lane < V_DIM, out0, out1).astype(o_ref.dtype)


def _attention_prompt(q, k, v, tq, tk):
    b, s, _ = q.shape
    assert tq == tk and s % tq == 0 and tq % CHUNK == 0
    pairs = MLA_HEADS // 2
    return pl.pallas_call(
        functools.partial(_attn_prompt_kernel, tq=tq, tk=tk),
        grid=(b, pairs, s // tq),
        in_specs=[
            pl.BlockSpec((None, tq, HEAD_PAIR), lambda bi, p, i: (bi, i, p)),
            pl.BlockSpec((None, s, HEAD_PAIR), lambda bi, p, i: (bi, 0, p)),
            pl.BlockSpec((None, s, HEAD_PAIR), lambda bi, p, i: (bi, 0, p)),
        ],
        out_specs=pl.BlockSpec((None, tq, LANES), lambda bi, p, i: (bi, i, p)),
        out_shape=jax.ShapeDtypeStruct((b, s, MLA_WIDTH), BF16),
        scratch_shapes=[pltpu.VMEM((2, tq, LANES), F32),
                        pltpu.VMEM((2, tq, LANES), F32),
                        pltpu.VMEM((2, tq, tk), F32),
                        pltpu.VMEM((2, tq, tk), F32)],
        compiler_params=pltpu.CompilerParams(
            dimension_semantics=("parallel", "parallel", "arbitrary"),
            vmem_limit_bytes=VMEM_LIMIT),
        name="attention_prompt",
    )(q, k, v)


def _attn_sample_kernel(q_ref, kn_ref, latn_ref, lat_ref, kr_ref, w_uk_ref, gkn_ref, w_uv_ref, mseg_ref,
                        place_ref, o_ref, k_sc, *, n_new):
    latb = lat_ref[...].astype(BF16)
    kr_all = jnp.dot(kr_ref[...].astype(BF16), place_ref[...], preferred_element_type=F32)
    for p in range(MLA_HEADS // 2):
        cols = slice(p * HEAD_PAIR, (p + 1) * HEAD_PAIR)
        raw = jnp.dot(latb, w_uk_ref[:, cols], preferred_element_type=F32)
        y = raw * lax.rsqrt(_segment_mean_sq(raw, mseg_ref) + EPS) * gkn_ref[:, cols]
        k_sc[:, cols] = (y + kr_all[:, cols]).astype(BF16)
    v_all = jnp.dot(latb, w_uv_ref[...], preferred_element_type=F32).astype(BF16)
    v_new = jnp.dot(latn_ref[...].astype(BF16), w_uv_ref[...], preferred_element_type=F32).astype(BF16)

    q = q_ref[...]
    qt = jnp.concatenate([q] * MLA_HEADS, axis=0)
    r_head = lax.broadcasted_iota(jnp.int32, qt.shape, 0) // n_new
    c_head = lax.broadcasted_iota(jnp.int32, qt.shape, 1) // LANES
    qm = jnp.where(r_head == c_head, qt, jnp.zeros_like(qt))

    nt = (((1,), (1,)), ((), ()))
    s_old = lax.dot_general(k_sc[...], qm, nt, preferred_element_type=F32)
    s_new = lax.dot_general(kn_ref[...], qm, nt, preferred_element_type=F32)
    mx = jnp.maximum(jnp.max(s_old, axis=0, keepdims=True), jnp.max(s_new, axis=0, keepdims=True))
    p_old = jnp.exp2(s_old - mx)
    p_new = jnp.exp2(s_new - mx)
    inv = 1.0 / (jnp.sum(p_old, axis=0, keepdims=True) + jnp.sum(p_new, axis=0, keepdims=True))
    p_old = (p_old * inv).astype(BF16)
    p_new = (p_new * inv).astype(BF16)
    tn = (((0,), (0,)), ((), ()))
    full = (lax.dot_general(p_old, v_all, tn, preferred_element_type=F32)
            + lax.dot_general(p_new, v_new, tn, preferred_element_type=F32))
    v_head = lax.broadcasted_iota(jnp.int32, (n_new, MLA_WIDTH), 1) // V_DIM
    out = jnp.zeros((n_new, MLA_WIDTH), F32)
    for h in range(MLA_HEADS):
        out = out + jnp.where(v_head == h, full[h * n_new:(h + 1) * n_new, :], 0.0)
    o_ref[...] = out.astype(o_ref.dtype)


def _attention_sample(q, k_new, lat_new, cache_lat, cache_kr, layer, w, n_new):
    _, b, past, _ = cache_lat.shape
    weights = [w["w_uk"], w["gkn"], w["w_uv"], w["mseg"], w["place"]]
    tok = lambda c: pl.BlockSpec((n_new, c), lambda i: (i, 0))
    return pl.pallas_call(
        functools.partial(_attn_sample_kernel, n_new=n_new),
        grid=(b,),
        in_specs=[tok(MLA_HEADS * LANES), tok(MLA_HEADS * LANES), tok(KV_LORA),
                  pl.BlockSpec((None, None, past, KV_LORA), lambda i: (layer, i, 0, 0)),
                  pl.BlockSpec((None, None, past, ROPE_DIM), lambda i: (layer, i, 0, 0))]
                 + [_full_spec(a) for a in weights],
        out_specs=tok(MLA_WIDTH),
        out_shape=jax.ShapeDtypeStruct((b * n_new, MLA_WIDTH), BF16),
        scratch_shapes=[pltpu.VMEM((past, MLA_HEADS * LANES), BF16)],
        compiler_params=pltpu.CompilerParams(dimension_semantics=("parallel",),
                                             vmem_limit_bytes=VMEM_LIMIT),
        name="attention_sample",
    )(q, k_new, lat_new, cache_lat, cache_kr, *weights)


CONV_PAD = SUBLANES


def _mlstm_kernel(qk_ref, v_ref, o_ref, misc_ref, conv0_ref, c0_ref, n0_ref, m0_ref, wconv_ref, bconv_ref,
                  bgate_ref, glstm_ref,
                  h_out, conv_out, c_out, n_out, m_out,
                  full_sc, c_sc, n_sc, m_sc, *, L):
    c = pl.program_id(1)
    last = pl.num_programs(1) - 1
    hist = CONV_W - 1
    lo = CONV_PAD - hist

    @pl.when(c == 0)
    def _init():
        full_sc[lo:CONV_PAD, :] = conv0_ref[...]
        c_sc[...] = c0_ref[...]
        n_sc[...] = n0_ref[...]
        m_sc[...] = m0_ref[...]

    full_sc[CONV_PAD:CONV_PAD + L, :] = qk_ref[...]
    y = bconv_ref[...]
    for j in range(CONV_W):
        y = y + full_sc[lo + j:lo + j + L, :] * wconv_ref[j:j + 1, :]
    qk = y * jax.nn.sigmoid(y)
    tail = full_sc[lo + L:CONV_PAD + L, :]
    full_sc[lo:CONV_PAD, :] = tail

    gs = misc_ref[...] + bgate_ref[...]
    lfs = jnp.minimum(gs, 0.0) - jnp.log1p(jnp.exp(-jnp.abs(gs)))
    row = lax.broadcasted_iota(jnp.int32, (L, L), 0)
    col = lax.broadcasted_iota(jnp.int32, (L, L), 1)
    causal = row >= col
    tri = jnp.where(causal, 1.0, 0.0).astype(BF16)
    lf_hi = lfs.astype(BF16)
    lf_mid = (lfs - lf_hi.astype(F32)).astype(BF16)
    lf_lo = (lfs - lf_hi.astype(F32) - lf_mid.astype(F32)).astype(BF16)
    b_slab = (jnp.dot(tri, lf_hi, preferred_element_type=F32)
              + jnp.dot(tri, lf_mid, preferred_element_type=F32)
              + jnp.dot(tri, lf_lo, preferred_element_type=F32))
    lane = lax.broadcasted_iota(jnp.int32, (L, LANES), 1)
    comb = jnp.where(lane < MISC_FG, gs, b_slab)
    if L < LANES:
        comb = jnp.concatenate([comb, jnp.zeros((LANES - L, LANES), F32)], axis=0)
    comb_t = comb.T

    g_all = glstm_ref[...]
    for h in range(LSTM_HEADS):
        dk = slice(h * LSTM_DK, (h + 1) * LSTM_DK)
        dv = slice(h * LSTM_DV, (h + 1) * LSTM_DV)
        ig_col = gs[:, MISC_IG + h:MISC_IG + h + 1]
        b_col = b_slab[:, MISC_FG + h:MISC_FG + h + 1]
        ig_row = comb_t[MISC_IG + h:MISC_IG + h + 1, :L]
        b_row = comb_t[MISC_FG + h:MISC_FG + h + 1, :L]
        m_prev = m_sc[:, h:h + 1]

        log_d = jnp.where(causal, b_col - b_row + ig_row, -jnp.inf)
        inter = b_col + m_prev
        m_t = jnp.maximum(inter, jnp.max(log_d, axis=1, keepdims=True))
        decay = jnp.exp(log_d - m_t)
        inter_scale = jnp.exp(inter - m_t)

        qh = qk[:, dk]
        kh = qk[:, LSTM_HEADS * LSTM_DK + h * LSTM_DK:LSTM_HEADS * LSTM_DK + (h + 1) * LSTM_DK] * (LSTM_DK ** -0.5)
        vh = v_ref[:, dv]
        qb = qh.astype(BF16)
        kb = kh.astype(BF16)
        c_prev = c_sc[h]
        n_prev = n_sc[h:h + 1, :]
        wgt = lax.dot_general(qb, kb, (((1,), (1,)), ((), ())), preferred_element_type=F32) * decay
        cq = lax.dot_general(qb, c_prev.astype(BF16), (((1,), (1,)), ((), ())), preferred_element_type=F32)
        num = jnp.dot(wgt.astype(BF16), vh.astype(BF16), preferred_element_type=F32) + inter_scale * cq
        den = (jnp.sum(wgt, axis=1, keepdims=True)
               + inter_scale * jnp.sum(qh * n_prev, axis=1, keepdims=True))
        hid = num / jnp.maximum(jnp.abs(den), jnp.exp(-m_t))

        m_new = m_t[L - 1:L, :]
        b_last = b_col[L - 1:L, :]
        carry = jnp.exp(b_last + m_prev - m_new)
        d_end = jnp.exp(b_last - b_col + ig_col - m_new)
        upd = lax.dot_general((d_end * vh).astype(BF16), kb, (((0,), (0,)), ((), ())),
                              preferred_element_type=F32)
        c_sc[h] = carry * c_prev + upd
        n_sc[h:h + 1, :] = carry * n_prev + jnp.sum(d_end * kh, axis=0, keepdims=True)
        m_sc[:, h:h + 1] = m_new

        hn = _rms(hid, g_all[:, dv])
        h_out[:, dv] = (hn * jax.nn.sigmoid(o_ref[:, dv])).astype(h_out.dtype)

    @pl.when(c == last)
    def _finish():
        conv_out[...] = tail
        c_out[...] = c_sc[...]
        n_out[...] = n_sc[...]
        m_out[...] = m_sc[...]


def _mlstm(qk_raw, v_raw, o_raw, misc, conv0, c0, n0, m0, w, L):
    b = conv0.shape[0]
    t = qk_raw.shape[0]
    nc = t // (b * L)
    assert nc * b * L == t
    hist = CONV_W - 1
    tok = lambda cdim: pl.BlockSpec((L, cdim), lambda bi, ci: (bi * nc + ci, 0))
    m0 = m0.reshape(b, 1, LSTM_HEADS)
    weights = [w["w_conv"], w["b_conv"], w["bgate"], w["g_lstm_out"]]
    out_shape = (
        jax.ShapeDtypeStruct((t, LSTM_WIDTH), BF16),
        jax.ShapeDtypeStruct((b, hist, LSTM_QK), F32),
        jax.ShapeDtypeStruct((b, LSTM_HEADS, LSTM_DV, LSTM_DK), F32),
        jax.ShapeDtypeStruct((b, LSTM_HEADS, LSTM_DK), F32),
        jax.ShapeDtypeStruct((b, 1, LSTM_HEADS), F32),
    )
    state = lambda *dims: pl.BlockSpec((None,) + dims, lambda bi, ci: (bi,) + (0,) * len(dims))
    h, conv_new, c_new, n_new, m_new = pl.pallas_call(
        functools.partial(_mlstm_kernel, L=L),
        grid=(b, nc),
        in_specs=[tok(LSTM_QK), tok(LSTM_WIDTH), tok(LSTM_WIDTH), tok(LANES),
                  state(hist, LSTM_QK), state(LSTM_HEADS, LSTM_DV, LSTM_DK), state(LSTM_HEADS, LSTM_DK),
                  state(1, LSTM_HEADS)] + [_full_spec(a) for a in weights],
        out_specs=(tok(LSTM_WIDTH), state(hist, LSTM_QK), state(LSTM_HEADS, LSTM_DV, LSTM_DK),
                   state(LSTM_HEADS, LSTM_DK), state(1, LSTM_HEADS)),
        out_shape=out_shape,
        scratch_shapes=[pltpu.VMEM((CONV_PAD + L, LSTM_QK), F32),
                        pltpu.VMEM((LSTM_HEADS, LSTM_DV, LSTM_DK), F32),
                        pltpu.VMEM((LSTM_HEADS, LSTM_DK), F32),
                        pltpu.VMEM((1, LSTM_HEADS), F32)],
        compiler_params=pltpu.CompilerParams(dimension_semantics=("parallel", "arbitrary"),
                                             vmem_limit_bytes=VMEM_LIMIT),
        name="mlstm",
    )(qk_raw, v_raw, o_raw, misc, conv0, c0, n0, m0, *weights)
    return h, conv_new, c_new, n_new, m_new.reshape(b, LSTM_HEADS)


FF_CHUNK = 1024


def _out_ffn_kernel(x_ref, attn_ref, lstm_ref, w_out_ref, g_ffn_ref, w_up_ref, w_down_ref, y_ref):
    mix = jnp.concatenate([attn_ref[...], lstm_ref[...]], axis=1)
    x1 = x_ref[...] + jnp.dot(mix, w_out_ref[...], preferred_element_type=F32)
    xb = _rms(x1, g_ffn_ref[...]).astype(BF16)
    d_ff = w_up_ref.shape[1]
    acc = x1
    for f in range(d_ff // FF_CHUNK):
        cols = slice(f * FF_CHUNK, (f + 1) * FF_CHUNK)
        u = jnp.maximum(jnp.dot(xb, w_up_ref[:, cols], preferred_element_type=F32), 0.0)
        acc = acc + jnp.dot((u * u).astype(BF16), w_down_ref[cols, :], preferred_element_type=F32)
    y_ref[...] = acc


def _out_ffn(x2, attn, lstm, w, tm):
    t, d = x2.shape
    assert t % tm == 0
    row = lambda c: pl.BlockSpec((tm, c), lambda i: (i, 0))
    const = lambda a: pl.BlockSpec(a.shape, lambda i: (0, 0), pipeline_mode=pl.Buffered(1))
    weights = [w["w_out"], w["g_ffn"], w["w_up"], w["w_down"]]
    return pl.pallas_call(
        _out_ffn_kernel,
        grid=(t // tm,),
        in_specs=[row(d), row(MLA_WIDTH), row(LSTM_WIDTH)] + [const(a) for a in weights],
        out_specs=row(d),
        out_shape=jax.ShapeDtypeStruct((t, d), F32),
        compiler_params=pltpu.CompilerParams(dimension_semantics=("parallel",),
                                             vmem_limit_bytes=VMEM_LIMIT),
        name="out_ffn",
    )(x2, attn, lstm, *weights)


def _pack_layer(l, g_mix, w_in, g_q_lat, w_uq, g_q_nope, g_q_rope, g_kv_lat, g_k_rope, w_uk, g_k_nope, w_uv,
                w_conv, b_conv, b_igate, b_fgate, g_lstm_out, w_out, g_ffn, w_up, w_down):
    d = w_in.shape[1]
    wi = w_in[l]
    o = 0
    parts = {}
    for name, width in (("cq", Q_LORA), ("ckv", KV_LORA), ("kpe", ROPE_DIM), ("qk", LSTM_QK), ("v", LSTM_WIDTH),
                        ("o", LSTM_WIDTH), ("ig", LSTM_HEADS), ("fg", LSTM_HEADS)):
        parts[name] = wi[:, o:o + width]
        o += width
    misc = jnp.concatenate([parts["kpe"], _rot_half(parts["kpe"]), parts["ig"], parts["fg"],
                            jnp.zeros((d, LANES - MISC_FG - LSTM_HEADS), F32)], axis=1)
    w_main = jnp.concatenate([parts["cq"], parts["ckv"], parts["qk"], parts["v"], parts["o"], misc],
                             axis=1).astype(BF16)

    uq = w_uq[l].reshape(Q_LORA, MLA_HEADS, QK_DIM)
    uq = jnp.concatenate([uq, _rot_half(uq[..., NOPE_DIM:])], axis=-1).reshape(Q_LORA, MLA_HEADS * LANES)
    gq_head = jnp.concatenate([g_q_nope[l], g_q_rope[l], _rot_half(g_q_rope[l])])
    uk = w_uk[l].reshape(KV_LORA, MLA_HEADS, NOPE_DIM)
    uk = jnp.concatenate([uk, jnp.zeros_like(uk)], axis=-1).reshape(KV_LORA, MLA_HEADS * LANES)
    gkn_head = jnp.concatenate([g_k_nope[l], jnp.zeros((LANES - NOPE_DIM,), F32)])
    uv = w_uv[l].reshape(KV_LORA, MLA_HEADS, V_DIM)
    uv_pad = jnp.concatenate([uv, jnp.zeros_like(uv)], axis=-1).reshape(KV_LORA, MLA_HEADS * LANES)
    vones_head = jnp.concatenate([jnp.zeros((V_DIM,), F32), jnp.ones((LANES - V_DIM,), F32)])
    gkm = jnp.concatenate([g_k_rope[l], _rot_half(g_k_rope[l]), jnp.zeros((LANES - 2 * ROPE_DIM,), F32)])
    bgate = jnp.concatenate([jnp.zeros((MISC_IG,), F32), b_igate[l], b_fgate[l],
                             jnp.zeros((LANES - MISC_FG - LSTM_HEADS,), F32)])

    i = jnp.arange(HEAD_PAIR)
    same = (i[:, None] // LANES) == (i[None, :] // LANES)
    li, lj = i[:, None] % LANES, i[None, :] % LANES
    mseg = jnp.where(same & (li < NOPE_DIM) & (lj < NOPE_DIM), 1.0 / NOPE_DIM,
                     jnp.where(same & (li >= NOPE_DIM) & (li < QK_DIM) & (lj >= NOPE_DIM), 1.0 / ROPE_DIM, 0.0))
    a = jnp.arange(LANES)
    mmisc = jnp.where((a[:, None] < ROPE_DIM) & (a[None, :] < 2 * ROPE_DIM), 1.0 / ROPE_DIM, 0.0)
    r = jnp.arange(ROPE_DIM)
    c = jnp.arange(MLA_HEADS * LANES)
    place = ((c[None, :] % LANES) == (NOPE_DIM + r[:, None])).astype(BF16)

    row = lambda v: v.reshape(1, -1).astype(F32)
    return {
        "g_mix": row(g_mix[l]), "w_main": w_main, "g_q_lat": row(g_q_lat[l]), "w_uq": uq.astype(BF16),
        "gq": row(jnp.tile(gq_head, MLA_HEADS)), "g_kv_lat": row(g_kv_lat[l]), "gkm": row(gkm),
        "w_uk": uk.astype(BF16), "gkn": row(jnp.tile(gkn_head, MLA_HEADS)), "w_uv": w_uv[l].astype(BF16),
        "w_uv_pad": uv_pad.astype(BF16), "vones": row(jnp.tile(vones_head, MLA_HEADS)),
        "mseg": mseg.astype(BF16), "mmisc": mmisc.astype(BF16), "place": place,
        "w_conv": w_conv[l], "b_conv": row(b_conv[l]), "bgate": row(bgate), "g_lstm_out": row(g_lstm_out[l]),
        "w_out": w_out[l].astype(BF16), "g_ffn": row(g_ffn[l]), "w_up": w_up[l].astype(BF16),
        "w_down": w_down[l].astype(BF16),
    }


def _rope_tables(pos):
    half = ROPE_DIM // 2
    inv = ROPE_BASE ** (-jnp.arange(half, dtype=F32) / half)
    ang = pos.astype(F32)[:, None] * inv[None, :]
    cos2 = jnp.concatenate([jnp.cos(ang), jnp.cos(ang)], axis=1)
    sin2 = jnp.concatenate([-jnp.sin(ang), jnp.sin(ang)], axis=1)
    n = pos.shape[0]
    z = lambda c: jnp.zeros((n, c), F32)
    cq = jnp.concatenate([jnp.ones((n, NOPE_DIM), F32), cos2, z(LANES - QK_DIM)], axis=1)
    sq = jnp.concatenate([z(NOPE_DIM), sin2, z(LANES - QK_DIM)], axis=1)
    ck = jnp.concatenate([cos2, z(LANES - ROPE_DIM)], axis=1)
    sk = jnp.concatenate([sin2, z(LANES - ROPE_DIM)], axis=1)
    return cq, sq, ck, sk


def kernel(x_prompt, x_sample, cache_kv_latent, cache_k_rope, state_conv, state_C, state_n, state_m,
           g_mix, w_in, g_q_lat, w_uq, g_q_nope, g_q_rope, g_kv_lat, g_k_rope, w_uk, g_k_nope, w_uv,
           w_conv, b_conv, b_igate, b_fgate, g_lstm_out, w_out, g_ffn, w_up, w_down):
    depth = w_in.shape[0]
    bp, sp, d = x_prompt.shape
    bs, ls, _ = x_sample.shape
    past = cache_kv_latent.shape[2]
    hist = CONV_W - 1

    tabs_p = tuple(jnp.tile(t, (bp, 1)) for t in _rope_tables(jnp.arange(sp, dtype=jnp.int32)))
    tabs_s = tuple(jnp.tile(t, (bs, 1)) for t in _rope_tables(past + jnp.arange(ls, dtype=jnp.int32)))
    xp = x_prompt.reshape(bp * sp, d)
    xs = x_sample.reshape(bs * ls, d)
    zero_conv = jnp.zeros((bp, hist, LSTM_QK), F32)
    zero_c = jnp.zeros((bp, LSTM_HEADS, LSTM_DV, LSTM_DK), F32)
    zero_n = jnp.zeros((bp, LSTM_HEADS, LSTM_DK), F32)
    zero_m = jnp.zeros((bp, LSTM_HEADS), F32)

    outs = {k: [] for k in ("p_lat", "p_kr", "p_conv", "p_c", "p_n", "p_m",
                            "s_lat", "s_kr", "s_conv", "s_c", "s_n", "s_m")}
    for l in range(depth):
        w = _pack_layer(l, g_mix, w_in, g_q_lat, w_uq, g_q_nope, g_q_rope, g_kv_lat, g_k_rope, w_uk, g_k_nope,
                        w_uv, w_conv, b_conv, b_igate, b_fgate, g_lstm_out, w_out, g_ffn, w_up, w_down)
        q, k, v, lat, kr, qk_raw, v_raw, o_raw, misc = _projection(xp, tabs_p, w, tm=512)
        attn = _attention_prompt(q.reshape(bp, sp, -1), k.reshape(bp, sp, -1), v.reshape(bp, sp, -1),
                                 tq=ATTN_TQ, tk=ATTN_TK)
        h, conv_new, c_new, n_new, m_new = _mlstm(qk_raw, v_raw, o_raw, misc, zero_conv, zero_c, zero_n, zero_m,
                                                  w, L=MLSTM_TILE)
        xp = _out_ffn(xp, attn.reshape(bp * sp, -1), h, w, tm=512)
        outs["p_lat"].append(lat.reshape(bp, sp, KV_LORA))
        outs["p_kr"].append(kr.reshape(bp, sp, ROPE_DIM))
        outs["p_conv"].append(conv_new)
        outs["p_c"].append(c_new)
        outs["p_n"].append(n_new)
        outs["p_m"].append(m_new)
        q, k, v, lat, kr, qk_raw, v_raw, o_raw, misc = _projection(xs, tabs_s, w, tm=bs * ls)
        attn = _attention_sample(q, k, lat, cache_kv_latent, cache_k_rope, l, w, n_new=ls)
        h, conv_new, c_new, n_new, m_new = _mlstm(qk_raw, v_raw, o_raw, misc, state_conv[l], state_C[l],
                                                  state_n[l], state_m[l], w, L=ls)
        xs = _out_ffn(xs, attn, h, w, tm=bs * ls)
        outs["s_lat"].append(lat.reshape(bs, ls, KV_LORA))
        outs["s_kr"].append(kr.reshape(bs, ls, ROPE_DIM))
        outs["s_conv"].append(conv_new)
        outs["s_c"].append(c_new)
        outs["s_n"].append(n_new)
        outs["s_m"].append(m_new)

    st = lambda key: jnp.stack(outs[key])
    return (xp.reshape(bp, sp, d), xs.reshape(bs, ls, d),
            st("p_lat"), st("p_kr"), st("p_conv"), st("p_c"), st("p_n"), st("p_m"),
            st("s_lat"), st("s_kr"), st("s_conv"), st("s_c"), st("s_n"), st("s_m"))
```

```python
import functools

import jax
import jax.numpy as jnp
from jax import lax
from jax.experimental import pallas as pl
from jax.experimental.pallas import tpu as pltpu

F32 = jnp.float32
BF16 = jnp.bfloat16

EPS = 1e-6
CHUNK = 64
MLA_HEADS = 8
NOPE_DIM = 64
ROPE_DIM = 32
QK_DIM = NOPE_DIM + ROPE_DIM
V_DIM = 64
Q_LORA = 256
KV_LORA = 128
ROPE_BASE = 10000.0
LSTM_HEADS = 4
LSTM_DK = 128
LSTM_DV = 128
CONV_W = 4
LSTM_WIDTH = LSTM_HEADS * LSTM_DV
LSTM_QK = 2 * LSTM_HEADS * LSTM_DK
MLA_WIDTH = MLA_HEADS * V_DIM

LANES = 128
SUBLANES = 8
HEAD_PAIR = 2 * LANES
VMEM_LIMIT = 52 * 1024 * 1024
Q_LOG2_SCALE = (QK_DIM ** -0.5) * 1.4426950408889634
ATTN_TQ = 512
ATTN_TK = 512
MLSTM_TILE = 256
VT_ROWS = 80

MISC_KPE = 0
MISC_KPE_ROT = ROPE_DIM
MISC_IG = 2 * ROPE_DIM
MISC_FG = MISC_IG + LSTM_HEADS

COL_CQ = (0, Q_LORA)
COL_CKV = (COL_CQ[1], COL_CQ[1] + KV_LORA)
COL_QK = (COL_CKV[1], COL_CKV[1] + LSTM_QK)
COL_V = (COL_QK[1], COL_QK[1] + LSTM_WIDTH)
COL_O = (COL_V[1], COL_V[1] + LSTM_WIDTH)
COL_MISC = (COL_O[1], COL_O[1] + LANES)
PACKED_COLS = COL_MISC[1]


def _rot_half(a):
    half = ROPE_DIM // 2
    return jnp.concatenate([a[..., half:], a[..., :half]], axis=-1)


def _rms(x, g):
    return x * lax.rsqrt(jnp.mean(x * x, axis=-1, keepdims=True) + EPS) * g


def _segment_mean_sq(y, m_ref):
    sq = y * y
    hi = sq.astype(BF16)
    lo = (sq - hi.astype(F32)).astype(BF16)
    m = m_ref[...]
    return (jnp.dot(hi, m, preferred_element_type=F32)
            + jnp.dot(lo, m, preferred_element_type=F32))


def _proj_kernel(x_ref, cq_ref, sq_ref, ck_ref, sk_ref, g_mix_ref, w_main_ref, g_qlat_ref, w_uq_ref,
                 gq_ref, g_kvlat_ref, gkm_ref, w_uk_ref, gkn_ref, w_uv_ref, vones_ref, mseg_ref, mmisc_ref,
                 q_out, k_out, v_out, lat_out, kr_out, qk_out, vl_out, o_out, misc_out):
    x = x_ref[...]
    xb = _rms(x, g_mix_ref[...]).astype(BF16)

    def proj(col):
        return jnp.dot(xb, w_main_ref[:, col[0]:col[1]], preferred_element_type=F32)

    qk_out[...] = proj(COL_QK)
    vl_out[...] = proj(COL_V)
    o_out[...] = proj(COL_O)
    misc = proj(COL_MISC)
    misc_out[...] = misc

    cqn = _rms(proj(COL_CQ), g_qlat_ref[...]).astype(BF16)
    cq_tab = cq_ref[...]
    sq_tab = sq_ref[...]
    for p in range(MLA_HEADS // 2):
        cols = slice(p * HEAD_PAIR, (p + 1) * HEAD_PAIR)
        raw = jnp.dot(cqn, w_uq_ref[:, cols], preferred_element_type=F32)
        y = raw * lax.rsqrt(_segment_mean_sq(raw, mseg_ref) + EPS) * gq_ref[:, cols]
        for j in range(2):
            yh = y[:, j * LANES:(j + 1) * LANES]
            qh = yh * cq_tab + pltpu.roll(yh, LANES - ROPE_DIM, 1) * sq_tab
            h = 2 * p + j
            q_out[:, h * LANES:(h + 1) * LANES] = (qh * Q_LOG2_SCALE).astype(BF16)

    lat = _rms(proj(COL_CKV), g_kvlat_ref[...])
    lat_out[...] = lat
    latb = lat.astype(BF16)
    lat_t = lat.T.astype(BF16)
    v_out[...] = (jnp.dot(w_uv_ref[...], lat_t, preferred_element_type=F32) + vones_ref[...]).astype(BF16)

    ms = misc * lax.rsqrt(_segment_mean_sq(misc, mmisc_ref) + EPS) * gkm_ref[...]
    kr = ms * ck_ref[...] + pltpu.roll(ms, LANES - ROPE_DIM, 1) * sk_ref[...]
    kr_out[...] = kr[:, :ROPE_DIM]
    kr_placed = pltpu.roll(kr, NOPE_DIM, 1)
    kr_pair = jnp.concatenate([kr_placed, kr_placed], axis=1)
    for p in range(MLA_HEADS // 2):
        cols = slice(p * HEAD_PAIR, (p + 1) * HEAD_PAIR)
        raw = jnp.dot(latb, w_uk_ref[:, cols], preferred_element_type=F32)
        y = raw * lax.rsqrt(_segment_mean_sq(raw, mseg_ref) + EPS) * gkn_ref[:, cols]
        k_out[:, cols] = (y + kr_pair).astype(BF16)


def _full_spec(a):
    nd = a.ndim
    return pl.BlockSpec(a.shape, lambda *_: (0,) * nd)


def _projection(x2, tabs, w, tm):
    t = x2.shape[0]
    assert t % tm == 0
    row = lambda c: pl.BlockSpec((tm, c), lambda i: (i, 0))
    weights = [w["g_mix"], w["w_main"], w["g_q_lat"], w["w_uq"], w["gq"], w["g_kv_lat"], w["gkm"],
               w["w_uk"], w["gkn"], w["w_uv_t"], w["vones"], w["mseg"], w["mmisc"]]
    out_shape = (
        jax.ShapeDtypeStruct((t, MLA_HEADS * LANES), BF16),
        jax.ShapeDtypeStruct((t, MLA_HEADS * LANES), BF16),
        jax.ShapeDtypeStruct((MLA_HEADS * VT_ROWS, t), BF16),
        jax.ShapeDtypeStruct((t, KV_LORA), F32),
        jax.ShapeDtypeStruct((t, ROPE_DIM), F32),
        jax.ShapeDtypeStruct((t, LSTM_QK), F32),
        jax.ShapeDtypeStruct((t, LSTM_WIDTH), F32),
        jax.ShapeDtypeStruct((t, LSTM_WIDTH), F32),
        jax.ShapeDtypeStruct((t, LANES), F32),
    )
    return pl.pallas_call(
        _proj_kernel,
        grid=(t // tm,),
        in_specs=[row(x2.shape[1])] + [row(LANES)] * 4 + [_full_spec(a) for a in weights],
        out_specs=tuple(pl.BlockSpec((MLA_HEADS * VT_ROWS, tm), lambda i: (0, i)) if n == 2 else row(s.shape[1])
                        for n, s in enumerate(out_shape)),
        out_shape=out_shape,
        compiler_params=pltpu.CompilerParams(dimension_semantics=("parallel",),
                                             vmem_limit_bytes=VMEM_LIMIT),
        name="projection",
    )(x2, *tabs, *weights)


def _attn_prompt_kernel(q_ref, k_ref, vt_ref, o_ref, qt_sc, m_sc, acc_sc, sa_sc, sb_sc, *, tq, tk):
    qi = pl.program_id(2)
    m_sc[...] = jnp.full(m_sc.shape, -1e30, F32)
    acc_sc[...] = jnp.zeros(acc_sc.shape, F32)
    q_col0 = qi * tq
    for j in range(2):
        qt_sc[j] = q_ref[:, j * LANES:(j + 1) * LANES].astype(F32).T.astype(BF16)

    def scores(k0, s_ref):
        for j in range(2):
            k = k_ref[pl.ds(k0, tk), j * LANES:(j + 1) * LANES]
            s_ref[j] = jnp.dot(k, qt_sc[j], preferred_element_type=F32)

    def consume(k0, s_ref, masked):
        for j in range(2):
            vt = vt_ref[j * VT_ROWS:(j + 1) * VT_ROWS, pl.ds(k0, tk)]
            s = s_ref[j]
            if masked:
                kc = (k0 + lax.broadcasted_iota(jnp.int32, (tk, tq), 0)) // CHUNK
                qc = (q_col0 + lax.broadcasted_iota(jnp.int32, (tk, tq), 1)) // CHUNK
                s = jnp.where(kc <= qc, s, -1e30)
            m_prev = m_sc[j]
            m_next = jnp.maximum(m_prev, jnp.max(s, axis=0, keepdims=True))
            alpha = jnp.exp2(m_prev - m_next)
            p = jnp.exp2(s - m_next)
            m_sc[j] = m_next
            acc_sc[j] = alpha * acc_sc[j] + jnp.dot(vt, p.astype(BF16), preferred_element_type=F32)

    n_full = q_col0 // tk
    at = lambda t: pl.multiple_of(t * tk, tk)
    scores(0, sa_sc)

    def body(i, carry):
        t = 2 * i
        scores(at(t + 1), sb_sc)
        consume(at(t), sa_sc, False)
        scores(at(t + 2), sa_sc)
        consume(at(t + 1), sb_sc, False)
        return carry

    lax.fori_loop(0, n_full // 2, body, 0)
    t0 = 2 * (n_full // 2)

    @pl.when(n_full % 2 == 1)
    def _odd_tail():
        scores(at(t0 + 1), sb_sc)
        consume(at(t0), sa_sc, False)
        consume(at(t0 + 1), sb_sc, True)

    @pl.when(n_full % 2 == 0)
    def _even_tail():
        consume(at(t0), sa_sc, True)

    outs = []
    for j in range(2):
        acc = acc_sc[j]
        outs.append(acc[:V_DIM, :] / acc[V_DIM:V_DIM + 1, :])
    o_ref[...] = jnp.concatenate(outs, axis=0).T.astype(o_ref.dtype)


def _attention_prompt(q, k, vt, tq, tk):
    b, s, _ = q.shape
    assert tq == tk and s % tq == 0 and tq % CHUNK == 0
    pairs = MLA_HEADS // 2
    return pl.pallas_call(
        functools.partial(_attn_prompt_kernel, tq=tq, tk=tk),
        grid=(b, pairs, s // tq),
        in_specs=[
            pl.BlockSpec((None, tq, HEAD_PAIR), lambda bi, p, i: (bi, i, p)),
            pl.BlockSpec((None, s, HEAD_PAIR), lambda bi, p, i: (bi, 0, p)),
            pl.BlockSpec((2 * VT_ROWS, s), lambda bi, p, i: (p, bi)),
        ],
        out_specs=pl.BlockSpec((None, tq, LANES), lambda bi, p, i: (bi, i, p)),
        out_shape=jax.ShapeDtypeStruct((b, s, MLA_WIDTH), BF16),
        scratch_shapes=[pltpu.VMEM((2, LANES, tq), BF16),
                        pltpu.VMEM((2, 1, tq), F32),
                        pltpu.VMEM((2, VT_ROWS, tq), F32),
                        pltpu.VMEM((2, tk, tq), F32),
                        pltpu.VMEM((2, tk, tq), F32)],
        compiler_params=pltpu.CompilerParams(
            dimension_semantics=("parallel", "parallel", "arbitrary"),
            vmem_limit_bytes=VMEM_LIMIT),
        name="attention_prompt",
    )(q, k, vt)


def _attn_sample_kernel(q_ref, kn_ref, latn_ref, lat_ref, kr_ref, w_uk_ref, gkn_ref, w_uv_ref, mseg_ref,
                        place_ref, o_ref, k_sc, *, n_new):
    latb = lat_ref[...].astype(BF16)
    kr_all = jnp.dot(kr_ref[...].astype(BF16), place_ref[...], preferred_element_type=F32)
    for p in range(MLA_HEADS // 2):
        cols = slice(p * HEAD_PAIR, (p + 1) * HEAD_PAIR)
        raw = jnp.dot(latb, w_uk_ref[:, cols], preferred_element_type=F32)
        y = raw * lax.rsqrt(_segment_mean_sq(raw, mseg_ref) + EPS) * gkn_ref[:, cols]
        k_sc[:, cols] = (y + kr_all[:, cols]).astype(BF16)
    v_all = jnp.dot(latb, w_uv_ref[...], preferred_element_type=F32).astype(BF16)
    v_new = jnp.dot(latn_ref[...].astype(BF16), w_uv_ref[...], preferred_element_type=F32).astype(BF16)

    q = q_ref[...]
    qt = jnp.concatenate([q] * MLA_HEADS, axis=0)
    r_head = lax.broadcasted_iota(jnp.int32, qt.shape, 0) // n_new
    c_head = lax.broadcasted_iota(jnp.int32, qt.shape, 1) // LANES
    qm = jnp.where(r_head == c_head, qt, jnp.zeros_like(qt))

    nt = (((1,), (1,)), ((), ()))
    s_old = lax.dot_general(k_sc[...], qm, nt, preferred_element_type=F32)
    s_new = lax.dot_general(kn_ref[...], qm, nt, preferred_element_type=F32)
    mx = jnp.maximum(jnp.max(s_old, axis=0, keepdims=True), jnp.max(s_new, axis=0, keepdims=True))
    p_old = jnp.exp2(s_old - mx)
    p_new = jnp.exp2(s_new - mx)
    inv = 1.0 / (jnp.sum(p_old, axis=0, keepdims=True) + jnp.sum(p_new, axis=0, keepdims=True))
    p_old = (p_old * inv).astype(BF16)
    p_new = (p_new * inv).astype(BF16)
    tn = (((0,), (0,)), ((), ()))
    full = (lax.dot_general(p_old, v_all, tn, preferred_element_type=F32)
            + lax.dot_general(p_new, v_new, tn, preferred_element_type=F32))
    v_head = lax.broadcasted_iota(jnp.int32, (n_new, MLA_WIDTH), 1) // V_DIM
    out = jnp.zeros((n_new, MLA_WIDTH), F32)
    for h in range(MLA_HEADS):
        out = out + jnp.where(v_head == h, full[h * n_new:(h + 1) * n_new, :], 0.0)
    o_ref[...] = out.astype(o_ref.dtype)


def _attention_sample(q, k_new, lat_new, cache_lat, cache_kr, layer, w, n_new):
    _, b, past, _ = cache_lat.shape
    weights = [w["w_uk"], w["gkn"], w["w_uv"], w["mseg"], w["place"]]
    tok = lambda c: pl.BlockSpec((n_new, c), lambda i: (i, 0))
    return pl.pallas_call(
        functools.partial(_attn_sample_kernel, n_new=n_new),
        grid=(b,),
        in_specs=[tok(MLA_HEADS * LANES), tok(MLA_HEADS * LANES), tok(KV_LORA),
                  pl.BlockSpec((None, None, past, KV_LORA), lambda i: (layer, i, 0, 0)),
                  pl.BlockSpec((None, None, past, ROPE_DIM), lambda i: (layer, i, 0, 0))]
                 + [_full_spec(a) for a in weights],
        out_specs=tok(MLA_WIDTH),
        out_shape=jax.ShapeDtypeStruct((b * n_new, MLA_WIDTH), BF16),
        scratch_shapes=[pltpu.VMEM((past, MLA_HEADS * LANES), BF16)],
        compiler_params=pltpu.CompilerParams(dimension_semantics=("parallel",),
                                             vmem_limit_bytes=VMEM_LIMIT),
        name="attention_sample",
    )(q, k_new, lat_new, cache_lat, cache_kr, *weights)


CONV_PAD = SUBLANES


def _mlstm_kernel(qk_ref, v_ref, o_ref, misc_ref, conv0_ref, c0_ref, n0_ref, m0_ref, wconv_ref, bconv_ref,
                  bgate_ref, glstm_ref,
                  h_out, conv_out, c_out, n_out, m_out,
                  full_sc, c_sc, n_sc, m_sc, *, L):
    c = pl.program_id(1)
    last = pl.num_programs(1) - 1
    hist = CONV_W - 1
    lo = CONV_PAD - hist

    @pl.when(c == 0)
    def _init():
        full_sc[lo:CONV_PAD, :] = conv0_ref[...]
        c_sc[...] = c0_ref[...]
        n_sc[...] = n0_ref[...]
        m_sc[...] = m0_ref[...]

    full_sc[CONV_PAD:CONV_PAD + L, :] = qk_ref[...]
    y = bconv_ref[...]
    for j in range(CONV_W):
        y = y + full_sc[lo + j:lo + j + L, :] * wconv_ref[j:j + 1, :]
    qk = y * jax.nn.sigmoid(y)
    tail = full_sc[lo + L:CONV_PAD + L, :]
    full_sc[lo:CONV_PAD, :] = tail

    gs = misc_ref[...] + bgate_ref[...]
    lfs = jnp.minimum(gs, 0.0) - jnp.log1p(jnp.exp(-jnp.abs(gs)))
    row = lax.broadcasted_iota(jnp.int32, (L, L), 0)
    col = lax.broadcasted_iota(jnp.int32, (L, L), 1)
    causal = row >= col
    tri = jnp.where(causal, 1.0, 0.0).astype(BF16)
    lf_hi = lfs.astype(BF16)
    lf_mid = (lfs - lf_hi.astype(F32)).astype(BF16)
    lf_lo = (lfs - lf_hi.astype(F32) - lf_mid.astype(F32)).astype(BF16)
    b_slab = (jnp.dot(tri, lf_hi, preferred_element_type=F32)
              + jnp.dot(tri, lf_mid, preferred_element_type=F32)
              + jnp.dot(tri, lf_lo, preferred_element_type=F32))
    lane = lax.broadcasted_iota(jnp.int32, (L, LANES), 1)
    comb = jnp.where(lane < MISC_FG, gs, b_slab)
    if L < LANES:
        comb = jnp.concatenate([comb, jnp.zeros((LANES - L, LANES), F32)], axis=0)
    comb_t = comb.T

    g_all = glstm_ref[...]
    for h in range(LSTM_HEADS):
        dk = slice(h * LSTM_DK, (h + 1) * LSTM_DK)
        dv = slice(h * LSTM_DV, (h + 1) * LSTM_DV)
        ig_col = gs[:, MISC_IG + h:MISC_IG + h + 1]
        b_col = b_slab[:, MISC_FG + h:MISC_FG + h + 1]
        ig_row = comb_t[MISC_IG + h:MISC_IG + h + 1, :L]
        b_row = comb_t[MISC_FG + h:MISC_FG + h + 1, :L]
        m_prev = m_sc[:, h:h + 1]

        log_d = jnp.where(causal, b_col - b_row + ig_row, -jnp.inf)
        inter = b_col + m_prev
        m_t = jnp.maximum(inter, jnp.max(log_d, axis=1, keepdims=True))
        decay = jnp.exp(log_d - m_t)
        inter_scale = jnp.exp(inter - m_t)

        qh = qk[:, dk]
        kh = qk[:, LSTM_HEADS * LSTM_DK + h * LSTM_DK:LSTM_HEADS * LSTM_DK + (h + 1) * LSTM_DK] * (LSTM_DK ** -0.5)
        vh = v_ref[:, dv]
        qb = qh.astype(BF16)
        kb = kh.astype(BF16)
        c_prev = c_sc[h]
        n_prev = n_sc[h:h + 1, :]
        wgt = lax.dot_general(qb, kb, (((1,), (1,)), ((), ())), preferred_element_type=F32) * decay
        cq = lax.dot_general(qb, c_prev.astype(BF16), (((1,), (1,)), ((), ())), preferred_element_type=F32)
        num = jnp.dot(wgt.astype(BF16), vh.astype(BF16), preferred_element_type=F32) + inter_scale * cq
        den = (jnp.sum(wgt, axis=1, keepdims=True)
               + inter_scale * jnp.sum(qh * n_prev, axis=1, keepdims=True))
        hid = num / jnp.maximum(jnp.abs(den), jnp.exp(-m_t))

        m_new = m_t[L - 1:L, :]
        b_last = b_col[L - 1:L, :]
        carry = jnp.exp(b_last + m_prev - m_new)
        d_end = jnp.exp(b_last - b_col + ig_col - m_new)
        upd = lax.dot_general((d_end * vh).astype(BF16), kb, (((0,), (0,)), ((), ())),
                              preferred_element_type=F32)
        c_sc[h] = carry * c_prev + upd
        n_sc[h:h + 1, :] = carry * n_prev + jnp.sum(d_end * kh, axis=0, keepdims=True)
        m_sc[:, h:h + 1] = m_new

        hn = _rms(hid, g_all[:, dv])
        h_out[:, dv] = (hn * jax.nn.sigmoid(o_ref[:, dv])).astype(h_out.dtype)

    @pl.when(c == last)
    def _finish():
        conv_out[...] = tail
        c_out[...] = c_sc[...]
        n_out[...] = n_sc[...]
        m_out[...] = m_sc[...]


def _mlstm(qk_raw, v_raw, o_raw, misc, conv0, c0, n0, m0, w, L):
    b = conv0.shape[0]
    t = qk_raw.shape[0]
    nc = t // (b * L)
    assert nc * b * L == t
    hist = CONV_W - 1
    tok = lambda cdim: pl.BlockSpec((L, cdim), lambda bi, ci: (bi * nc + ci, 0))
    m0 = m0.reshape(b, 1, LSTM_HEADS)
    weights = [w["w_conv"], w["b_conv"], w["bgate"], w["g_lstm_out"]]
    out_shape = (
        jax.ShapeDtypeStruct((t, LSTM_WIDTH), BF16),
        jax.ShapeDtypeStruct((b, hist, LSTM_QK), F32),
        jax.ShapeDtypeStruct((b, LSTM_HEADS, LSTM_DV, LSTM_DK), F32),
        jax.ShapeDtypeStruct((b, LSTM_HEADS, LSTM_DK), F32),
        jax.ShapeDtypeStruct((b, 1, LSTM_HEADS), F32),
    )
    state = lambda *dims: pl.BlockSpec((None,) + dims, lambda bi, ci: (bi,) + (0,) * len(dims))
    h, conv_new, c_new, n_new, m_new = pl.pallas_call(
        functools.partial(_mlstm_kernel, L=L),
        grid=(b, nc),
        in_specs=[tok(LSTM_QK), tok(LSTM_WIDTH), tok(LSTM_WIDTH), tok(LANES),
                  state(hist, LSTM_QK), state(LSTM_HEADS, LSTM_DV, LSTM_DK), state(LSTM_HEADS, LSTM_DK),
                  state(1, LSTM_HEADS)] + [_full_spec(a) for a in weights],
        out_specs=(tok(LSTM_WIDTH), state(hist, LSTM_QK), state(LSTM_HEADS, LSTM_DV, LSTM_DK),
                   state(LSTM_HEADS, LSTM_DK), state(1, LSTM_HEADS)),
        out_shape=out_shape,
        scratch_shapes=[pltpu.VMEM((CONV_PAD + L, LSTM_QK), F32),
                        pltpu.VMEM((LSTM_HEADS, LSTM_DV, LSTM_DK), F32),
                        pltpu.VMEM((LSTM_HEADS, LSTM_DK), F32),
                        pltpu.VMEM((1, LSTM_HEADS), F32)],
        compiler_params=pltpu.CompilerParams(dimension_semantics=("parallel", "arbitrary"),
                                             vmem_limit_bytes=VMEM_LIMIT),
        name="mlstm",
    )(qk_raw, v_raw, o_raw, misc, conv0, c0, n0, m0, *weights)
    return h, conv_new, c_new, n_new, m_new.reshape(b, LSTM_HEADS)


FF_CHUNK = 1024


def _out_ffn_kernel(x_ref, attn_ref, lstm_ref, w_out_ref, g_ffn_ref, w_up_ref, w_down_ref, y_ref):
    mix = jnp.concatenate([attn_ref[...], lstm_ref[...]], axis=1)
    x1 = x_ref[...] + jnp.dot(mix, w_out_ref[...], preferred_element_type=F32)
    xb = _rms(x1, g_ffn_ref[...]).astype(BF16)
    d_ff = w_up_ref.shape[1]
    acc = x1
    for f in range(d_ff // FF_CHUNK):
        cols = slice(f * FF_CHUNK, (f + 1) * FF_CHUNK)
        u = jnp.maximum(jnp.dot(xb, w_up_ref[:, cols], preferred_element_type=F32), 0.0)
        acc = acc + jnp.dot((u * u).astype(BF16), w_down_ref[cols, :], preferred_element_type=F32)
    y_ref[...] = acc


def _out_ffn(x2, attn, lstm, w, tm):
    t, d = x2.shape
    assert t % tm == 0
    row = lambda c: pl.BlockSpec((tm, c), lambda i: (i, 0))
    const = lambda a: pl.BlockSpec(a.shape, lambda i: (0, 0), pipeline_mode=pl.Buffered(1))
    weights = [w["w_out"], w["g_ffn"], w["w_up"], w["w_down"]]
    return pl.pallas_call(
        _out_ffn_kernel,
        grid=(t // tm,),
        in_specs=[row(d), row(MLA_WIDTH), row(LSTM_WIDTH)] + [const(a) for a in weights],
        out_specs=row(d),
        out_shape=jax.ShapeDtypeStruct((t, d), F32),
        compiler_params=pltpu.CompilerParams(dimension_semantics=("parallel",),
                                             vmem_limit_bytes=VMEM_LIMIT),
        name="out_ffn",
    )(x2, attn, lstm, *weights)


def _pack_layer(l, g_mix, w_in, g_q_lat, w_uq, g_q_nope, g_q_rope, g_kv_lat, g_k_rope, w_uk, g_k_nope, w_uv,
                w_conv, b_conv, b_igate, b_fgate, g_lstm_out, w_out, g_ffn, w_up, w_down):
    d = w_in.shape[1]
    wi = w_in[l]
    o = 0
    parts = {}
    for name, width in (("cq", Q_LORA), ("ckv", KV_LORA), ("kpe", ROPE_DIM), ("qk", LSTM_QK), ("v", LSTM_WIDTH),
                        ("o", LSTM_WIDTH), ("ig", LSTM_HEADS), ("fg", LSTM_HEADS)):
        parts[name] = wi[:, o:o + width]
        o += width
    misc = jnp.concatenate([parts["kpe"], _rot_half(parts["kpe"]), parts["ig"], parts["fg"],
                            jnp.zeros((d, LANES - MISC_FG - LSTM_HEADS), F32)], axis=1)
    w_main = jnp.concatenate([parts["cq"], parts["ckv"], parts["qk"], parts["v"], parts["o"], misc],
                             axis=1).astype(BF16)

    uq = w_uq[l].reshape(Q_LORA, MLA_HEADS, QK_DIM)
    uq = jnp.concatenate([uq, _rot_half(uq[..., NOPE_DIM:])], axis=-1).reshape(Q_LORA, MLA_HEADS * LANES)
    gq_head = jnp.concatenate([g_q_nope[l], g_q_rope[l], _rot_half(g_q_rope[l])])
    uk = w_uk[l].reshape(KV_LORA, MLA_HEADS, NOPE_DIM)
    uk = jnp.concatenate([uk, jnp.zeros_like(uk)], axis=-1).reshape(KV_LORA, MLA_HEADS * LANES)
    gkn_head = jnp.concatenate([g_k_nope[l], jnp.zeros((LANES - NOPE_DIM,), F32)])
    uv = w_uv[l].reshape(KV_LORA, MLA_HEADS, V_DIM)
    uv_t = jnp.concatenate([uv, jnp.zeros((KV_LORA, MLA_HEADS, VT_ROWS - V_DIM), F32)], axis=-1)
    uv_t = uv_t.reshape(KV_LORA, MLA_HEADS * VT_ROWS).T
    vones_head = jnp.zeros((VT_ROWS,), F32).at[V_DIM].set(1.0)
    gkm = jnp.concatenate([g_k_rope[l], _rot_half(g_k_rope[l]), jnp.zeros((LANES - 2 * ROPE_DIM,), F32)])
    bgate = jnp.concatenate([jnp.zeros((MISC_IG,), F32), b_igate[l], b_fgate[l],
                             jnp.zeros((LANES - MISC_FG - LSTM_HEADS,), F32)])

    i = jnp.arange(HEAD_PAIR)
    same = (i[:, None] // LANES) == (i[None, :] // LANES)
    li, lj = i[:, None] % LANES, i[None, :] % LANES
    mseg = jnp.where(same & (li < NOPE_DIM) & (lj < NOPE_DIM), 1.0 / NOPE_DIM,
                     jnp.where(same & (li >= NOPE_DIM) & (li < QK_DIM) & (lj >= NOPE_DIM), 1.0 / ROPE_DIM, 0.0))
    a = jnp.arange(LANES)
    mmisc = jnp.where((a[:, None] < ROPE_DIM) & (a[None, :] < 2 * ROPE_DIM), 1.0 / ROPE_DIM, 0.0)
    r = jnp.arange(ROPE_DIM)
    c = jnp.arange(MLA_HEADS * LANES)
    place = ((c[None, :] % LANES) == (NOPE_DIM + r[:, None])).astype(BF16)

    row = lambda v: v.reshape(1, -1).astype(F32)
    return {
        "g_mix": row(g_mix[l]), "w_main": w_main, "g_q_lat": row(g_q_lat[l]), "w_uq": uq.astype(BF16),
        "gq": row(jnp.tile(gq_head, MLA_HEADS)), "g_kv_lat": row(g_kv_lat[l]), "gkm": row(gkm),
        "w_uk": uk.astype(BF16), "gkn": row(jnp.tile(gkn_head, MLA_HEADS)), "w_uv": w_uv[l].astype(BF16),
        "w_uv_t": uv_t.astype(BF16), "vones": jnp.tile(vones_head, MLA_HEADS).reshape(-1, 1),
        "mseg": mseg.astype(BF16), "mmisc": mmisc.astype(BF16), "place": place,
        "w_conv": w_conv[l], "b_conv": row(b_conv[l]), "bgate": row(bgate), "g_lstm_out": row(g_lstm_out[l]),
        "w_out": w_out[l].astype(BF16), "g_ffn": row(g_ffn[l]), "w_up": w_up[l].astype(BF16),
        "w_down": w_down[l].astype(BF16),
    }


def _rope_tables(pos):
    half = ROPE_DIM // 2
    inv = ROPE_BASE ** (-jnp.arange(half, dtype=F32) / half)
    ang = pos.astype(F32)[:, None] * inv[None, :]
    cos2 = jnp.concatenate([jnp.cos(ang), jnp.cos(ang)], axis=1)
    sin2 = jnp.concatenate([-jnp.sin(ang), jnp.sin(ang)], axis=1)
    n = pos.shape[0]
    z = lambda c: jnp.zeros((n, c), F32)
    cq = jnp.concatenate([jnp.ones((n, NOPE_DIM), F32), cos2, z(LANES - QK_DIM)], axis=1)
    sq = jnp.concatenate([z(NOPE_DIM), sin2, z(LANES - QK_DIM)], axis=1)
    ck = jnp.concatenate([cos2, z(LANES - ROPE_DIM)], axis=1)
    sk = jnp.concatenate([sin2, z(LANES - ROPE_DIM)], axis=1)
    return cq, sq, ck, sk


def kernel(x_prompt, x_sample, cache_kv_latent, cache_k_rope, state_conv, state_C, state_n, state_m,
           g_mix, w_in, g_q_lat, w_uq, g_q_nope, g_q_rope, g_kv_lat, g_k_rope, w_uk, g_k_nope, w_uv,
           w_conv, b_conv, b_igate, b_fgate, g_lstm_out, w_out, g_ffn, w_up, w_down):
    depth = w_in.shape[0]
    bp, sp, d = x_prompt.shape
    bs, ls, _ = x_sample.shape
    past = cache_kv_latent.shape[2]
    hist = CONV_W - 1

    tabs_p = tuple(jnp.tile(t, (bp, 1)) for t in _rope_tables(jnp.arange(sp, dtype=jnp.int32)))
    tabs_s = tuple(jnp.tile(t, (bs, 1)) for t in _rope_tables(past + jnp.arange(ls, dtype=jnp.int32)))
    xp = x_prompt.reshape(bp * sp, d)
    xs = x_sample.reshape(bs * ls, d)
    zero_conv = jnp.zeros((bp, hist, LSTM_QK), F32)
    zero_c = jnp.zeros((bp, LSTM_HEADS, LSTM_DV, LSTM_DK), F32)
    zero_n = jnp.zeros((bp, LSTM_HEADS, LSTM_DK), F32)
    zero_m = jnp.zeros((bp, LSTM_HEADS), F32)

    outs = {k: [] for k in ("p_lat", "p_kr", "p_conv", "p_c", "p_n", "p_m",
                            "s_lat", "s_kr", "s_conv", "s_c", "s_n", "s_m")}
    for l in range(depth):
        w = _pack_layer(l, g_mix, w_in, g_q_lat, w_uq, g_q_nope, g_q_rope, g_kv_lat, g_k_rope, w_uk, g_k_nope,
                        w_uv, w_conv, b_conv, b_igate, b_fgate, g_lstm_out, w_out, g_ffn, w_up, w_down)
        q, k, v, lat, kr, qk_raw, v_raw, o_raw, misc = _projection(xp, tabs_p, w, tm=512)
        attn = _attention_prompt(q.reshape(bp, sp, -1), k.reshape(bp, sp, -1), v,
                                 tq=ATTN_TQ, tk=ATTN_TK)
        h, conv_new, c_new, n_new, m_new = _mlstm(qk_raw, v_raw, o_raw, misc, zero_conv, zero_c, zero_n, zero_m,
                                                  w, L=MLSTM_TILE)
        xp = _out_ffn(xp, attn.reshape(bp * sp, -1), h, w, tm=512)
        outs["p_lat"].append(lat.reshape(bp, sp, KV_LORA))
        outs["p_kr"].append(kr.reshape(bp, sp, ROPE_DIM))
        outs["p_conv"].append(conv_new)
        outs["p_c"].append(c_new)
        outs["p_n"].append(n_new)
        outs["p_m"].append(m_new)
        q, k, v, lat, kr, qk_raw, v_raw, o_raw, misc = _projection(xs, tabs_s, w, tm=bs * ls)
        attn = _attention_sample(q, k, lat, cache_kv_latent, cache_k_rope, l, w, n_new=ls)
        h, conv_new, c_new, n_new, m_new = _mlstm(qk_raw, v_raw, o_raw, misc, state_conv[l], state_C[l],
                                                  state_n[l], state_m[l], w, L=ls)
        xs = _out_ffn(xs, attn, h, w, tm=bs * ls)
        outs["s_lat"].append(lat.reshape(bs, ls, KV_LORA))
        outs["s_kr"].append(kr.reshape(bs, ls, ROPE_DIM))
        outs["s_conv"].append(conv_new)
        outs["s_c"].append(c_new)
        outs["s_n"].append(n_new)
        outs["s_m"].append(m_new)

    st = lambda key: jnp.stack(outs[key])
    return (xp.reshape(bp, sp, d), xs.reshape(bs, ls, d),
            st("p_lat"), st("p_kr"), st("p_conv"), st("p_c"), st("p_n"), st("p_m"),
            st("s_lat"), st("s_kr"), st("s_conv"), st("s_c"), st("s_n"), st("s_m"))
```

```python
import functools

import jax
import jax.numpy as jnp
from jax import lax
from jax.experimental import pallas as pl
from jax.experimental.pallas import tpu as pltpu

F32 = jnp.float32
BF16 = jnp.bfloat16

EPS = 1e-6
CHUNK = 64
MLA_HEADS = 8
NOPE_DIM = 64
ROPE_DIM = 32
QK_DIM = NOPE_DIM + ROPE_DIM
V_DIM = 64
Q_LORA = 256
KV_LORA = 128
ROPE_BASE = 10000.0
LSTM_HEADS = 4
LSTM_DK = 128
LSTM_DV = 128
CONV_W = 4
LSTM_WIDTH = LSTM_HEADS * LSTM_DV
LSTM_QK = 2 * LSTM_HEADS * LSTM_DK
MLA_WIDTH = MLA_HEADS * V_DIM

LANES = 128
SUBLANES = 8
HEAD_PAIR = 2 * LANES
VMEM_LIMIT = 52 * 1024 * 1024
Q_LOG2_SCALE = (QK_DIM ** -0.5) * 1.4426950408889634
ATTN_TQ = 512
ATTN_TK = 512
MLSTM_TILE = 256
VT_ROWS = 80

MISC_KPE = 0
MISC_KPE_ROT = ROPE_DIM
MISC_IG = 2 * ROPE_DIM
MISC_FG = MISC_IG + LSTM_HEADS

COL_CQ = (0, Q_LORA)
COL_CKV = (COL_CQ[1], COL_CQ[1] + KV_LORA)
COL_QK = (COL_CKV[1], COL_CKV[1] + LSTM_QK)
COL_V = (COL_QK[1], COL_QK[1] + LSTM_WIDTH)
COL_O = (COL_V[1], COL_V[1] + LSTM_WIDTH)
COL_MISC = (COL_O[1], COL_O[1] + LANES)
PACKED_COLS = COL_MISC[1]


def _rot_half(a):
    half = ROPE_DIM // 2
    return jnp.concatenate([a[..., half:], a[..., :half]], axis=-1)


def _rms(x, g):
    return x * lax.rsqrt(jnp.mean(x * x, axis=-1, keepdims=True) + EPS) * g


def _segment_mean_sq(y, m_ref):
    sq = y * y
    hi = sq.astype(BF16)
    lo = (sq - hi.astype(F32)).astype(BF16)
    m = m_ref[...]
    return (jnp.dot(hi, m, preferred_element_type=F32)
            + jnp.dot(lo, m, preferred_element_type=F32))


def _proj_kernel(x_ref, rope_ref, g_mix_ref, w_main_ref, g_qlat_ref, w_uq_ref,
                 gq_ref, g_kvlat_ref, gkm_ref, w_uk_ref, gkn_ref, w_uv_ref, vones_ref, mseg_ref, mmisc_ref,
                 q_out, k_out, v_out, lat_out, kr_out, qk_out, vl_out, o_out, misc_out):
    x = x_ref[...]
    xb = _rms(x, g_mix_ref[...]).astype(BF16)

    def proj(col):
        return jnp.dot(xb, w_main_ref[:, col[0]:col[1]], preferred_element_type=F32)

    qk_out[...] = proj(COL_QK)
    vl_out[...] = proj(COL_V)
    o_out[...] = proj(COL_O)
    misc = proj(COL_MISC)
    misc_out[...] = misc

    cqn = _rms(proj(COL_CQ), g_qlat_ref[...]).astype(BF16)
    tab = rope_ref[...]
    tab_next = pltpu.roll(tab, LANES - ROPE_DIM, 1)
    lane = lax.broadcasted_iota(jnp.int32, tab.shape, 1)
    in_rope = (lane >= NOPE_DIM) & (lane < QK_DIM)
    cq_tab = jnp.where(lane < NOPE_DIM, 1.0, jnp.where(in_rope, tab, 0.0))
    sq_tab = jnp.where(in_rope, tab_next, 0.0)
    ck_tab = jnp.where(lane < ROPE_DIM, tab, 0.0)
    sk_tab = jnp.where(lane < ROPE_DIM, tab_next, 0.0)
    for p in range(MLA_HEADS // 2):
        cols = slice(p * HEAD_PAIR, (p + 1) * HEAD_PAIR)
        raw = jnp.dot(cqn, w_uq_ref[:, cols], preferred_element_type=F32)
        y = raw * lax.rsqrt(_segment_mean_sq(raw, mseg_ref) + EPS) * gq_ref[:, cols]
        for j in range(2):
            yh = y[:, j * LANES:(j + 1) * LANES]
            qh = yh * cq_tab + pltpu.roll(yh, LANES - ROPE_DIM, 1) * sq_tab
            h = 2 * p + j
            q_out[:, h * LANES:(h + 1) * LANES] = (qh * Q_LOG2_SCALE).astype(BF16)

    lat = _rms(proj(COL_CKV), g_kvlat_ref[...])
    lat_out[...] = lat
    latb = lat.astype(BF16)
    lat_t = lat.T.astype(BF16)
    v_out[...] = (jnp.dot(w_uv_ref[...], lat_t, preferred_element_type=F32) + vones_ref[...]).astype(BF16)

    ms = misc * lax.rsqrt(_segment_mean_sq(misc, mmisc_ref) + EPS) * gkm_ref[...]
    kr = ms * ck_tab + pltpu.roll(ms, LANES - ROPE_DIM, 1) * sk_tab
    kr_out[...] = kr[:, :ROPE_DIM]
    kr_placed = pltpu.roll(kr, NOPE_DIM, 1)
    kr_pair = jnp.concatenate([kr_placed, kr_placed], axis=1)
    for p in range(MLA_HEADS // 2):
        cols = slice(p * HEAD_PAIR, (p + 1) * HEAD_PAIR)
        raw = jnp.dot(latb, w_uk_ref[:, cols], preferred_element_type=F32)
        y = raw * lax.rsqrt(_segment_mean_sq(raw, mseg_ref) + EPS) * gkn_ref[:, cols]
        k_out[:, cols] = (y + kr_pair).astype(BF16)


def _full_spec(a):
    nd = a.ndim
    return pl.BlockSpec(a.shape, lambda *_: (0,) * nd)


def _projection(x2, rope_tab, w, tm):
    t = x2.shape[0]
    assert t % tm == 0
    row = lambda c: pl.BlockSpec((tm, c), lambda i: (i, 0))
    weights = [w["g_mix"], w["w_main"], w["g_q_lat"], w["w_uq"], w["gq"], w["g_kv_lat"], w["gkm"],
               w["w_uk"], w["gkn"], w["w_uv_t"], w["vones"], w["mseg"], w["mmisc"]]
    out_shape = (
        jax.ShapeDtypeStruct((t, MLA_HEADS * LANES), BF16),
        jax.ShapeDtypeStruct((t, MLA_HEADS * LANES), BF16),
        jax.ShapeDtypeStruct((MLA_HEADS * VT_ROWS, t), BF16),
        jax.ShapeDtypeStruct((t, KV_LORA), F32),
        jax.ShapeDtypeStruct((t, ROPE_DIM), F32),
        jax.ShapeDtypeStruct((t, LSTM_QK), F32),
        jax.ShapeDtypeStruct((t, LSTM_WIDTH), F32),
        jax.ShapeDtypeStruct((t, LSTM_WIDTH), F32),
        jax.ShapeDtypeStruct((t, LANES), F32),
    )
    return pl.pallas_call(
        _proj_kernel,
        grid=(t // tm,),
        in_specs=[row(x2.shape[1]), row(LANES)] + [_full_spec(a) for a in weights],
        out_specs=tuple(pl.BlockSpec((MLA_HEADS * VT_ROWS, tm), lambda i: (0, i)) if n == 2 else row(s.shape[1])
                        for n, s in enumerate(out_shape)),
        out_shape=out_shape,
        compiler_params=pltpu.CompilerParams(dimension_semantics=("parallel",),
                                             vmem_limit_bytes=VMEM_LIMIT),
        name="projection",
    )(x2, rope_tab, *weights)


def _attn_prompt_kernel(q_ref, k_ref, vt_ref, o_ref, qt_sc, m_sc, acc_sc, sa_sc, sb_sc, *, tq, tk):
    qi = pl.program_id(2)
    m_sc[...] = jnp.full(m_sc.shape, -1e30, F32)
    acc_sc[...] = jnp.zeros(acc_sc.shape, F32)
    q_col0 = qi * tq
    for j in range(2):
        qt_sc[j] = q_ref[:, j * LANES:(j + 1) * LANES].astype(F32).T.astype(BF16)

    def scores(k0, s_ref):
        for j in range(2):
            k = k_ref[pl.ds(k0, tk), j * LANES:(j + 1) * LANES]
            s_ref[j] = jnp.dot(k, qt_sc[j], preferred_element_type=F32)

    def consume(k0, s_ref, masked):
        for j in range(2):
            vt = vt_ref[j * VT_ROWS:(j + 1) * VT_ROWS, pl.ds(k0, tk)]
            s = s_ref[j]
            if masked:
                kc = (k0 + lax.broadcasted_iota(jnp.int32, (tk, tq), 0)) // CHUNK
                qc = (q_col0 + lax.broadcasted_iota(jnp.int32, (tk, tq), 1)) // CHUNK
                s = jnp.where(kc <= qc, s, -1e30)
            m_prev = m_sc[j]
            m_next = jnp.maximum(m_prev, jnp.max(s, axis=0, keepdims=True))
            alpha = jnp.exp2(m_prev - m_next)
            p = jnp.exp2(s - m_next)
            m_sc[j] = m_next
            acc_sc[j] = alpha * acc_sc[j] + jnp.dot(vt, p.astype(BF16), preferred_element_type=F32)

    n_full = q_col0 // tk
    at = lambda t: pl.multiple_of(t * tk, tk)
    scores(0, sa_sc)

    def pair(t):
        scores(at(t + 1), sb_sc)
        consume(at(t), sa_sc, False)
        scores(at(t + 2), sa_sc)
        consume(at(t + 1), sb_sc, False)

    def body(i, carry):
        pair(4 * i)
        pair(4 * i + 2)
        return carry

    lax.fori_loop(0, n_full // 4, body, 0)

    @pl.when(n_full % 4 >= 2)
    def _pair_tail():
        pair(4 * (n_full // 4))

    t0 = 2 * (n_full // 2)

    @pl.when(n_full % 2 == 1)
    def _odd_tail():
        scores(at(t0 + 1), sb_sc)
        consume(at(t0), sa_sc, False)
        consume(at(t0 + 1), sb_sc, True)

    @pl.when(n_full % 2 == 0)
    def _even_tail():
        consume(at(t0), sa_sc, True)

    outs = []
    for j in range(2):
        acc = acc_sc[j]
        outs.append(acc[:V_DIM, :] / acc[V_DIM:V_DIM + 1, :])
    o_ref[...] = jnp.concatenate(outs, axis=0).T.astype(o_ref.dtype)


def _attention_prompt(q, k, vt, tq, tk):
    b, s, _ = q.shape
    assert tq == tk and s % tq == 0 and tq % CHUNK == 0
    pairs = MLA_HEADS // 2
    return pl.pallas_call(
        functools.partial(_attn_prompt_kernel, tq=tq, tk=tk),
        grid=(b, pairs, s // tq),
        in_specs=[
            pl.BlockSpec((None, tq, HEAD_PAIR), lambda bi, p, i: (bi, i, p)),
            pl.BlockSpec((None, s, HEAD_PAIR), lambda bi, p, i: (bi, 0, p)),
            pl.BlockSpec((2 * VT_ROWS, s), lambda bi, p, i: (p, bi)),
        ],
        out_specs=pl.BlockSpec((None, tq, LANES), lambda bi, p, i: (bi, i, p)),
        out_shape=jax.ShapeDtypeStruct((b, s, MLA_WIDTH), BF16),
        scratch_shapes=[pltpu.VMEM((2, LANES, tq), BF16),
                        pltpu.VMEM((2, 1, tq), F32),
                        pltpu.VMEM((2, VT_ROWS, tq), F32),
                        pltpu.VMEM((2, tk, tq), F32),
                        pltpu.VMEM((2, tk, tq), F32)],
        compiler_params=pltpu.CompilerParams(
            dimension_semantics=("parallel", "parallel", "arbitrary"),
            vmem_limit_bytes=VMEM_LIMIT),
        name="attention_prompt",
    )(q, k, vt)


def _attn_sample_kernel(q_ref, kn_ref, latn_ref, lat_ref, kr_ref, w_uk_ref, gkn_ref, w_uv_ref, mseg_ref,
                        o_ref, k_sc, *, n_new):
    latb = lat_ref[...].astype(BF16)
    kr = kr_ref[...]
    zeros = lambda c: jnp.zeros((kr.shape[0], c), F32)
    kr_placed = jnp.concatenate([zeros(NOPE_DIM), kr, zeros(LANES - QK_DIM)], axis=1)
    kr_pair = jnp.concatenate([kr_placed, kr_placed], axis=1)
    for p in range(MLA_HEADS // 2):
        cols = slice(p * HEAD_PAIR, (p + 1) * HEAD_PAIR)
        raw = jnp.dot(latb, w_uk_ref[:, cols], preferred_element_type=F32)
        y = raw * lax.rsqrt(_segment_mean_sq(raw, mseg_ref) + EPS) * gkn_ref[:, cols]
        k_sc[:, cols] = (y + kr_pair).astype(BF16)
    v_all = jnp.dot(latb, w_uv_ref[...], preferred_element_type=F32).astype(BF16)
    v_new = jnp.dot(latn_ref[...].astype(BF16), w_uv_ref[...], preferred_element_type=F32).astype(BF16)

    q = q_ref[...]
    qt = jnp.concatenate([q] * MLA_HEADS, axis=0)
    r_head = lax.broadcasted_iota(jnp.int32, qt.shape, 0) // n_new
    c_head = lax.broadcasted_iota(jnp.int32, qt.shape, 1) // LANES
    qm = jnp.where(r_head == c_head, qt, jnp.zeros_like(qt))

    nt = (((1,), (1,)), ((), ()))
    s_old = lax.dot_general(k_sc[...], qm, nt, preferred_element_type=F32)
    s_new = lax.dot_general(kn_ref[...], qm, nt, preferred_element_type=F32)
    mx = jnp.maximum(jnp.max(s_old, axis=0, keepdims=True), jnp.max(s_new, axis=0, keepdims=True))
    p_old = jnp.exp2(s_old - mx)
    p_new = jnp.exp2(s_new - mx)
    inv = 1.0 / (jnp.sum(p_old, axis=0, keepdims=True) + jnp.sum(p_new, axis=0, keepdims=True))
    p_old = (p_old * inv).astype(BF16)
    p_new = (p_new * inv).astype(BF16)
    tn = (((0,), (0,)), ((), ()))
    full = (lax.dot_general(p_old, v_all, tn, preferred_element_type=F32)
            + lax.dot_general(p_new, v_new, tn, preferred_element_type=F32))
    v_head = lax.broadcasted_iota(jnp.int32, (n_new, MLA_WIDTH), 1) // V_DIM
    out = jnp.zeros((n_new, MLA_WIDTH), F32)
    for h in range(MLA_HEADS):
        out = out + jnp.where(v_head == h, full[h * n_new:(h + 1) * n_new, :], 0.0)
    o_ref[...] = out.astype(o_ref.dtype)


def _attention_sample(q, k_new, lat_new, cache_lat, cache_kr, layer, w, n_new):
    _, b, past, _ = cache_lat.shape
    weights = [w["w_uk"], w["gkn"], w["w_uv"], w["mseg"]]
    tok = lambda c: pl.BlockSpec((n_new, c), lambda i: (i, 0))
    return pl.pallas_call(
        functools.partial(_attn_sample_kernel, n_new=n_new),
        grid=(b,),
        in_specs=[tok(MLA_HEADS * LANES), tok(MLA_HEADS * LANES), tok(KV_LORA),
                  pl.BlockSpec((None, None, past, KV_LORA), lambda i: (layer, i, 0, 0)),
                  pl.BlockSpec((None, None, past, ROPE_DIM), lambda i: (layer, i, 0, 0))]
                 + [_full_spec(a) for a in weights],
        out_specs=tok(MLA_WIDTH),
        out_shape=jax.ShapeDtypeStruct((b * n_new, MLA_WIDTH), BF16),
        scratch_shapes=[pltpu.VMEM((past, MLA_HEADS * LANES), BF16)],
        compiler_params=pltpu.CompilerParams(dimension_semantics=("parallel",),
                                             vmem_limit_bytes=VMEM_LIMIT),
        name="attention_sample",
    )(q, k_new, lat_new, cache_lat, cache_kr, *weights)


CONV_PAD = SUBLANES


def _mlstm_kernel(qk_ref, v_ref, o_ref, misc_ref, conv0_ref, c0_ref, n0_ref, m0_ref, wconv_ref, bconv_ref,
                  bgate_ref, glstm_ref,
                  h_out, conv_out, c_out, n_out, m_out,
                  full_sc, c_sc, n_sc, m_sc, *, L):
    c = pl.program_id(1)
    last = pl.num_programs(1) - 1
    hist = CONV_W - 1
    lo = CONV_PAD - hist

    @pl.when(c == 0)
    def _init():
        full_sc[lo:CONV_PAD, :] = conv0_ref[...]
        c_sc[...] = c0_ref[...]
        n_sc[...] = n0_ref[...]
        m_sc[...] = m0_ref[...]

    full_sc[CONV_PAD:CONV_PAD + L, :] = qk_ref[...]
    y = bconv_ref[...]
    for j in range(CONV_W):
        y = y + full_sc[lo + j:lo + j + L, :] * wconv_ref[j:j + 1, :]
    qk = y * jax.nn.sigmoid(y)
    tail = full_sc[lo + L:CONV_PAD + L, :]
    full_sc[lo:CONV_PAD, :] = tail

    gs = misc_ref[...] + bgate_ref[...]
    lfs = jnp.minimum(gs, 0.0) - jnp.log1p(jnp.exp(-jnp.abs(gs)))
    row = lax.broadcasted_iota(jnp.int32, (L, L), 0)
    col = lax.broadcasted_iota(jnp.int32, (L, L), 1)
    causal = row >= col
    tri = jnp.where(causal, 1.0, 0.0).astype(BF16)
    lf_hi = lfs.astype(BF16)
    lf_mid = (lfs - lf_hi.astype(F32)).astype(BF16)
    lf_lo = (lfs - lf_hi.astype(F32) - lf_mid.astype(F32)).astype(BF16)
    b_slab = (jnp.dot(tri, lf_hi, preferred_element_type=F32)
              + jnp.dot(tri, lf_mid, preferred_element_type=F32)
              + jnp.dot(tri, lf_lo, preferred_element_type=F32))
    lane = lax.broadcasted_iota(jnp.int32, (L, LANES), 1)
    comb = jnp.where(lane < MISC_FG, gs, b_slab)
    if L < LANES:
        comb = jnp.concatenate([comb, jnp.zeros((LANES - L, LANES), F32)], axis=0)
    comb_t = comb.T

    g_all = glstm_ref[...]
    for h in range(LSTM_HEADS):
        dk = slice(h * LSTM_DK, (h + 1) * LSTM_DK)
        dv = slice(h * LSTM_DV, (h + 1) * LSTM_DV)
        ig_col = gs[:, MISC_IG + h:MISC_IG + h + 1]
        b_col = b_slab[:, MISC_FG + h:MISC_FG + h + 1]
        ig_row = comb_t[MISC_IG + h:MISC_IG + h + 1, :L]
        b_row = comb_t[MISC_FG + h:MISC_FG + h + 1, :L]
        m_prev = m_sc[:, h:h + 1]

        log_d = jnp.where(causal, b_col - b_row + ig_row, -jnp.inf)
        inter = b_col + m_prev
        m_t = jnp.maximum(inter, jnp.max(log_d, axis=1, keepdims=True))
        decay = jnp.exp(log_d - m_t)
        inter_scale = jnp.exp(inter - m_t)

        qh = qk[:, dk]
        kh = qk[:, LSTM_HEADS * LSTM_DK + h * LSTM_DK:LSTM_HEADS * LSTM_DK + (h + 1) * LSTM_DK] * (LSTM_DK ** -0.5)
        vh = v_ref[:, dv]
        qb = qh.astype(BF16)
        kb = kh.astype(BF16)
        c_prev = c_sc[h]
        n_prev = n_sc[h:h + 1, :]
        wgt = lax.dot_general(qb, kb, (((1,), (1,)), ((), ())), preferred_element_type=F32) * decay
        cq = lax.dot_general(qb, c_prev.astype(BF16), (((1,), (1,)), ((), ())), preferred_element_type=F32)
        num = jnp.dot(wgt.astype(BF16), vh.astype(BF16), preferred_element_type=F32) + inter_scale * cq
        den = (jnp.sum(wgt, axis=1, keepdims=True)
               + inter_scale * jnp.sum(qh * n_prev, axis=1, keepdims=True))
        hid = num / jnp.maximum(jnp.abs(den), jnp.exp(-m_t))

        m_new = m_t[L - 1:L, :]
        b_last = b_col[L - 1:L, :]
        carry = jnp.exp(b_last + m_prev - m_new)
        d_end = jnp.exp(b_last - b_col + ig_col - m_new)
        upd = lax.dot_general((d_end * vh).astype(BF16), kb, (((0,), (0,)), ((), ())),
                              preferred_element_type=F32)
        c_sc[h] = carry * c_prev + upd
        n_sc[h:h + 1, :] = carry * n_prev + jnp.sum(d_end * kh, axis=0, keepdims=True)
        m_sc[:, h:h + 1] = m_new

        hn = _rms(hid, g_all[:, dv])
        h_out[:, dv] = (hn * jax.nn.sigmoid(o_ref[:, dv])).astype(h_out.dtype)

    @pl.when(c == last)
    def _finish():
        conv_out[...] = tail
        c_out[...] = c_sc[...]
        n_out[...] = n_sc[...]
        m_out[...] = m_sc[...]


def _mlstm(qk_raw, v_raw, o_raw, misc, conv0, c0, n0, m0, layer, w, L):
    b = conv0.shape[1]
    t = qk_raw.shape[0]
    nc = t // (b * L)
    assert nc * b * L == t
    hist = CONV_W - 1
    tok = lambda cdim: pl.BlockSpec((L, cdim), lambda bi, ci: (bi * nc + ci, 0))
    m0 = m0.reshape(m0.shape[0], b, 1, LSTM_HEADS)
    state0 = lambda *dims: pl.BlockSpec((None, None) + dims, lambda bi, ci: (layer, bi) + (0,) * len(dims))
    weights = [w["w_conv"], w["b_conv"], w["bgate"], w["g_lstm_out"]]
    out_shape = (
        jax.ShapeDtypeStruct((t, LSTM_WIDTH), BF16),
        jax.ShapeDtypeStruct((b, hist, LSTM_QK), F32),
        jax.ShapeDtypeStruct((b, LSTM_HEADS, LSTM_DV, LSTM_DK), F32),
        jax.ShapeDtypeStruct((b, LSTM_HEADS, LSTM_DK), F32),
        jax.ShapeDtypeStruct((b, 1, LSTM_HEADS), F32),
    )
    state = lambda *dims: pl.BlockSpec((None,) + dims, lambda bi, ci: (bi,) + (0,) * len(dims))
    h, conv_new, c_new, n_new, m_new = pl.pallas_call(
        functools.partial(_mlstm_kernel, L=L),
        grid=(b, nc),
        in_specs=[tok(LSTM_QK), tok(LSTM_WIDTH), tok(LSTM_WIDTH), tok(LANES),
                  state0(hist, LSTM_QK), state0(LSTM_HEADS, LSTM_DV, LSTM_DK), state0(LSTM_HEADS, LSTM_DK),
                  state0(1, LSTM_HEADS)] + [_full_spec(a) for a in weights],
        out_specs=(tok(LSTM_WIDTH), state(hist, LSTM_QK), state(LSTM_HEADS, LSTM_DV, LSTM_DK),
                   state(LSTM_HEADS, LSTM_DK), state(1, LSTM_HEADS)),
        out_shape=out_shape,
        scratch_shapes=[pltpu.VMEM((CONV_PAD + L, LSTM_QK), F32),
                        pltpu.VMEM((LSTM_HEADS, LSTM_DV, LSTM_DK), F32),
                        pltpu.VMEM((LSTM_HEADS, LSTM_DK), F32),
                        pltpu.VMEM((1, LSTM_HEADS), F32)],
        compiler_params=pltpu.CompilerParams(dimension_semantics=("parallel", "arbitrary"),
                                             vmem_limit_bytes=VMEM_LIMIT),
        name="mlstm",
    )(qk_raw, v_raw, o_raw, misc, conv0, c0, n0, m0, *weights)
    return h, conv_new, c_new, n_new, m_new.reshape(b, LSTM_HEADS)


FF_CHUNK = 1024


def _out_ffn_kernel(x_ref, attn_ref, lstm_ref, w_out_ref, g_ffn_ref, w_up_ref, w_down_ref, y_ref):
    mix = jnp.concatenate([attn_ref[...], lstm_ref[...]], axis=1)
    x1 = x_ref[...] + jnp.dot(mix, w_out_ref[...], preferred_element_type=F32)
    xb = _rms(x1, g_ffn_ref[...]).astype(BF16)
    d_ff = w_up_ref.shape[1]
    acc = x1
    for f in range(d_ff // FF_CHUNK):
        cols = slice(f * FF_CHUNK, (f + 1) * FF_CHUNK)
        u = jnp.maximum(jnp.dot(xb, w_up_ref[:, cols], preferred_element_type=F32), 0.0)
        acc = acc + jnp.dot((u * u).astype(BF16), w_down_ref[cols, :], preferred_element_type=F32)
    y_ref[...] = acc


def _out_ffn(x2, attn, lstm, w, tm):
    t, d = x2.shape
    assert t % tm == 0
    row = lambda c: pl.BlockSpec((tm, c), lambda i: (i, 0))
    const = lambda a: pl.BlockSpec(a.shape, lambda i: (0, 0), pipeline_mode=pl.Buffered(1))
    weights = [w["w_out"], w["g_ffn"], w["w_up"], w["w_down"]]
    return pl.pallas_call(
        _out_ffn_kernel,
        grid=(t // tm,),
        in_specs=[row(d), row(MLA_WIDTH), row(LSTM_WIDTH)] + [const(a) for a in weights],
        out_specs=row(d),
        out_shape=jax.ShapeDtypeStruct((t, d), F32),
        compiler_params=pltpu.CompilerParams(dimension_semantics=("parallel",),
                                             vmem_limit_bytes=VMEM_LIMIT),
        name="out_ffn",
    )(x2, attn, lstm, *weights)


def _pack_layer(l, g_mix, w_in, g_q_lat, w_uq, g_q_nope, g_q_rope, g_kv_lat, g_k_rope, w_uk, g_k_nope, w_uv,
                w_conv, b_conv, b_igate, b_fgate, g_lstm_out, w_out, g_ffn, w_up, w_down):
    d = w_in.shape[1]
    wi = w_in[l]
    o = 0
    parts = {}
    for name, width in (("cq", Q_LORA), ("ckv", KV_LORA), ("kpe", ROPE_DIM), ("qk", LSTM_QK), ("v", LSTM_WIDTH),
                        ("o", LSTM_WIDTH), ("ig", LSTM_HEADS), ("fg", LSTM_HEADS)):
        parts[name] = wi[:, o:o + width]
        o += width
    misc = jnp.concatenate([parts["kpe"], _rot_half(parts["kpe"]), parts["ig"], parts["fg"],
                            jnp.zeros((d, LANES - MISC_FG - LSTM_HEADS), F32)], axis=1)
    w_main = jnp.concatenate([parts["cq"], parts["ckv"], parts["qk"], parts["v"], parts["o"], misc],
                             axis=1).astype(BF16)

    uq = w_uq[l].reshape(Q_LORA, MLA_HEADS, QK_DIM)
    uq = jnp.concatenate([uq, _rot_half(uq[..., NOPE_DIM:])], axis=-1).reshape(Q_LORA, MLA_HEADS * LANES)
    gq_head = jnp.concatenate([g_q_nope[l], g_q_rope[l], _rot_half(g_q_rope[l])])
    uk = w_uk[l].reshape(KV_LORA, MLA_HEADS, NOPE_DIM)
    uk = jnp.concatenate([uk, jnp.zeros_like(uk)], axis=-1).reshape(KV_LORA, MLA_HEADS * LANES)
    gkn_head = jnp.concatenate([g_k_nope[l], jnp.zeros((LANES - NOPE_DIM,), F32)])
    uv = w_uv[l].reshape(KV_LORA, MLA_HEADS, V_DIM)
    uv_t = jnp.concatenate([uv, jnp.zeros((KV_LORA, MLA_HEADS, VT_ROWS - V_DIM), F32)], axis=-1)
    uv_t = uv_t.reshape(KV_LORA, MLA_HEADS * VT_ROWS).T
    vones_head = jnp.zeros((VT_ROWS,), F32).at[V_DIM].set(1.0)
    gkm = jnp.concatenate([g_k_rope[l], _rot_half(g_k_rope[l]), jnp.zeros((LANES - 2 * ROPE_DIM,), F32)])
    bgate = jnp.concatenate([jnp.zeros((MISC_IG,), F32), b_igate[l], b_fgate[l],
                             jnp.zeros((LANES - MISC_FG - LSTM_HEADS,), F32)])

    i = jnp.arange(HEAD_PAIR)
    same = (i[:, None] // LANES) == (i[None, :] // LANES)
    li, lj = i[:, None] % LANES, i[None, :] % LANES
    mseg = jnp.where(same & (li < NOPE_DIM) & (lj < NOPE_DIM), 1.0 / NOPE_DIM,
                     jnp.where(same & (li >= NOPE_DIM) & (li < QK_DIM) & (lj >= NOPE_DIM), 1.0 / ROPE_DIM, 0.0))
    a = jnp.arange(LANES)
    mmisc = jnp.where((a[:, None] < ROPE_DIM) & (a[None, :] < 2 * ROPE_DIM), 1.0 / ROPE_DIM, 0.0)

    row = lambda v: v.reshape(1, -1).astype(F32)
    return {
        "g_mix": row(g_mix[l]), "w_main": w_main, "g_q_lat": row(g_q_lat[l]), "w_uq": uq.astype(BF16),
        "gq": row(jnp.tile(gq_head, MLA_HEADS)), "g_kv_lat": row(g_kv_lat[l]), "gkm": row(gkm),
        "w_uk": uk.astype(BF16), "gkn": row(jnp.tile(gkn_head, MLA_HEADS)), "w_uv": w_uv[l].astype(BF16),
        "w_uv_t": uv_t.astype(BF16), "vones": jnp.tile(vones_head, MLA_HEADS).reshape(-1, 1),
        "mseg": mseg.astype(BF16), "mmisc": mmisc.astype(BF16),
        "w_conv": w_conv[l], "b_conv": row(b_conv[l]), "bgate": row(bgate), "g_lstm_out": row(g_lstm_out[l]),
        "w_out": w_out[l].astype(BF16), "g_ffn": row(g_ffn[l]), "w_up": w_up[l].astype(BF16),
        "w_down": w_down[l].astype(BF16),
    }


def _rope_table(first_pos, n):
    half = ROPE_DIM // 2
    per_row = LANES // half
    assert n % per_row == 0
    lane = jnp.arange(LANES, dtype=jnp.int32)
    inv = ROPE_BASE ** (-(lane % half).astype(F32) / half)
    pos = first_pos + per_row * jnp.arange(n // per_row, dtype=jnp.int32)[:, None] + (lane // half)[None, :]
    ang = pos.astype(F32) * inv[None, :]
    cos = jnp.cos(ang).reshape(n, half)
    sin = jnp.sin(ang).reshape(n, half)
    cos2 = jnp.concatenate([cos, cos], axis=1)
    sin2 = jnp.concatenate([-sin, sin], axis=1)
    return jnp.concatenate([cos2, sin2, cos2, sin2], axis=1)


def kernel(x_prompt, x_sample, cache_kv_latent, cache_k_rope, state_conv, state_C, state_n, state_m,
           g_mix, w_in, g_q_lat, w_uq, g_q_nope, g_q_rope, g_kv_lat, g_k_rope, w_uk, g_k_nope, w_uv,
           w_conv, b_conv, b_igate, b_fgate, g_lstm_out, w_out, g_ffn, w_up, w_down):
    depth = w_in.shape[0]
    bp, sp, d = x_prompt.shape
    bs, ls, _ = x_sample.shape
    past = cache_kv_latent.shape[2]
    hist = CONV_W - 1

    tabs_p = jnp.tile(_rope_table(0, sp), (bp, 1))
    tabs_s = jnp.tile(_rope_table(past, ls), (bs, 1))
    xp = x_prompt.reshape(bp * sp, d)
    xs = x_sample.reshape(bs * ls, d)
    zero_conv = jnp.zeros((1, bp, hist, LSTM_QK), F32)
    zero_c = jnp.zeros((1, bp, LSTM_HEADS, LSTM_DV, LSTM_DK), F32)
    zero_n = jnp.zeros((1, bp, LSTM_HEADS, LSTM_DK), F32)
    zero_m = jnp.zeros((1, bp, LSTM_HEADS), F32)

    outs = {k: [] for k in ("p_lat", "p_kr", "p_conv", "p_c", "p_n", "p_m",
                            "s_lat", "s_kr", "s_conv", "s_c", "s_n", "s_m")}
    for l in range(depth):
        w = _pack_layer(l, g_mix, w_in, g_q_lat, w_uq, g_q_nope, g_q_rope, g_kv_lat, g_k_rope, w_uk, g_k_nope,
                        w_uv, w_conv, b_conv, b_igate, b_fgate, g_lstm_out, w_out, g_ffn, w_up, w_down)
        q, k, v, lat, kr, qk_raw, v_raw, o_raw, misc = _projection(xp, tabs_p, w, tm=512)
        attn = _attention_prompt(q.reshape(bp, sp, -1), k.reshape(bp, sp, -1), v,
                                 tq=ATTN_TQ, tk=ATTN_TK)
        h, conv_new, c_new, n_new, m_new = _mlstm(qk_raw, v_raw, o_raw, misc, zero_conv, zero_c, zero_n, zero_m,
                                                  0, w, L=MLSTM_TILE)
        xp = _out_ffn(xp, attn.reshape(bp * sp, -1), h, w, tm=512)
        outs["p_lat"].append(lat.reshape(bp, sp, KV_LORA))
        outs["p_kr"].append(kr.reshape(bp, sp, ROPE_DIM))
        outs["p_conv"].append(conv_new)
        outs["p_c"].append(c_new)
        outs["p_n"].append(n_new)
        outs["p_m"].append(m_new)
        q, k, v, lat, kr, qk_raw, v_raw, o_raw, misc = _projection(xs, tabs_s, w, tm=bs * ls)
        attn = _attention_sample(q, k, lat, cache_kv_latent, cache_k_rope, l, w, n_new=ls)
        h, conv_new, c_new, n_new, m_new = _mlstm(qk_raw, v_raw, o_raw, misc, state_conv, state_C,
                                                  state_n, state_m, l, w, L=ls)
        xs = _out_ffn(xs, attn, h, w, tm=bs * ls)
        outs["s_lat"].append(lat.reshape(bs, ls, KV_LORA))
        outs["s_kr"].append(kr.reshape(bs, ls, ROPE_DIM))
        outs["s_conv"].append(conv_new)
        outs["s_c"].append(c_new)
        outs["s_n"].append(n_new)
        outs["s_m"].append(m_new)

    st = lambda key: jnp.stack(outs[key])
    return (xp.reshape(bp, sp, d), xs.reshape(bs, ls, d),
            st("p_lat"), st("p_kr"), st("p_conv"), st("p_c"), st("p_n"), st("p_m"),
            st("s_lat"), st("s_kr"), st("s_conv"), st("s_c"), st("s_n"), st("s_m"))
```

```python
import functools

import jax
import jax.numpy as jnp
from jax import lax
from jax.experimental import pallas as pl
from jax.experimental.pallas import tpu as pltpu

F32 = jnp.float32
BF16 = jnp.bfloat16

EPS = 1e-6
CHUNK = 64
MLA_HEADS = 8
NOPE_DIM = 64
ROPE_DIM = 32
QK_DIM = NOPE_DIM + ROPE_DIM
V_DIM = 64
Q_LORA = 256
KV_LORA = 128
ROPE_BASE = 10000.0
LSTM_HEADS = 4
LSTM_DK = 128
LSTM_DV = 128
CONV_W = 4
LSTM_WIDTH = LSTM_HEADS * LSTM_DV
LSTM_QK = 2 * LSTM_HEADS * LSTM_DK
MLA_WIDTH = MLA_HEADS * V_DIM

LANES = 128
SUBLANES = 8
HEAD_PAIR = 2 * LANES
VMEM_LIMIT = 52 * 1024 * 1024
Q_LOG2_SCALE = (QK_DIM ** -0.5) * 1.4426950408889634
ATTN_TQ = 512
ATTN_TK = 512
MLSTM_TILE = 256
VT_ROWS = 80

MISC_KPE = 0
MISC_KPE_ROT = ROPE_DIM
MISC_IG = 2 * ROPE_DIM
MISC_FG = MISC_IG + LSTM_HEADS

COL_CQ = (0, Q_LORA)
COL_CKV = (COL_CQ[1], COL_CQ[1] + KV_LORA)
COL_QK = (COL_CKV[1], COL_CKV[1] + LSTM_QK)
COL_V = (COL_QK[1], COL_QK[1] + LSTM_WIDTH)
COL_O = (COL_V[1], COL_V[1] + LSTM_WIDTH)
COL_MISC = (COL_O[1], COL_O[1] + LANES)
PACKED_COLS = COL_MISC[1]


def _rot_half(a):
    half = ROPE_DIM // 2
    return jnp.concatenate([a[..., half:], a[..., :half]], axis=-1)


def _rms(x, g):
    return x * lax.rsqrt(jnp.mean(x * x, axis=-1, keepdims=True) + EPS) * g


def _segment_mean_sq(y, m_ref):
    return jnp.dot((y * y).astype(BF16), m_ref[...], preferred_element_type=F32)


def _proj_kernel(x_ref, rope_ref, g_mix_ref, w_main_ref, g_qlat_ref, w_uq_ref,
                 gq_ref, g_kvlat_ref, gkm_ref, w_uk_ref, gkn_ref, w_uv_ref, vones_ref, mseg_ref, mmisc_ref,
                 q_out, k_out, v_out, lat_out, kr_out, qk_out, vl_out, o_out, misc_out):
    x = x_ref[...]
    xb = _rms(x, g_mix_ref[...]).astype(BF16)

    def proj(col):
        return jnp.dot(xb, w_main_ref[:, col[0]:col[1]], preferred_element_type=F32)

    qk_out[...] = proj(COL_QK)
    vl_out[...] = proj(COL_V)
    o_out[...] = proj(COL_O)
    misc = proj(COL_MISC)
    misc_out[...] = misc

    cqn = _rms(proj(COL_CQ), g_qlat_ref[...]).astype(BF16)
    tab = rope_ref[...]
    tab_next = pltpu.roll(tab, LANES - ROPE_DIM, 1)
    lane = lax.broadcasted_iota(jnp.int32, tab.shape, 1)
    in_rope = (lane >= NOPE_DIM) & (lane < QK_DIM)
    cq_tab = jnp.where(lane < NOPE_DIM, 1.0, jnp.where(in_rope, tab, 0.0))
    sq_tab = jnp.where(in_rope, tab_next, 0.0)
    ck_tab = jnp.where(lane < ROPE_DIM, tab, 0.0)
    sk_tab = jnp.where(lane < ROPE_DIM, tab_next, 0.0)
    for p in range(MLA_HEADS // 2):
        cols = slice(p * HEAD_PAIR, (p + 1) * HEAD_PAIR)
        raw = jnp.dot(cqn, w_uq_ref[:, cols], preferred_element_type=F32)
        y = raw * lax.rsqrt(_segment_mean_sq(raw, mseg_ref) + EPS) * gq_ref[:, cols]
        for j in range(2):
            yh = y[:, j * LANES:(j + 1) * LANES]
            qh = yh * cq_tab + pltpu.roll(yh, LANES - ROPE_DIM, 1) * sq_tab
            h = 2 * p + j
            q_out[:, h * LANES:(h + 1) * LANES] = (qh * Q_LOG2_SCALE).astype(BF16)

    lat = _rms(proj(COL_CKV), g_kvlat_ref[...])
    lat_out[...] = lat
    latb = lat.astype(BF16)
    lat_t = lat.T.astype(BF16)
    v_out[...] = (jnp.dot(w_uv_ref[...], lat_t, preferred_element_type=F32) + vones_ref[...]).astype(BF16)

    ms = misc * lax.rsqrt(_segment_mean_sq(misc, mmisc_ref) + EPS) * gkm_ref[...]
    kr = ms * ck_tab + pltpu.roll(ms, LANES - ROPE_DIM, 1) * sk_tab
    kr_out[...] = kr[:, :ROPE_DIM]
    kr_placed = pltpu.roll(kr, NOPE_DIM, 1)
    kr_pair = jnp.concatenate([kr_placed, kr_placed], axis=1)
    for p in range(MLA_HEADS // 2):
        cols = slice(p * HEAD_PAIR, (p + 1) * HEAD_PAIR)
        raw = jnp.dot(latb, w_uk_ref[:, cols], preferred_element_type=F32)
        y = raw * lax.rsqrt(_segment_mean_sq(raw, mseg_ref) + EPS) * gkn_ref[:, cols]
        k_out[:, cols] = (y + kr_pair).astype(BF16)


def _full_spec(a):
    nd = a.ndim
    return pl.BlockSpec(a.shape, lambda *_: (0,) * nd)


def _projection(x2, rope_tab, w, tm):
    t = x2.shape[0]
    assert t % tm == 0
    row = lambda c: pl.BlockSpec((tm, c), lambda i: (i, 0))
    weights = [w["g_mix"], w["w_main"], w["g_q_lat"], w["w_uq"], w["gq"], w["g_kv_lat"], w["gkm"],
               w["w_uk"], w["gkn"], w["w_uv_t"], w["vones"], w["mseg"], w["mmisc"]]
    out_shape = (
        jax.ShapeDtypeStruct((t, MLA_HEADS * LANES), BF16),
        jax.ShapeDtypeStruct((t, MLA_HEADS * LANES), BF16),
        jax.ShapeDtypeStruct((MLA_HEADS * VT_ROWS, t), BF16),
        jax.ShapeDtypeStruct((t, KV_LORA), F32),
        jax.ShapeDtypeStruct((t, ROPE_DIM), F32),
        jax.ShapeDtypeStruct((t, LSTM_QK), F32),
        jax.ShapeDtypeStruct((t, LSTM_WIDTH), F32),
        jax.ShapeDtypeStruct((t, LSTM_WIDTH), F32),
        jax.ShapeDtypeStruct((t, LANES), F32),
    )
    return pl.pallas_call(
        _proj_kernel,
        grid=(t // tm,),
        in_specs=[row(x2.shape[1]), row(LANES)] + [_full_spec(a) for a in weights],
        out_specs=tuple(pl.BlockSpec((MLA_HEADS * VT_ROWS, tm), lambda i: (0, i)) if n == 2 else row(s.shape[1])
                        for n, s in enumerate(out_shape)),
        out_shape=out_shape,
        compiler_params=pltpu.CompilerParams(dimension_semantics=("parallel",),
                                             vmem_limit_bytes=VMEM_LIMIT),
        name="projection",
    )(x2, rope_tab, *weights)


def _attn_prompt_kernel(q_ref, k_ref, vt_ref, o_ref, qt_sc, m_sc, acc_sc, sa_sc, sb_sc, *, tq, tk):
    qi = pl.program_id(2)
    m_sc[...] = jnp.full(m_sc.shape, -1e30, F32)
    acc_sc[...] = jnp.zeros(acc_sc.shape, F32)
    q_col0 = qi * tq
    for j in range(2):
        qt_sc[j] = q_ref[:, j * LANES:(j + 1) * LANES].astype(F32).T.astype(BF16)

    def scores(k0, s_ref):
        for j in range(2):
            k = k_ref[pl.ds(k0, tk), j * LANES:(j + 1) * LANES]
            s_ref[j] = jnp.dot(k, qt_sc[j], preferred_element_type=F32)

    def consume(k0, s_ref, masked):
        for j in range(2):
            vt = vt_ref[j * VT_ROWS:(j + 1) * VT_ROWS, pl.ds(k0, tk)]
            s = s_ref[j]
            if masked:
                kc = (k0 + lax.broadcasted_iota(jnp.int32, (tk, tq), 0)) // CHUNK
                qc = (q_col0 + lax.broadcasted_iota(jnp.int32, (tk, tq), 1)) // CHUNK
                s = jnp.where(kc <= qc, s, -1e30)
            m_prev = m_sc[j]
            m_next = jnp.maximum(m_prev, jnp.max(s, axis=0, keepdims=True))
            alpha = jnp.exp2(m_prev - m_next)
            p = jnp.exp2(s - m_next)
            m_sc[j] = m_next
            acc_sc[j] = alpha * acc_sc[j] + jnp.dot(vt, p.astype(BF16), preferred_element_type=F32)

    n_full = q_col0 // tk
    at = lambda t: pl.multiple_of(t * tk, tk)
    scores(0, sa_sc)

    def pair(t):
        scores(at(t + 1), sb_sc)
        consume(at(t), sa_sc, False)
        scores(at(t + 2), sa_sc)
        consume(at(t + 1), sb_sc, False)

    def body(i, carry):
        pair(4 * i)
        pair(4 * i + 2)
        return carry

    lax.fori_loop(0, n_full // 4, body, 0)

    @pl.when(n_full % 4 >= 2)
    def _pair_tail():
        pair(4 * (n_full // 4))

    t0 = 2 * (n_full // 2)

    @pl.when(n_full % 2 == 1)
    def _odd_tail():
        scores(at(t0 + 1), sb_sc)
        consume(at(t0), sa_sc, False)
        consume(at(t0 + 1), sb_sc, True)

    @pl.when(n_full % 2 == 0)
    def _even_tail():
        consume(at(t0), sa_sc, True)

    outs = []
    for j in range(2):
        acc = acc_sc[j]
        outs.append(acc[:V_DIM, :] / acc[V_DIM:V_DIM + 1, :])
    o_ref[...] = jnp.concatenate(outs, axis=0).T.astype(o_ref.dtype)


def _attention_prompt(q, k, vt, tq, tk):
    b, s, _ = q.shape
    assert tq == tk and s % tq == 0 and tq % CHUNK == 0
    pairs = MLA_HEADS // 2
    return pl.pallas_call(
        functools.partial(_attn_prompt_kernel, tq=tq, tk=tk),
        grid=(b, pairs, s // tq),
        in_specs=[
            pl.BlockSpec((None, tq, HEAD_PAIR), lambda bi, p, i: (bi, i, p)),
            pl.BlockSpec((None, s, HEAD_PAIR), lambda bi, p, i: (bi, 0, p)),
            pl.BlockSpec((2 * VT_ROWS, s), lambda bi, p, i: (p, bi)),
        ],
        out_specs=pl.BlockSpec((None, tq, LANES), lambda bi, p, i: (bi, i, p)),
        out_shape=jax.ShapeDtypeStruct((b, s, MLA_WIDTH), BF16),
        scratch_shapes=[pltpu.VMEM((2, LANES, tq), BF16),
                        pltpu.VMEM((2, 1, tq), F32),
                        pltpu.VMEM((2, VT_ROWS, tq), F32),
                        pltpu.VMEM((2, tk, tq), F32),
                        pltpu.VMEM((2, tk, tq), F32)],
        compiler_params=pltpu.CompilerParams(
            dimension_semantics=("parallel", "parallel", "arbitrary"),
            vmem_limit_bytes=VMEM_LIMIT),
        name="attention_prompt",
    )(q, k, vt)


def _attn_sample_kernel(q_ref, kn_ref, latn_ref, lat_ref, kr_ref, w_uk_ref, gkn_ref, w_uv_ref, mseg_ref,
                        o_ref, k_sc, *, n_new):
    latb = lat_ref[...].astype(BF16)
    kr = kr_ref[...]
    zeros = lambda c: jnp.zeros((kr.shape[0], c), F32)
    kr_placed = jnp.concatenate([zeros(NOPE_DIM), kr, zeros(LANES - QK_DIM)], axis=1)
    kr_pair = jnp.concatenate([kr_placed, kr_placed], axis=1)
    for p in range(MLA_HEADS // 2):
        cols = slice(p * HEAD_PAIR, (p + 1) * HEAD_PAIR)
        raw = jnp.dot(latb, w_uk_ref[:, cols], preferred_element_type=F32)
        y = raw * lax.rsqrt(_segment_mean_sq(raw, mseg_ref) + EPS) * gkn_ref[:, cols]
        k_sc[:, cols] = (y + kr_pair).astype(BF16)
    v_all = jnp.dot(latb, w_uv_ref[...], preferred_element_type=F32).astype(BF16)
    v_new = jnp.dot(latn_ref[...].astype(BF16), w_uv_ref[...], preferred_element_type=F32).astype(BF16)

    q = q_ref[...]
    qt = jnp.concatenate([q] * MLA_HEADS, axis=0)
    r_head = lax.broadcasted_iota(jnp.int32, qt.shape, 0) // n_new
    c_head = lax.broadcasted_iota(jnp.int32, qt.shape, 1) // LANES
    qm = jnp.where(r_head == c_head, qt, jnp.zeros_like(qt))

    nt = (((1,), (1,)), ((), ()))
    s_old = lax.dot_general(k_sc[...], qm, nt, preferred_element_type=F32)
    s_new = lax.dot_general(kn_ref[...], qm, nt, preferred_element_type=F32)
    mx = jnp.maximum(jnp.max(s_old, axis=0, keepdims=True), jnp.max(s_new, axis=0, keepdims=True))
    p_old = jnp.exp2(s_old - mx)
    p_new = jnp.exp2(s_new - mx)
    inv = 1.0 / (jnp.sum(p_old, axis=0, keepdims=True) + jnp.sum(p_new, axis=0, keepdims=True))
    p_old = (p_old * inv).astype(BF16)
    p_new = (p_new * inv).astype(BF16)
    tn = (((0,), (0,)), ((), ()))
    full = (lax.dot_general(p_old, v_all, tn, preferred_element_type=F32)
            + lax.dot_general(p_new, v_new, tn, preferred_element_type=F32))
    v_head = lax.broadcasted_iota(jnp.int32, (n_new, MLA_WIDTH), 1) // V_DIM
    out = jnp.zeros((n_new, MLA_WIDTH), F32)
    for h in range(MLA_HEADS):
        out = out + jnp.where(v_head == h, full[h * n_new:(h + 1) * n_new, :], 0.0)
    o_ref[...] = out.astype(o_ref.dtype)


def _attention_sample(q, k_new, lat_new, cache_lat, cache_kr, layer, w, n_new):
    _, b, past, _ = cache_lat.shape
    weights = [w["w_uk"], w["gkn"], w["w_uv"], w["mseg"]]
    tok = lambda c: pl.BlockSpec((n_new, c), lambda i: (i, 0))
    return pl.pallas_call(
        functools.partial(_attn_sample_kernel, n_new=n_new),
        grid=(b,),
        in_specs=[tok(MLA_HEADS * LANES), tok(MLA_HEADS * LANES), tok(KV_LORA),
                  pl.BlockSpec((None, None, past, KV_LORA), lambda i: (layer, i, 0, 0)),
                  pl.BlockSpec((None, None, past, ROPE_DIM), lambda i: (layer, i, 0, 0))]
                 + [_full_spec(a) for a in weights],
        out_specs=tok(MLA_WIDTH),
        out_shape=jax.ShapeDtypeStruct((b * n_new, MLA_WIDTH), BF16),
        scratch_shapes=[pltpu.VMEM((past, MLA_HEADS * LANES), BF16)],
        compiler_params=pltpu.CompilerParams(dimension_semantics=("parallel",),
                                             vmem_limit_bytes=VMEM_LIMIT),
        name="attention_sample",
    )(q, k_new, lat_new, cache_lat, cache_kr, *weights)


CONV_PAD = SUBLANES


def _mlstm_kernel(qk_ref, v_ref, o_ref, misc_ref, conv0_ref, c0_ref, n0_ref, m0_ref, wconv_ref, bconv_ref,
                  bgate_ref, glstm_ref,
                  h_out, conv_out, c_out, n_out, m_out,
                  full_sc, c_sc, n_sc, m_sc, *, L):
    c = pl.program_id(1)
    last = pl.num_programs(1) - 1
    hist = CONV_W - 1
    lo = CONV_PAD - hist

    @pl.when(c == 0)
    def _init():
        full_sc[lo:CONV_PAD, :] = conv0_ref[...]
        c_sc[...] = c0_ref[...]
        n_sc[...] = n0_ref[...]
        m_sc[...] = m0_ref[...]

    full_sc[CONV_PAD:CONV_PAD + L, :] = qk_ref[...]
    y = bconv_ref[...]
    for j in range(CONV_W):
        y = y + full_sc[lo + j:lo + j + L, :] * wconv_ref[j:j + 1, :]
    qk = y * jax.nn.sigmoid(y)
    tail = full_sc[lo + L:CONV_PAD + L, :]
    full_sc[lo:CONV_PAD, :] = tail

    gs = misc_ref[...] + bgate_ref[...]
    lfs = jnp.minimum(gs, 0.0) - jnp.log1p(jnp.exp(-jnp.abs(gs)))
    row = lax.broadcasted_iota(jnp.int32, (L, L), 0)
    col = lax.broadcasted_iota(jnp.int32, (L, L), 1)
    causal = row >= col
    tri = jnp.where(causal, 1.0, 0.0).astype(BF16)
    lf_hi = lfs.astype(BF16)
    lf_mid = (lfs - lf_hi.astype(F32)).astype(BF16)
    lf_lo = (lfs - lf_hi.astype(F32) - lf_mid.astype(F32)).astype(BF16)
    b_slab = (jnp.dot(tri, lf_hi, preferred_element_type=F32)
              + jnp.dot(tri, lf_mid, preferred_element_type=F32)
              + jnp.dot(tri, lf_lo, preferred_element_type=F32))
    lane = lax.broadcasted_iota(jnp.int32, (L, LANES), 1)
    comb = jnp.where(lane < MISC_FG, gs, b_slab)
    if L < LANES:
        comb = jnp.concatenate([comb, jnp.zeros((LANES - L, LANES), F32)], axis=0)
    comb_t = comb.T

    g_all = glstm_ref[...]
    for h in range(LSTM_HEADS):
        dk = slice(h * LSTM_DK, (h + 1) * LSTM_DK)
        dv = slice(h * LSTM_DV, (h + 1) * LSTM_DV)
        ig_col = gs[:, MISC_IG + h:MISC_IG + h + 1]
        b_col = b_slab[:, MISC_FG + h:MISC_FG + h + 1]
        ig_row = comb_t[MISC_IG + h:MISC_IG + h + 1, :L]
        b_row = comb_t[MISC_FG + h:MISC_FG + h + 1, :L]
        m_prev = m_sc[:, h:h + 1]

        log_d = jnp.where(causal, b_col - b_row + ig_row, -jnp.inf)
        inter = b_col + m_prev
        m_t = jnp.maximum(inter, jnp.max(log_d, axis=1, keepdims=True))
        decay = jnp.exp(log_d - m_t)
        inter_scale = jnp.exp(inter - m_t)

        qh = qk[:, dk]
        kh = qk[:, LSTM_HEADS * LSTM_DK + h * LSTM_DK:LSTM_HEADS * LSTM_DK + (h + 1) * LSTM_DK] * (LSTM_DK ** -0.5)
        vh = v_ref[:, dv]
        qb = qh.astype(BF16)
        kb = kh.astype(BF16)
        c_prev = c_sc[h]
        n_prev = n_sc[h:h + 1, :]
        wgt = lax.dot_general(qb, kb, (((1,), (1,)), ((), ())), preferred_element_type=F32) * decay
        cq = lax.dot_general(qb, c_prev.astype(BF16), (((1,), (1,)), ((), ())), preferred_element_type=F32)
        num = jnp.dot(wgt.astype(BF16), vh.astype(BF16), preferred_element_type=F32) + inter_scale * cq
        den = (jnp.sum(wgt, axis=1, keepdims=True)
               + inter_scale * jnp.sum(qh * n_prev, axis=1, keepdims=True))
        hid = num / jnp.maximum(jnp.abs(den), jnp.exp(-m_t))

        m_new = m_t[L - 1:L, :]
        b_last = b_col[L - 1:L, :]
        carry = jnp.exp(b_last + m_prev - m_new)
        d_end = jnp.exp(b_last - b_col + ig_col - m_new)
        upd = lax.dot_general((d_end * vh).astype(BF16), kb, (((0,), (0,)), ((), ())),
                              preferred_element_type=F32)
        c_sc[h] = carry * c_prev + upd
        n_sc[h:h + 1, :] = carry * n_prev + jnp.sum(d_end * kh, axis=0, keepdims=True)
        m_sc[:, h:h + 1] = m_new

        hn = _rms(hid, g_all[:, dv])
        h_out[:, dv] = (hn * jax.nn.sigmoid(o_ref[:, dv])).astype(h_out.dtype)

    @pl.when(c == last)
    def _finish():
        conv_out[...] = tail
        c_out[...] = c_sc[...]
        n_out[...] = n_sc[...]
        m_out[...] = m_sc[...]


def _mlstm(qk_raw, v_raw, o_raw, misc, conv0, c0, n0, m0, layer, w, L):
    b = conv0.shape[1]
    t = qk_raw.shape[0]
    nc = t // (b * L)
    assert nc * b * L == t
    hist = CONV_W - 1
    tok = lambda cdim: pl.BlockSpec((L, cdim), lambda bi, ci: (bi * nc + ci, 0))
    m0 = m0.reshape(m0.shape[0], b, 1, LSTM_HEADS)
    state0 = lambda *dims: pl.BlockSpec((None, None) + dims, lambda bi, ci: (layer, bi) + (0,) * len(dims))
    weights = [w["w_conv"], w["b_conv"], w["bgate"], w["g_lstm_out"]]
    out_shape = (
        jax.ShapeDtypeStruct((t, LSTM_WIDTH), BF16),
        jax.ShapeDtypeStruct((b, hist, LSTM_QK), F32),
        jax.ShapeDtypeStruct((b, LSTM_HEADS, LSTM_DV, LSTM_DK), F32),
        jax.ShapeDtypeStruct((b, LSTM_HEADS, LSTM_DK), F32),
        jax.ShapeDtypeStruct((b, 1, LSTM_HEADS), F32),
    )
    state = lambda *dims: pl.BlockSpec((None,) + dims, lambda bi, ci: (bi,) + (0,) * len(dims))
    h, conv_new, c_new, n_new, m_new = pl.pallas_call(
        functools.partial(_mlstm_kernel, L=L),
        grid=(b, nc),
        in_specs=[tok(LSTM_QK), tok(LSTM_WIDTH), tok(LSTM_WIDTH), tok(LANES),
                  state0(hist, LSTM_QK), state0(LSTM_HEADS, LSTM_DV, LSTM_DK), state0(LSTM_HEADS, LSTM_DK),
                  state0(1, LSTM_HEADS)] + [_full_spec(a) for a in weights],
        out_specs=(tok(LSTM_WIDTH), state(hist, LSTM_QK), state(LSTM_HEADS, LSTM_DV, LSTM_DK),
                   state(LSTM_HEADS, LSTM_DK), state(1, LSTM_HEADS)),
        out_shape=out_shape,
        scratch_shapes=[pltpu.VMEM((CONV_PAD + L, LSTM_QK), F32),
                        pltpu.VMEM((LSTM_HEADS, LSTM_DV, LSTM_DK), F32),
                        pltpu.VMEM((LSTM_HEADS, LSTM_DK), F32),
                        pltpu.VMEM((1, LSTM_HEADS), F32)],
        compiler_params=pltpu.CompilerParams(dimension_semantics=("parallel", "arbitrary"),
                                             vmem_limit_bytes=VMEM_LIMIT),
        name="mlstm",
    )(qk_raw, v_raw, o_raw, misc, conv0, c0, n0, m0, *weights)
    return h, conv_new, c_new, n_new, m_new.reshape(b, LSTM_HEADS)


FF_CHUNK = 1024


def _out_ffn_kernel(x_ref, attn_ref, lstm_ref, w_out_ref, g_ffn_ref, w_up_ref, w_down_ref, y_ref):
    mix = jnp.concatenate([attn_ref[...], lstm_ref[...]], axis=1)
    x1 = x_ref[...] + jnp.dot(mix, w_out_ref[...], preferred_element_type=F32)
    xb = _rms(x1, g_ffn_ref[...]).astype(BF16)
    d_ff = w_up_ref.shape[1]
    acc = x1
    for f in range(d_ff // FF_CHUNK):
        cols = slice(f * FF_CHUNK, (f + 1) * FF_CHUNK)
        u = jnp.maximum(jnp.dot(xb, w_up_ref[:, cols], preferred_element_type=F32), 0.0)
        acc = acc + jnp.dot((u * u).astype(BF16), w_down_ref[cols, :], preferred_element_type=F32)
    y_ref[...] = acc


def _out_ffn(x2, attn, lstm, w, tm):
    t, d = x2.shape
    assert t % tm == 0
    row = lambda c: pl.BlockSpec((tm, c), lambda i: (i, 0))
    const = lambda a: pl.BlockSpec(a.shape, lambda i: (0, 0), pipeline_mode=pl.Buffered(1))
    weights = [w["w_out"], w["g_ffn"], w["w_up"], w["w_down"]]
    return pl.pallas_call(
        _out_ffn_kernel,
        grid=(t // tm,),
        in_specs=[row(d), row(MLA_WIDTH), row(LSTM_WIDTH)] + [const(a) for a in weights],
        out_specs=row(d),
        out_shape=jax.ShapeDtypeStruct((t, d), F32),
        compiler_params=pltpu.CompilerParams(dimension_semantics=("parallel",),
                                             vmem_limit_bytes=VMEM_LIMIT),
        name="out_ffn",
    )(x2, attn, lstm, *weights)


def _pack_layer(l, g_mix, w_in, g_q_lat, w_uq, g_q_nope, g_q_rope, g_kv_lat, g_k_rope, w_uk, g_k_nope, w_uv,
                w_conv, b_conv, b_igate, b_fgate, g_lstm_out, w_out, g_ffn, w_up, w_down):
    d = w_in.shape[1]
    wi = w_in[l]
    o = 0
    parts = {}
    for name, width in (("cq", Q_LORA), ("ckv", KV_LORA), ("kpe", ROPE_DIM), ("qk", LSTM_QK), ("v", LSTM_WIDTH),
                        ("o", LSTM_WIDTH), ("ig", LSTM_HEADS), ("fg", LSTM_HEADS)):
        parts[name] = wi[:, o:o + width]
        o += width
    misc = jnp.concatenate([parts["kpe"], _rot_half(parts["kpe"]), parts["ig"], parts["fg"],
                            jnp.zeros((d, LANES - MISC_FG - LSTM_HEADS), F32)], axis=1)
    w_main = jnp.concatenate([parts["cq"], parts["ckv"], parts["qk"], parts["v"], parts["o"], misc],
                             axis=1).astype(BF16)

    uq = w_uq[l].reshape(Q_LORA, MLA_HEADS, QK_DIM)
    uq = jnp.concatenate([uq, _rot_half(uq[..., NOPE_DIM:])], axis=-1).reshape(Q_LORA, MLA_HEADS * LANES)
    gq_head = jnp.concatenate([g_q_nope[l], g_q_rope[l], _rot_half(g_q_rope[l])])
    uk = w_uk[l].reshape(KV_LORA, MLA_HEADS, NOPE_DIM)
    uk = jnp.concatenate([uk, jnp.zeros_like(uk)], axis=-1).reshape(KV_LORA, MLA_HEADS * LANES)
    gkn_head = jnp.concatenate([g_k_nope[l], jnp.zeros((LANES - NOPE_DIM,), F32)])
    uv = w_uv[l].reshape(KV_LORA, MLA_HEADS, V_DIM)
    uv_t = jnp.concatenate([uv, jnp.zeros((KV_LORA, MLA_HEADS, VT_ROWS - V_DIM), F32)], axis=-1)
    uv_t = uv_t.reshape(KV_LORA, MLA_HEADS * VT_ROWS).T
    vones_head = jnp.zeros((VT_ROWS,), F32).at[V_DIM].set(1.0)
    gkm = jnp.concatenate([g_k_rope[l], _rot_half(g_k_rope[l]), jnp.zeros((LANES - 2 * ROPE_DIM,), F32)])
    bgate = jnp.concatenate([jnp.zeros((MISC_IG,), F32), b_igate[l], b_fgate[l],
                             jnp.zeros((LANES - MISC_FG - LSTM_HEADS,), F32)])

    i = jnp.arange(HEAD_PAIR)
    same = (i[:, None] // LANES) == (i[None, :] // LANES)
    li, lj = i[:, None] % LANES, i[None, :] % LANES
    mseg = jnp.where(same & (li < NOPE_DIM) & (lj < NOPE_DIM), 1.0 / NOPE_DIM,
                     jnp.where(same & (li >= NOPE_DIM) & (li < QK_DIM) & (lj >= NOPE_DIM), 1.0 / ROPE_DIM, 0.0))
    a = jnp.arange(LANES)
    mmisc = jnp.where((a[:, None] < ROPE_DIM) & (a[None, :] < 2 * ROPE_DIM), 1.0 / ROPE_DIM, 0.0)

    row = lambda v: v.reshape(1, -1).astype(F32)
    return {
        "g_mix": row(g_mix[l]), "w_main": w_main, "g_q_lat": row(g_q_lat[l]), "w_uq": uq.astype(BF16),
        "gq": row(jnp.tile(gq_head, MLA_HEADS)), "g_kv_lat": row(g_kv_lat[l]), "gkm": row(gkm),
        "w_uk": uk.astype(BF16), "gkn": row(jnp.tile(gkn_head, MLA_HEADS)), "w_uv": w_uv[l].astype(BF16),
        "w_uv_t": uv_t.astype(BF16), "vones": jnp.tile(vones_head, MLA_HEADS).reshape(-1, 1),
        "mseg": mseg.astype(BF16), "mmisc": mmisc.astype(BF16),
        "w_conv": w_conv[l], "b_conv": row(b_conv[l]), "bgate": row(bgate), "g_lstm_out": row(g_lstm_out[l]),
        "w_out": w_out[l].astype(BF16), "g_ffn": row(g_ffn[l]), "w_up": w_up[l].astype(BF16),
        "w_down": w_down[l].astype(BF16),
    }


def _rope_table(first_pos, n):
    half = ROPE_DIM // 2
    blk = min(n, LANES)
    assert n % blk == 0
    lane = jnp.arange(LANES, dtype=jnp.int32)
    inv = ROPE_BASE ** (-(lane % half).astype(F32) / half)
    ang_a = (first_pos + blk * jnp.arange(n // blk, dtype=jnp.int32)).astype(F32)[:, None] * inv[None, :]
    ang_b = jnp.arange(blk, dtype=jnp.int32).astype(F32)[:, None] * inv[None, :]
    ca, sa = jnp.cos(ang_a)[:, None, :], jnp.sin(ang_a)[:, None, :]
    cb, sb = jnp.cos(ang_b)[None, :, :], jnp.sin(ang_b)[None, :, :]
    cos = ca * cb - sa * sb
    sin = sa * cb + ca * sb
    kind = (lane // half) % 4
    tab = jnp.where(kind < 2, cos, jnp.where(kind == 2, -sin, sin))
    return tab.reshape(n, LANES)


def kernel(x_prompt, x_sample, cache_kv_latent, cache_k_rope, state_conv, state_C, state_n, state_m,
           g_mix, w_in, g_q_lat, w_uq, g_q_nope, g_q_rope, g_kv_lat, g_k_rope, w_uk, g_k_nope, w_uv,
           w_conv, b_conv, b_igate, b_fgate, g_lstm_out, w_out, g_ffn, w_up, w_down):
    depth = w_in.shape[0]
    bp, sp, d = x_prompt.shape
    bs, ls, _ = x_sample.shape
    past = cache_kv_latent.shape[2]
    hist = CONV_W - 1

    tabs_p = jnp.tile(_rope_table(0, sp), (bp, 1))
    tabs_s = jnp.tile(_rope_table(past, ls), (bs, 1))
    xp = x_prompt.reshape(bp * sp, d)
    xs = x_sample.reshape(bs * ls, d)
    zero_conv = jnp.zeros((1, bp, hist, LSTM_QK), F32)
    zero_c = jnp.zeros((1, bp, LSTM_HEADS, LSTM_DV, LSTM_DK), F32)
    zero_n = jnp.zeros((1, bp, LSTM_HEADS, LSTM_DK), F32)
    zero_m = jnp.zeros((1, bp, LSTM_HEADS), F32)

    outs = {k: [] for k in ("p_lat", "p_kr", "p_conv", "p_c", "p_n", "p_m",
                            "s_lat", "s_kr", "s_conv", "s_c", "s_n", "s_m")}
    for l in range(depth):
        w = _pack_layer(l, g_mix, w_in, g_q_lat, w_uq, g_q_nope, g_q_rope, g_kv_lat, g_k_rope, w_uk, g_k_nope,
                        w_uv, w_conv, b_conv, b_igate, b_fgate, g_lstm_out, w_out, g_ffn, w_up, w_down)
        q, k, v, lat, kr, qk_raw, v_raw, o_raw, misc = _projection(xp, tabs_p, w, tm=512)
        attn = _attention_prompt(q.reshape(bp, sp, -1), k.reshape(bp, sp, -1), v,
                                 tq=ATTN_TQ, tk=ATTN_TK)
        h, conv_new, c_new, n_new, m_new = _mlstm(qk_raw, v_raw, o_raw, misc, zero_conv, zero_c, zero_n, zero_m,
                                                  0, w, L=MLSTM_TILE)
        xp = _out_ffn(xp, attn.reshape(bp * sp, -1), h, w, tm=512)
        outs["p_lat"].append(lat.reshape(bp, sp, KV_LORA))
        outs["p_kr"].append(kr.reshape(bp, sp, ROPE_DIM))
        outs["p_conv"].append(conv_new)
        outs["p_c"].append(c_new)
        outs["p_n"].append(n_new)
        outs["p_m"].append(m_new)
        q, k, v, lat, kr, qk_raw, v_raw, o_raw, misc = _projection(xs, tabs_s, w, tm=bs * ls)
        attn = _attention_sample(q, k, lat, cache_kv_latent, cache_k_rope, l, w, n_new=ls)
        h, conv_new, c_new, n_new, m_new = _mlstm(qk_raw, v_raw, o_raw, misc, state_conv, state_C,
                                                  state_n, state_m, l, w, L=ls)
        xs = _out_ffn(xs, attn, h, w, tm=bs * ls)
        outs["s_lat"].append(lat.reshape(bs, ls, KV_LORA))
        outs["s_kr"].append(kr.reshape(bs, ls, ROPE_DIM))
        outs["s_conv"].append(conv_new)
        outs["s_c"].append(c_new)
        outs["s_n"].append(n_new)
        outs["s_m"].append(m_new)

    st = lambda key: jnp.stack(outs[key])
    return (xp.reshape(bp, sp, d), xs.reshape(bs, ls, d),
            st("p_lat"), st("p_kr"), st("p_conv"), st("p_c"), st("p_n"), st("p_m"),
            st("s_lat"), st("s_kr"), st("s_conv"), st("s_c"), st("s_n"), st("s_m"))
```

```python
import functools

import jax
import jax.numpy as jnp
from jax import lax
from jax.experimental import pallas as pl
from jax.experimental.pallas import tpu as pltpu

F32 = jnp.float32
BF16 = jnp.bfloat16

EPS = 1e-6
CHUNK = 64
MLA_HEADS = 8
NOPE_DIM = 64
ROPE_DIM = 32
QK_DIM = NOPE_DIM + ROPE_DIM
V_DIM = 64
Q_LORA = 256
KV_LORA = 128
ROPE_BASE = 10000.0
LSTM_HEADS = 4
LSTM_DK = 128
LSTM_DV = 128
CONV_W = 4
LSTM_WIDTH = LSTM_HEADS * LSTM_DV
LSTM_QK = 2 * LSTM_HEADS * LSTM_DK
MLA_WIDTH = MLA_HEADS * V_DIM

LANES = 128
SUBLANES = 8
HEAD_PAIR = 2 * LANES
VMEM_LIMIT = 52 * 1024 * 1024
Q_LOG2_SCALE = (QK_DIM ** -0.5) * 1.4426950408889634
ATTN_TQ = 512
ATTN_TK = 512
PAIRS_PER_TRIP = 4
MLSTM_TILE = 256
VT_ROWS = 80

MISC_KPE = 0
MISC_KPE_ROT = ROPE_DIM
MISC_IG = 2 * ROPE_DIM
MISC_FG = MISC_IG + LSTM_HEADS

COL_CQ = (0, Q_LORA)
COL_CKV = (COL_CQ[1], COL_CQ[1] + KV_LORA)
COL_QK = (COL_CKV[1], COL_CKV[1] + LSTM_QK)
COL_V = (COL_QK[1], COL_QK[1] + LSTM_WIDTH)
COL_O = (COL_V[1], COL_V[1] + LSTM_WIDTH)
COL_MISC = (COL_O[1], COL_O[1] + LANES)
PACKED_COLS = COL_MISC[1]


def _rot_half(a):
    half = ROPE_DIM // 2
    return jnp.concatenate([a[..., half:], a[..., :half]], axis=-1)


def _rms(x, g):
    return x * lax.rsqrt(jnp.mean(x * x, axis=-1, keepdims=True) + EPS) * g


def _segment_mean_sq(y, m_ref):
    return jnp.dot((y * y).astype(BF16), m_ref[...], preferred_element_type=F32)


def _proj_kernel(x_ref, rope_ref, g_mix_ref, w_main_ref, g_qlat_ref, w_uq_ref,
                 gq_ref, g_kvlat_ref, gkm_ref, w_uk_ref, gkn_ref, w_uv_ref, vones_ref, mseg_ref, mmisc_ref,
                 q_out, k_out, v_out, lat_out, kr_out, qk_out, vl_out, o_out, misc_out):
    x = x_ref[...]
    xb = _rms(x, g_mix_ref[...]).astype(BF16)

    def proj(col):
        return jnp.dot(xb, w_main_ref[:, col[0]:col[1]], preferred_element_type=F32)

    qk_out[...] = proj(COL_QK)
    vl_out[...] = proj(COL_V)
    o_out[...] = proj(COL_O)
    misc = proj(COL_MISC)
    misc_out[...] = misc

    cqn = _rms(proj(COL_CQ), g_qlat_ref[...]).astype(BF16)
    tab = rope_ref[...]
    tab_next = pltpu.roll(tab, LANES - ROPE_DIM, 1)
    lane = lax.broadcasted_iota(jnp.int32, tab.shape, 1)
    in_rope = (lane >= NOPE_DIM) & (lane < QK_DIM)
    cq_tab = jnp.where(lane < NOPE_DIM, 1.0, jnp.where(in_rope, tab, 0.0))
    sq_tab = jnp.where(in_rope, tab_next, 0.0)
    ck_tab = jnp.where(lane < ROPE_DIM, tab, 0.0)
    sk_tab = jnp.where(lane < ROPE_DIM, tab_next, 0.0)
    for p in range(MLA_HEADS // 2):
        cols = slice(p * HEAD_PAIR, (p + 1) * HEAD_PAIR)
        raw = jnp.dot(cqn, w_uq_ref[:, cols], preferred_element_type=F32)
        y = raw * lax.rsqrt(_segment_mean_sq(raw, mseg_ref) + EPS) * gq_ref[:, cols]
        for j in range(2):
            yh = y[:, j * LANES:(j + 1) * LANES]
            qh = yh * cq_tab + pltpu.roll(yh, LANES - ROPE_DIM, 1) * sq_tab
            h = 2 * p + j
            q_out[:, h * LANES:(h + 1) * LANES] = (qh * Q_LOG2_SCALE).astype(BF16)

    lat = _rms(proj(COL_CKV), g_kvlat_ref[...])
    lat_out[...] = lat
    latb = lat.astype(BF16)
    lat_t = lat.T.astype(BF16)
    v_out[...] = (jnp.dot(w_uv_ref[...], lat_t, preferred_element_type=F32) + vones_ref[...]).astype(BF16)

    ms = misc * lax.rsqrt(_segment_mean_sq(misc, mmisc_ref) + EPS) * gkm_ref[...]
    kr = ms * ck_tab + pltpu.roll(ms, LANES - ROPE_DIM, 1) * sk_tab
    kr_out[...] = kr[:, :ROPE_DIM]
    kr_placed = pltpu.roll(kr, NOPE_DIM, 1)
    kr_pair = jnp.concatenate([kr_placed, kr_placed], axis=1)
    for p in range(MLA_HEADS // 2):
        cols = slice(p * HEAD_PAIR, (p + 1) * HEAD_PAIR)
        raw = jnp.dot(latb, w_uk_ref[:, cols], preferred_element_type=F32)
        y = raw * lax.rsqrt(_segment_mean_sq(raw, mseg_ref) + EPS) * gkn_ref[:, cols]
        k_out[:, cols] = (y + kr_pair).astype(BF16)


def _full_spec(a):
    nd = a.ndim
    return pl.BlockSpec(a.shape, lambda *_: (0,) * nd)


def _projection(x2, rope_tab, w, tm):
    t = x2.shape[0]
    assert t % tm == 0
    row = lambda c: pl.BlockSpec((tm, c), lambda i: (i, 0))
    weights = [w["g_mix"], w["w_main"], w["g_q_lat"], w["w_uq"], w["gq"], w["g_kv_lat"], w["gkm"],
               w["w_uk"], w["gkn"], w["w_uv_t"], w["vones"], w["mseg"], w["mmisc"]]
    out_shape = (
        jax.ShapeDtypeStruct((t, MLA_HEADS * LANES), BF16),
        jax.ShapeDtypeStruct((t, MLA_HEADS * LANES), BF16),
        jax.ShapeDtypeStruct((MLA_HEADS * VT_ROWS, t), BF16),
        jax.ShapeDtypeStruct((t, KV_LORA), F32),
        jax.ShapeDtypeStruct((t, ROPE_DIM), F32),
        jax.ShapeDtypeStruct((t, LSTM_QK), F32),
        jax.ShapeDtypeStruct((t, LSTM_WIDTH), F32),
        jax.ShapeDtypeStruct((t, LSTM_WIDTH), F32),
        jax.ShapeDtypeStruct((t, LANES), F32),
    )
    return pl.pallas_call(
        _proj_kernel,
        grid=(t // tm,),
        in_specs=[row(x2.shape[1]), row(LANES)] + [_full_spec(a) for a in weights],
        out_specs=tuple(pl.BlockSpec((MLA_HEADS * VT_ROWS, tm), lambda i: (0, i)) if n == 2 else row(s.shape[1])
                        for n, s in enumerate(out_shape)),
        out_shape=out_shape,
        compiler_params=pltpu.CompilerParams(dimension_semantics=("parallel",),
                                             vmem_limit_bytes=VMEM_LIMIT),
        name="projection",
    )(x2, rope_tab, *weights)


def _attn_prompt_kernel(q_ref, k_ref, vt_ref, o_ref, qt_sc, m_sc, acc_sc, sa_sc, sb_sc, *, tq, tk):
    qi = pl.program_id(2)
    m_sc[...] = jnp.full(m_sc.shape, -1e30, F32)
    acc_sc[...] = jnp.zeros(acc_sc.shape, F32)
    q_col0 = qi * tq
    for j in range(2):
        qt_sc[j] = q_ref[:, j * LANES:(j + 1) * LANES].astype(F32).T.astype(BF16)

    def scores(k0, s_ref):
        for j in range(2):
            k = k_ref[pl.ds(k0, tk), j * LANES:(j + 1) * LANES]
            s_ref[j] = jnp.dot(k, qt_sc[j], preferred_element_type=F32)

    def consume(k0, s_ref, masked):
        for j in range(2):
            vt = vt_ref[j * VT_ROWS:(j + 1) * VT_ROWS, pl.ds(k0, tk)]
            s = s_ref[j]
            if masked:
                kc = (k0 + lax.broadcasted_iota(jnp.int32, (tk, tq), 0)) // CHUNK
                qc = (q_col0 + lax.broadcasted_iota(jnp.int32, (tk, tq), 1)) // CHUNK
                s = jnp.where(kc <= qc, s, -1e30)
            m_prev = m_sc[j]
            m_next = jnp.maximum(m_prev, jnp.max(s, axis=0, keepdims=True))
            alpha = jnp.exp2(m_prev - m_next)
            p = jnp.exp2(s - m_next)
            m_sc[j] = m_next
            acc_sc[j] = alpha * acc_sc[j] + jnp.dot(vt, p.astype(BF16), preferred_element_type=F32)

    n_full = q_col0 // tk
    at = lambda t: pl.multiple_of(t * tk, tk)
    scores(0, sa_sc)

    def pair(t):
        scores(at(t + 1), sb_sc)
        consume(at(t), sa_sc, False)
        scores(at(t + 2), sa_sc)
        consume(at(t + 1), sb_sc, False)

    def body(i, carry):
        for u in range(PAIRS_PER_TRIP):
            pair(2 * PAIRS_PER_TRIP * i + 2 * u)
        return carry

    trips = n_full // (2 * PAIRS_PER_TRIP)
    lax.fori_loop(0, trips, body, 0)

    def rest(i, carry):
        pair(2 * PAIRS_PER_TRIP * trips + 2 * i)
        return carry

    lax.fori_loop(0, (n_full % (2 * PAIRS_PER_TRIP)) // 2, rest, 0)

    t0 = 2 * (n_full // 2)

    @pl.when(n_full % 2 == 1)
    def _odd_tail():
        scores(at(t0 + 1), sb_sc)
        consume(at(t0), sa_sc, False)
        consume(at(t0 + 1), sb_sc, True)

    @pl.when(n_full % 2 == 0)
    def _even_tail():
        consume(at(t0), sa_sc, True)

    outs = []
    for j in range(2):
        acc = acc_sc[j]
        outs.append(acc[:V_DIM, :] / acc[V_DIM:V_DIM + 1, :])
    o_ref[...] = jnp.concatenate(outs, axis=0).T.astype(o_ref.dtype)


def _attention_prompt(q, k, vt, tq, tk):
    b, s, _ = q.shape
    assert tq == tk and s % tq == 0 and tq % CHUNK == 0
    pairs = MLA_HEADS // 2
    return pl.pallas_call(
        functools.partial(_attn_prompt_kernel, tq=tq, tk=tk),
        grid=(b, pairs, s // tq),
        in_specs=[
            pl.BlockSpec((None, tq, HEAD_PAIR), lambda bi, p, i: (bi, i, p)),
            pl.BlockSpec((None, s, HEAD_PAIR), lambda bi, p, i: (bi, 0, p)),
            pl.BlockSpec((2 * VT_ROWS, s), lambda bi, p, i: (p, bi)),
        ],
        out_specs=pl.BlockSpec((None, tq, LANES), lambda bi, p, i: (bi, i, p)),
        out_shape=jax.ShapeDtypeStruct((b, s, MLA_WIDTH), BF16),
        scratch_shapes=[pltpu.VMEM((2, LANES, tq), BF16),
                        pltpu.VMEM((2, 1, tq), F32),
                        pltpu.VMEM((2, VT_ROWS, tq), F32),
                        pltpu.VMEM((2, tk, tq), F32),
                        pltpu.VMEM((2, tk, tq), F32)],
        compiler_params=pltpu.CompilerParams(
            dimension_semantics=("parallel", "parallel", "arbitrary"),
            vmem_limit_bytes=VMEM_LIMIT),
        name="attention_prompt",
    )(q, k, vt)


def _attn_sample_kernel(q_ref, kn_ref, latn_ref, lat_ref, kr_ref, w_uk_ref, gkn_ref, w_uv_ref, mseg_ref,
                        o_ref, k_sc, *, n_new):
    latb = lat_ref[...].astype(BF16)
    kr = kr_ref[...]
    zeros = lambda c: jnp.zeros((kr.shape[0], c), F32)
    kr_placed = jnp.concatenate([zeros(NOPE_DIM), kr, zeros(LANES - QK_DIM)], axis=1)
    kr_pair = jnp.concatenate([kr_placed, kr_placed], axis=1)
    for p in range(MLA_HEADS // 2):
        cols = slice(p * HEAD_PAIR, (p + 1) * HEAD_PAIR)
        raw = jnp.dot(latb, w_uk_ref[:, cols], preferred_element_type=F32)
        y = raw * lax.rsqrt(_segment_mean_sq(raw, mseg_ref) + EPS) * gkn_ref[:, cols]
        k_sc[:, cols] = (y + kr_pair).astype(BF16)
    v_all = jnp.dot(latb, w_uv_ref[...], preferred_element_type=F32).astype(BF16)
    v_new = jnp.dot(latn_ref[...].astype(BF16), w_uv_ref[...], preferred_element_type=F32).astype(BF16)

    q = q_ref[...]
    qt = jnp.concatenate([q] * MLA_HEADS, axis=0)
    r_head = lax.broadcasted_iota(jnp.int32, qt.shape, 0) // n_new
    c_head = lax.broadcasted_iota(jnp.int32, qt.shape, 1) // LANES
    qm = jnp.where(r_head == c_head, qt, jnp.zeros_like(qt))

    nt = (((1,), (1,)), ((), ()))
    s_old = lax.dot_general(k_sc[...], qm, nt, preferred_element_type=F32)
    s_new = lax.dot_general(kn_ref[...], qm, nt, preferred_element_type=F32)
    mx = jnp.maximum(jnp.max(s_old, axis=0, keepdims=True), jnp.max(s_new, axis=0, keepdims=True))
    p_old = jnp.exp2(s_old - mx)
    p_new = jnp.exp2(s_new - mx)
    inv = 1.0 / (jnp.sum(p_old, axis=0, keepdims=True) + jnp.sum(p_new, axis=0, keepdims=True))
    p_old = (p_old * inv).astype(BF16)
    p_new = (p_new * inv).astype(BF16)
    tn = (((0,), (0,)), ((), ()))
    full = (lax.dot_general(p_old, v_all, tn, preferred_element_type=F32)
            + lax.dot_general(p_new, v_new, tn, preferred_element_type=F32))
    v_head = lax.broadcasted_iota(jnp.int32, (n_new, MLA_WIDTH), 1) // V_DIM
    out = jnp.zeros((n_new, MLA_WIDTH), F32)
    for h in range(MLA_HEADS):
        out = out + jnp.where(v_head == h, full[h * n_new:(h + 1) * n_new, :], 0.0)
    o_ref[...] = out.astype(o_ref.dtype)


def _attention_sample(q, k_new, lat_new, cache_lat, cache_kr, layer, w, n_new):
    _, b, past, _ = cache_lat.shape
    weights = [w["w_uk"], w["gkn"], w["w_uv"], w["mseg"]]
    tok = lambda c: pl.BlockSpec((n_new, c), lambda i: (i, 0))
    return pl.pallas_call(
        functools.partial(_attn_sample_kernel, n_new=n_new),
        grid=(b,),
        in_specs=[tok(MLA_HEADS * LANES), tok(MLA_HEADS * LANES), tok(KV_LORA),
                  pl.BlockSpec((None, None, past, KV_LORA), lambda i: (layer, i, 0, 0)),
                  pl.BlockSpec((None, None, past, ROPE_DIM), lambda i: (layer, i, 0, 0))]
                 + [_full_spec(a) for a in weights],
        out_specs=tok(MLA_WIDTH),
        out_shape=jax.ShapeDtypeStruct((b * n_new, MLA_WIDTH), BF16),
        scratch_shapes=[pltpu.VMEM((past, MLA_HEADS * LANES), BF16)],
        compiler_params=pltpu.CompilerParams(dimension_semantics=("parallel",),
                                             vmem_limit_bytes=VMEM_LIMIT),
        name="attention_sample",
    )(q, k_new, lat_new, cache_lat, cache_kr, *weights)


CONV_PAD = SUBLANES


def _mlstm_kernel(qk_ref, v_ref, o_ref, misc_ref, conv0_ref, c0_ref, n0_ref, m0_ref, wconv_ref, bconv_ref,
                  bgate_ref, glstm_ref,
                  h_out, conv_out, c_out, n_out, m_out,
                  full_sc, c_sc, n_sc, m_sc, *, L):
    c = pl.program_id(1)
    last = pl.num_programs(1) - 1
    hist = CONV_W - 1
    lo = CONV_PAD - hist

    @pl.when(c == 0)
    def _init():
        full_sc[lo:CONV_PAD, :] = conv0_ref[...]
        c_sc[...] = c0_ref[...]
        n_sc[...] = n0_ref[...]
        m_sc[...] = m0_ref[...]

    full_sc[CONV_PAD:CONV_PAD + L, :] = qk_ref[...]
    y = bconv_ref[...]
    for j in range(CONV_W):
        y = y + full_sc[lo + j:lo + j + L, :] * wconv_ref[j:j + 1, :]
    qk = y * jax.nn.sigmoid(y)
    tail = full_sc[lo + L:CONV_PAD + L, :]
    full_sc[lo:CONV_PAD, :] = tail

    LP = max(L, LANES)
    pad_rows = lambda a: a if L == LP else jnp.concatenate([a, jnp.zeros((LP - L, a.shape[1]), a.dtype)], axis=0)
    gs_t = pad_rows(misc_ref[...] + bgate_ref[...]).T
    g8 = gs_t[MISC_IG:MISC_IG + 2 * LSTM_HEADS, :]
    lf8 = jnp.minimum(g8, 0.0) - jnp.log1p(jnp.exp(-jnp.abs(g8)))
    s_idx = lax.broadcasted_iota(jnp.int32, (LP, LP), 0)
    t_idx = lax.broadcasted_iota(jnp.int32, (LP, LP), 1)
    causal = s_idx <= t_idx
    triu = jnp.where(causal, 1.0, 0.0).astype(BF16)
    lf_hi = lf8.astype(BF16)
    lf_mid = (lf8 - lf_hi.astype(F32)).astype(BF16)
    lf_lo = (lf8 - lf_hi.astype(F32) - lf_mid.astype(F32)).astype(BF16)
    b8 = (jnp.dot(lf_hi, triu, preferred_element_type=F32)
          + jnp.dot(lf_mid, triu, preferred_element_type=F32)
          + jnp.dot(lf_lo, triu, preferred_element_type=F32))
    c4 = g8[:LSTM_HEADS, :] - b8[LSTM_HEADS:, :]
    c_cols = jnp.concatenate([c4, jnp.zeros((LANES - LSTM_HEADS, LP), F32)], axis=0).T
    lane_t = lax.broadcasted_iota(jnp.int32, (1, LP), 1)

    for h in range(LSTM_HEADS):
        dk = slice(h * LSTM_DK, (h + 1) * LSTM_DK)
        dv = slice(h * LSTM_DV, (h + 1) * LSTM_DV)
        ig_row = g8[h:h + 1, :]
        b_row = b8[LSTM_HEADS + h:LSTM_HEADS + h + 1, :]
        c_col = c_cols[:, h:h + 1]
        m_prev = m_sc[:, h:h + 1]

        log_d = jnp.where(causal, c_col + b_row, -jnp.inf)
        inter = b_row + m_prev
        m_t = jnp.maximum(inter, jnp.max(log_d, axis=0, keepdims=True))
        decay = jnp.exp(log_d - m_t)
        inter_scale = jnp.exp(inter - m_t)

        qb = pad_rows(qk[:, dk]).astype(BF16)
        kb = pad_rows(qk[:, LSTM_HEADS * LSTM_DK + h * LSTM_DK:LSTM_HEADS * LSTM_DK + (h + 1) * LSTM_DK]
                      * (LSTM_DK ** -0.5)).astype(BF16)
        v_t = pad_rows(v_ref[:, dv]).T
        c_prev = c_sc[h]
        n_prev = n_sc[h:h + 1, :]
        nt = (((1,), (1,)), ((), ()))
        wgt = lax.dot_general(kb, qb, nt, preferred_element_type=F32) * decay
        cq = lax.dot_general(c_prev.astype(BF16), qb, nt, preferred_element_type=F32)
        nq = lax.dot_general(jnp.broadcast_to(n_prev, (SUBLANES, LSTM_DK)).astype(BF16), qb, nt,
                             preferred_element_type=F32)[:1, :]
        num = jnp.dot(v_t.astype(BF16), wgt.astype(BF16), preferred_element_type=F32) + inter_scale * cq
        den = jnp.sum(wgt, axis=0, keepdims=True) + inter_scale * nq
        hid = num / jnp.maximum(jnp.abs(den), jnp.exp(-m_t))

        m_new = m_t[:, L - 1:L]
        b_last = b_row[:, L - 1:L]
        carry = jnp.exp(b_last + m_prev - m_new)
        d_end = jnp.exp(b_last - b_row + ig_row - m_new)
        if L < LP:
            d_end = jnp.where(lane_t < L, d_end, 0.0)
        upd = jnp.dot((v_t * d_end).astype(BF16), kb, preferred_element_type=F32)
        nk = jnp.dot(jnp.broadcast_to(d_end, (SUBLANES, LP)).astype(BF16), kb,
                     preferred_element_type=F32)[:1, :]
        c_sc[h] = carry * c_prev + upd
        n_sc[h:h + 1, :] = carry * n_prev + nk
        m_sc[:, h:h + 1] = m_new

        g_col = jnp.concatenate([glstm_ref[dv, :]] * (LP // LANES), axis=1)
        hn = hid * lax.rsqrt(jnp.mean(hid * hid, axis=0, keepdims=True) + EPS) * g_col
        h_out[:, dv] = (hn.T[:L, :] * jax.nn.sigmoid(o_ref[:, dv])).astype(h_out.dtype)

    @pl.when(c == last)
    def _finish():
        conv_out[...] = tail
        c_out[...] = c_sc[...]
        n_out[...] = n_sc[...]
        m_out[...] = m_sc[...]


def _mlstm(qk_raw, v_raw, o_raw, misc, conv0, c0, n0, m0, layer, w, L):
    b = conv0.shape[1]
    t = qk_raw.shape[0]
    nc = t // (b * L)
    assert nc * b * L == t
    hist = CONV_W - 1
    tok = lambda cdim: pl.BlockSpec((L, cdim), lambda bi, ci: (bi * nc + ci, 0))
    m0 = m0.reshape(m0.shape[0], b, 1, LSTM_HEADS)
    state0 = lambda *dims: pl.BlockSpec((None, None) + dims, lambda bi, ci: (layer, bi) + (0,) * len(dims))
    weights = [w["w_conv"], w["b_conv"], w["bgate"], w["g_lstm_out"]]
    out_shape = (
        jax.ShapeDtypeStruct((t, LSTM_WIDTH), BF16),
        jax.ShapeDtypeStruct((b, hist, LSTM_QK), F32),
        jax.ShapeDtypeStruct((b, LSTM_HEADS, LSTM_DV, LSTM_DK), F32),
        jax.ShapeDtypeStruct((b, LSTM_HEADS, LSTM_DK), F32),
        jax.ShapeDtypeStruct((b, 1, LSTM_HEADS), F32),
    )
    state = lambda *dims: pl.BlockSpec((None,) + dims, lambda bi, ci: (bi,) + (0,) * len(dims))
    h, conv_new, c_new, n_new, m_new = pl.pallas_call(
        functools.partial(_mlstm_kernel, L=L),
        grid=(b, nc),
        in_specs=[tok(LSTM_QK), tok(LSTM_WIDTH), tok(LSTM_WIDTH), tok(LANES),
                  state0(hist, LSTM_QK), state0(LSTM_HEADS, LSTM_DV, LSTM_DK), state0(LSTM_HEADS, LSTM_DK),
                  state0(1, LSTM_HEADS)] + [_full_spec(a) for a in weights],
        out_specs=(tok(LSTM_WIDTH), state(hist, LSTM_QK), state(LSTM_HEADS, LSTM_DV, LSTM_DK),
                   state(LSTM_HEADS, LSTM_DK), state(1, LSTM_HEADS)),
        out_shape=out_shape,
        scratch_shapes=[pltpu.VMEM((CONV_PAD + L, LSTM_QK), F32),
                        pltpu.VMEM((LSTM_HEADS, LSTM_DV, LSTM_DK), F32),
                        pltpu.VMEM((LSTM_HEADS, LSTM_DK), F32),
                        pltpu.VMEM((1, LSTM_HEADS), F32)],
        compiler_params=pltpu.CompilerParams(dimension_semantics=("parallel", "arbitrary"),
                                             vmem_limit_bytes=VMEM_LIMIT),
        name="mlstm",
    )(qk_raw, v_raw, o_raw, misc, conv0, c0, n0, m0, *weights)
    return h, conv_new, c_new, n_new, m_new.reshape(b, LSTM_HEADS)


FF_CHUNK = 1024


def _out_ffn_kernel(x_ref, attn_ref, lstm_ref, w_out_ref, g_ffn_ref, w_up_ref, w_down_ref, y_ref):
    mix = jnp.concatenate([attn_ref[...], lstm_ref[...]], axis=1)
    x1 = x_ref[...] + jnp.dot(mix, w_out_ref[...], preferred_element_type=F32)
    xb = _rms(x1, g_ffn_ref[...]).astype(BF16)
    d_ff = w_up_ref.shape[1]
    acc = x1
    for f in range(d_ff // FF_CHUNK):
        cols = slice(f * FF_CHUNK, (f + 1) * FF_CHUNK)
        u = jnp.maximum(jnp.dot(xb, w_up_ref[:, cols], preferred_element_type=F32), 0.0)
        acc = acc + jnp.dot((u * u).astype(BF16), w_down_ref[cols, :], preferred_element_type=F32)
    y_ref[...] = acc


def _out_ffn(x2, attn, lstm, w, tm):
    t, d = x2.shape
    assert t % tm == 0
    row = lambda c: pl.BlockSpec((tm, c), lambda i: (i, 0))
    const = lambda a: pl.BlockSpec(a.shape, lambda i: (0, 0), pipeline_mode=pl.Buffered(1))
    weights = [w["w_out"], w["g_ffn"], w["w_up"], w["w_down"]]
    return pl.pallas_call(
        _out_ffn_kernel,
        grid=(t // tm,),
        in_specs=[row(d), row(MLA_WIDTH), row(LSTM_WIDTH)] + [const(a) for a in weights],
        out_specs=row(d),
        out_shape=jax.ShapeDtypeStruct((t, d), F32),
        compiler_params=pltpu.CompilerParams(dimension_semantics=("parallel",),
                                             vmem_limit_bytes=VMEM_LIMIT),
        name="out_ffn",
    )(x2, attn, lstm, *weights)


def _pack_layer(l, g_mix, w_in, g_q_lat, w_uq, g_q_nope, g_q_rope, g_kv_lat, g_k_rope, w_uk, g_k_nope, w_uv,
                w_conv, b_conv, b_igate, b_fgate, g_lstm_out, w_out, g_ffn, w_up, w_down):
    d = w_in.shape[1]
    wi = w_in[l]
    o = 0
    parts = {}
    for name, width in (("cq", Q_LORA), ("ckv", KV_LORA), ("kpe", ROPE_DIM), ("qk", LSTM_QK), ("v", LSTM_WIDTH),
                        ("o", LSTM_WIDTH), ("ig", LSTM_HEADS), ("fg", LSTM_HEADS)):
        parts[name] = wi[:, o:o + width]
        o += width
    misc = jnp.concatenate([parts["kpe"], _rot_half(parts["kpe"]), parts["ig"], parts["fg"],
                            jnp.zeros((d, LANES - MISC_FG - LSTM_HEADS), F32)], axis=1)
    w_main = jnp.concatenate([parts["cq"], parts["ckv"], parts["qk"], parts["v"], parts["o"], misc],
                             axis=1).astype(BF16)

    uq = w_uq[l].reshape(Q_LORA, MLA_HEADS, QK_DIM)
    uq = jnp.concatenate([uq, _rot_half(uq[..., NOPE_DIM:])], axis=-1).reshape(Q_LORA, MLA_HEADS * LANES)
    gq_head = jnp.concatenate([g_q_nope[l], g_q_rope[l], _rot_half(g_q_rope[l])])
    uk = w_uk[l].reshape(KV_LORA, MLA_HEADS, NOPE_DIM)
    uk = jnp.concatenate([uk, jnp.zeros_like(uk)], axis=-1).reshape(KV_LORA, MLA_HEADS * LANES)
    gkn_head = jnp.concatenate([g_k_nope[l], jnp.zeros((LANES - NOPE_DIM,), F32)])
    uv = w_uv[l].reshape(KV_LORA, MLA_HEADS, V_DIM)
    uv_t = jnp.concatenate([uv, jnp.zeros((KV_LORA, MLA_HEADS, VT_ROWS - V_DIM), F32)], axis=-1)
    uv_t = uv_t.reshape(KV_LORA, MLA_HEADS * VT_ROWS).T
    vones_head = jnp.zeros((VT_ROWS,), F32).at[V_DIM].set(1.0)
    gkm = jnp.concatenate([g_k_rope[l], _rot_half(g_k_rope[l]), jnp.zeros((LANES - 2 * ROPE_DIM,), F32)])
    bgate = jnp.concatenate([jnp.zeros((MISC_IG,), F32), b_igate[l], b_fgate[l],
                             jnp.zeros((LANES - MISC_FG - LSTM_HEADS,), F32)])

    i = jnp.arange(HEAD_PAIR)
    same = (i[:, None] // LANES) == (i[None, :] // LANES)
    li, lj = i[:, None] % LANES, i[None, :] % LANES
    mseg = jnp.where(same & (li < NOPE_DIM) & (lj < NOPE_DIM), 1.0 / NOPE_DIM,
                     jnp.where(same & (li >= NOPE_DIM) & (li < QK_DIM) & (lj >= NOPE_DIM), 1.0 / ROPE_DIM, 0.0))
    a = jnp.arange(LANES)
    mmisc = jnp.where((a[:, None] < ROPE_DIM) & (a[None, :] < 2 * ROPE_DIM), 1.0 / ROPE_DIM, 0.0)

    row = lambda v: v.reshape(1, -1).astype(F32)
    return {
        "g_mix": row(g_mix[l]), "w_main": w_main, "g_q_lat": row(g_q_lat[l]), "w_uq": uq.astype(BF16),
        "gq": row(jnp.tile(gq_head, MLA_HEADS)), "g_kv_lat": row(g_kv_lat[l]), "gkm": row(gkm),
        "w_uk": uk.astype(BF16), "gkn": row(jnp.tile(gkn_head, MLA_HEADS)), "w_uv": w_uv[l].astype(BF16),
        "w_uv_t": uv_t.astype(BF16), "vones": jnp.tile(vones_head, MLA_HEADS).reshape(-1, 1),
        "mseg": mseg.astype(BF16), "mmisc": mmisc.astype(BF16),
        "w_conv": w_conv[l], "b_conv": row(b_conv[l]), "bgate": row(bgate), "g_lstm_out": jnp.broadcast_to(g_lstm_out[l][:, None], (LSTM_WIDTH, LANES)),
        "w_out": w_out[l].astype(BF16), "g_ffn": row(g_ffn[l]), "w_up": w_up[l].astype(BF16),
        "w_down": w_down[l].astype(BF16),
    }


def _rope_table(first_pos, n):
    half = ROPE_DIM // 2
    blk = min(n, LANES)
    assert n % blk == 0
    lane = jnp.arange(LANES, dtype=jnp.int32)
    inv = ROPE_BASE ** (-(lane % half).astype(F32) / half)
    ang_a = (first_pos + blk * jnp.arange(n // blk, dtype=jnp.int32)).astype(F32)[:, None] * inv[None, :]
    ang_b = jnp.arange(blk, dtype=jnp.int32).astype(F32)[:, None] * inv[None, :]
    ca, sa = jnp.cos(ang_a)[:, None, :], jnp.sin(ang_a)[:, None, :]
    cb, sb = jnp.cos(ang_b)[None, :, :], jnp.sin(ang_b)[None, :, :]
    cos = ca * cb - sa * sb
    sin = sa * cb + ca * sb
    kind = (lane // half) % 4
    tab = jnp.where(kind < 2, cos, jnp.where(kind == 2, -sin, sin))
    return tab.reshape(n, LANES)


def kernel(x_prompt, x_sample, cache_kv_latent, cache_k_rope, state_conv, state_C, state_n, state_m,
           g_mix, w_in, g_q_lat, w_uq, g_q_nope, g_q_rope, g_kv_lat, g_k_rope, w_uk, g_k_nope, w_uv,
           w_conv, b_conv, b_igate, b_fgate, g_lstm_out, w_out, g_ffn, w_up, w_down):
    depth = w_in.shape[0]
    bp, sp, d = x_prompt.shape
    bs, ls, _ = x_sample.shape
    past = cache_kv_latent.shape[2]
    hist = CONV_W - 1

    tabs_p = jnp.tile(_rope_table(0, sp), (bp, 1))
    tabs_s = jnp.tile(_rope_table(past, ls), (bs, 1))
    xp = x_prompt.reshape(bp * sp, d)
    xs = x_sample.reshape(bs * ls, d)
    zero_conv = jnp.zeros((1, bp, hist, LSTM_QK), F32)
    zero_c = jnp.zeros((1, bp, LSTM_HEADS, LSTM_DV, LSTM_DK), F32)
    zero_n = jnp.zeros((1, bp, LSTM_HEADS, LSTM_DK), F32)
    zero_m = jnp.zeros((1, bp, LSTM_HEADS), F32)

    outs = {k: [] for k in ("p_lat", "p_kr", "p_conv", "p_c", "p_n", "p_m",
                            "s_lat", "s_kr", "s_conv", "s_c", "s_n", "s_m")}
    for l in range(depth):
        w = _pack_layer(l, g_mix, w_in, g_q_lat, w_uq, g_q_nope, g_q_rope, g_kv_lat, g_k_rope, w_uk, g_k_nope,
                        w_uv, w_conv, b_conv, b_igate, b_fgate, g_lstm_out, w_out, g_ffn, w_up, w_down)
        q, k, v, lat, kr, qk_raw, v_raw, o_raw, misc = _projection(xp, tabs_p, w, tm=512)
        attn = _attention_prompt(q.reshape(bp, sp, -1), k.reshape(bp, sp, -1), v,
                                 tq=ATTN_TQ, tk=ATTN_TK)
        h, conv_new, c_new, n_new, m_new = _mlstm(qk_raw, v_raw, o_raw, misc, zero_conv, zero_c, zero_n, zero_m,
                                                  0, w, L=MLSTM_TILE)
        xp = _out_ffn(xp, attn.reshape(bp * sp, -1), h, w, tm=512)
        outs["p_lat"].append(lat.reshape(bp, sp, KV_LORA))
        outs["p_kr"].append(kr.reshape(bp, sp, ROPE_DIM))
        outs["p_conv"].append(conv_new)
        outs["p_c"].append(c_new)
        outs["p_n"].append(n_new)
        outs["p_m"].append(m_new)
        q, k, v, lat, kr, qk_raw, v_raw, o_raw, misc = _projection(xs, tabs_s, w, tm=bs * ls)
        attn = _attention_sample(q, k, lat, cache_kv_latent, cache_k_rope, l, w, n_new=ls)
        h, conv_new, c_new, n_new, m_new = _mlstm(qk_raw, v_raw, o_raw, misc, state_conv, state_C,
                                                  state_n, state_m, l, w, L=ls)
        xs = _out_ffn(xs, attn, h, w, tm=bs * ls)
        outs["s_lat"].append(lat.reshape(bs, ls, KV_LORA))
        outs["s_kr"].append(kr.reshape(bs, ls, ROPE_DIM))
        outs["s_conv"].append(conv_new)
        outs["s_c"].append(c_new)
        outs["s_n"].append(n_new)
        outs["s_m"].append(m_new)

    st = lambda key: jnp.stack(outs[key])
    return (xp.reshape(bp, sp, d), xs.reshape(bs, ls, d),
            st("p_lat"), st("p_kr"), st("p_conv"), st("p_c"), st("p_n"), st("p_m"),
            st("s_lat"), st("s_kr"), st("s_conv"), st("s_c"), st("s_n"), st("s_m"))
```

```python
import functools

import jax
import jax.numpy as jnp
from jax import lax
from jax.experimental import pallas as pl
from jax.experimental.pallas import tpu as pltpu

F32 = jnp.float32
BF16 = jnp.bfloat16

EPS = 1e-6
CHUNK = 64
MLA_HEADS = 8
NOPE_DIM = 64
ROPE_DIM = 32
QK_DIM = NOPE_DIM + ROPE_DIM
V_DIM = 64
Q_LORA = 256
KV_LORA = 128
ROPE_BASE = 10000.0
LSTM_HEADS = 4
LSTM_DK = 128
LSTM_DV = 128
CONV_W = 4
LSTM_WIDTH = LSTM_HEADS * LSTM_DV
LSTM_QK = 2 * LSTM_HEADS * LSTM_DK
MLA_WIDTH = MLA_HEADS * V_DIM

LANES = 128
SUBLANES = 8
HEAD_PAIR = 2 * LANES
VMEM_LIMIT = 52 * 1024 * 1024
Q_LOG2_SCALE = (QK_DIM ** -0.5) * 1.4426950408889634
ATTN_TQ = 512
ATTN_TK = 512
PAIRS_PER_TRIP = 4
MLSTM_TILE = 256
MLSTM_SAMPLE_GROUP = 4
VT_ROWS = 80

MISC_KPE = 0
MISC_KPE_ROT = ROPE_DIM
MISC_IG = 2 * ROPE_DIM
MISC_FG = MISC_IG + LSTM_HEADS

COL_CQ = (0, Q_LORA)
COL_CKV = (COL_CQ[1], COL_CQ[1] + KV_LORA)
COL_QK = (COL_CKV[1], COL_CKV[1] + LSTM_QK)
COL_V = (COL_QK[1], COL_QK[1] + LSTM_WIDTH)
COL_O = (COL_V[1], COL_V[1] + LSTM_WIDTH)
COL_MISC = (COL_O[1], COL_O[1] + LANES)
PACKED_COLS = COL_MISC[1]


def _rot_half(a):
    half = ROPE_DIM // 2
    return jnp.concatenate([a[..., half:], a[..., :half]], axis=-1)


def _rms(x, g):
    return x * lax.rsqrt(jnp.mean(x * x, axis=-1, keepdims=True) + EPS) * g


def _in_turn(*stage_generators):
    for _ in zip(*stage_generators):
        pass


def _segment_mean_sq(y, m_ref):
    return jnp.dot((y * y).astype(BF16), m_ref[...], preferred_element_type=F32)


def _proj_kernel(x_ref, rope_ref, g_mix_ref, w_main_ref, g_qlat_ref, w_uq_ref,
                 gq_ref, g_kvlat_ref, gkm_ref, w_uk_ref, gkn_ref, w_uv_ref, vones_ref, mseg_ref, mmisc_ref,
                 q_out, k_out, v_out, lat_out, kr_out, qk_out, vl_out, o_out, misc_out):
    x = x_ref[...]
    xb = _rms(x, g_mix_ref[...]).astype(BF16)

    def proj(col):
        return jnp.dot(xb, w_main_ref[:, col[0]:col[1]], preferred_element_type=F32)

    c_q = proj(COL_CQ)
    c_kv = proj(COL_CKV)
    misc = proj(COL_MISC)
    qk_out[...] = proj(COL_QK)
    misc_out[...] = misc

    tab = rope_ref[...]
    tab_next = pltpu.roll(tab, LANES - ROPE_DIM, 1)
    lane = lax.broadcasted_iota(jnp.int32, tab.shape, 1)
    in_rope = (lane >= NOPE_DIM) & (lane < QK_DIM)
    cq_tab = jnp.where(lane < NOPE_DIM, 1.0, jnp.where(in_rope, tab, 0.0))
    sq_tab = jnp.where(in_rope, tab_next, 0.0)
    ck_tab = jnp.where(lane < ROPE_DIM, tab, 0.0)
    sk_tab = jnp.where(lane < ROPE_DIM, tab_next, 0.0)

    cqn = _rms(c_q, g_qlat_ref[...]).astype(BF16)
    lat = _rms(c_kv, g_kvlat_ref[...])
    lat_out[...] = lat
    latb = lat.astype(BF16)
    lat_t = lat.T.astype(BF16)
    pairs = range(MLA_HEADS // 2)
    cols = [slice(p * HEAD_PAIR, (p + 1) * HEAD_PAIR) for p in pairs]

    q_raw = [jnp.dot(cqn, w_uq_ref[:, cols[p]], preferred_element_type=F32) for p in pairs]
    misc_mean = _segment_mean_sq(misc, mmisc_ref)
    vl_out[...] = proj(COL_V)
    q_mean = [_segment_mean_sq(q_raw[p], mseg_ref) for p in pairs]
    k_raw = [jnp.dot(latb, w_uk_ref[:, cols[p]], preferred_element_type=F32) for p in pairs]
    o_out[...] = proj(COL_O)
    k_mean = [_segment_mean_sq(k_raw[p], mseg_ref) for p in pairs]
    v_out[...] = (jnp.dot(w_uv_ref[...], lat_t, preferred_element_type=F32) + vones_ref[...]).astype(BF16)

    for p in pairs:
        y = q_raw[p] * lax.rsqrt(q_mean[p] + EPS) * gq_ref[:, cols[p]]
        for j in range(2):
            yh = y[:, j * LANES:(j + 1) * LANES]
            qh = yh * cq_tab + pltpu.roll(yh, LANES - ROPE_DIM, 1) * sq_tab
            h = 2 * p + j
            q_out[:, h * LANES:(h + 1) * LANES] = (qh * Q_LOG2_SCALE).astype(BF16)

    ms = misc * lax.rsqrt(misc_mean + EPS) * gkm_ref[...]
    kr = ms * ck_tab + pltpu.roll(ms, LANES - ROPE_DIM, 1) * sk_tab
    kr_out[...] = kr[:, :ROPE_DIM]
    kr_placed = pltpu.roll(kr, NOPE_DIM, 1)
    kr_pair = jnp.concatenate([kr_placed, kr_placed], axis=1)
    for p in pairs:
        y = k_raw[p] * lax.rsqrt(k_mean[p] + EPS) * gkn_ref[:, cols[p]]
        k_out[:, cols[p]] = (y + kr_pair).astype(BF16)


def _full_spec(a):
    nd = a.ndim
    return pl.BlockSpec(a.shape, lambda *_: (0,) * nd)


def _projection(x2, rope_tab, w, tm):
    t = x2.shape[0]
    assert t % tm == 0
    row = lambda c: pl.BlockSpec((tm, c), lambda i: (i, 0))
    weights = [w["g_mix"], w["w_main"], w["g_q_lat"], w["w_uq"], w["gq"], w["g_kv_lat"], w["gkm"],
               w["w_uk"], w["gkn"], w["w_uv_t"], w["vones"], w["mseg"], w["mmisc"]]
    out_shape = (
        jax.ShapeDtypeStruct((t, MLA_HEADS * LANES), BF16),
        jax.ShapeDtypeStruct((t, MLA_HEADS * LANES), BF16),
        jax.ShapeDtypeStruct((MLA_HEADS * VT_ROWS, t), BF16),
        jax.ShapeDtypeStruct((t, KV_LORA), F32),
        jax.ShapeDtypeStruct((t, ROPE_DIM), F32),
        jax.ShapeDtypeStruct((t, LSTM_QK), F32),
        jax.ShapeDtypeStruct((t, LSTM_WIDTH), F32),
        jax.ShapeDtypeStruct((t, LSTM_WIDTH), F32),
        jax.ShapeDtypeStruct((t, LANES), F32),
    )
    return pl.pallas_call(
        _proj_kernel,
        grid=(t // tm,),
        in_specs=[row(x2.shape[1]), row(LANES)] + [_full_spec(a) for a in weights],
        out_specs=tuple(pl.BlockSpec((MLA_HEADS * VT_ROWS, tm), lambda i: (0, i)) if n == 2 else row(s.shape[1])
                        for n, s in enumerate(out_shape)),
        out_shape=out_shape,
        compiler_params=pltpu.CompilerParams(dimension_semantics=("parallel",),
                                             vmem_limit_bytes=VMEM_LIMIT),
        name="projection",
    )(x2, rope_tab, *weights)


def _attn_prompt_kernel(q_ref, qnext_ref, k_ref, vt_ref, o_ref, qt_sc, m_sc, acc_sc, sa_sc, sb_sc, *, tq, tk):
    qi = pl.program_id(2)
    m_sc[...] = jnp.full(m_sc.shape, -1e30, F32)
    acc_sc[...] = jnp.zeros(acc_sc.shape, F32)
    q_col0 = qi * tq
    for j in range(2):
        qt_sc[j] = q_ref[:, j * LANES:(j + 1) * LANES].astype(F32).T.astype(BF16)

    def scores_head(j, k0, s_ref):
        k = k_ref[pl.ds(k0, tk), j * LANES:(j + 1) * LANES]
        s_ref[j] = jnp.dot(k, qt_sc[j], preferred_element_type=F32)

    def scores(k0, s_ref):
        for j in range(2):
            scores_head(j, k0, s_ref)

    def next_tile_scores():
        for j in range(2):
            qt = qnext_ref[:, j * LANES:(j + 1) * LANES].astype(F32).T.astype(BF16)
            sa_sc[j] = jnp.dot(k_ref[pl.ds(0, tk), j * LANES:(j + 1) * LANES], qt, preferred_element_type=F32)

    def consume_head(j, k0, s_ref, masked):
        vt = vt_ref[j * VT_ROWS:(j + 1) * VT_ROWS, pl.ds(k0, tk)]
        s = s_ref[j]
        if masked:
            kc = (k0 + lax.broadcasted_iota(jnp.int32, (tk, tq), 0)) // CHUNK
            qc = (q_col0 + lax.broadcasted_iota(jnp.int32, (tk, tq), 1)) // CHUNK
            s = jnp.where(kc <= qc, s, -1e30)
        m_prev = m_sc[j]
        m_next = jnp.maximum(m_prev, jnp.max(s, axis=0, keepdims=True))
        alpha = jnp.exp2(m_prev - m_next)
        p = jnp.exp2(s - m_next)
        m_sc[j] = m_next
        acc_sc[j] = alpha * acc_sc[j] + jnp.dot(vt, p.astype(BF16), preferred_element_type=F32)

    def consume(k0, s_ref, masked):
        for j in range(2):
            consume_head(j, k0, s_ref, masked)

    n_full = q_col0 // tk
    at = lambda t: pl.multiple_of(t * tk, tk)

    @pl.when(qi == 0)
    def _first_tile_of_pair():
        scores(0, sa_sc)

    def pair(t):
        scores(at(t + 1), sb_sc)
        for j in range(2):
            consume_head(j, at(t), sa_sc, False)
            scores_head(j, at(t + 2), sa_sc)
        consume(at(t + 1), sb_sc, False)

    def body(i, carry):
        for u in range(PAIRS_PER_TRIP):
            pair(2 * PAIRS_PER_TRIP * i + 2 * u)
        return carry

    trips = n_full // (2 * PAIRS_PER_TRIP)
    lax.fori_loop(0, trips, body, 0)

    def rest(i, carry):
        pair(2 * PAIRS_PER_TRIP * trips + 2 * i)
        return carry

    lax.fori_loop(0, (n_full % (2 * PAIRS_PER_TRIP)) // 2, rest, 0)

    t0 = 2 * (n_full // 2)

    @pl.when(n_full % 2 == 1)
    def _odd_tail():
        scores(at(t0 + 1), sb_sc)
        consume(at(t0), sa_sc, False)
        next_tile_scores()
        consume(at(t0 + 1), sb_sc, True)

    @pl.when(n_full % 2 == 0)
    def _even_tail():
        consume(at(t0), sa_sc, True)
        next_tile_scores()

    outs = []
    for j in range(2):
        acc = acc_sc[j]
        outs.append(acc[:V_DIM, :] / acc[V_DIM:V_DIM + 1, :])
    o_ref[...] = jnp.concatenate(outs, axis=0).T.astype(o_ref.dtype)


def _attention_prompt(q, k, vt, tq, tk):
    b, s, _ = q.shape
    assert tq == tk and s % tq == 0 and tq % CHUNK == 0
    pairs = MLA_HEADS // 2
    return pl.pallas_call(
        functools.partial(_attn_prompt_kernel, tq=tq, tk=tk),
        grid=(b, pairs, s // tq),
        in_specs=[
            pl.BlockSpec((None, tq, HEAD_PAIR), lambda bi, p, i: (bi, i, p)),
            pl.BlockSpec((None, tq, HEAD_PAIR), lambda bi, p, i: (bi, jnp.minimum(i + 1, s // tq - 1), p)),
            pl.BlockSpec((None, s, HEAD_PAIR), lambda bi, p, i: (bi, 0, p)),
            pl.BlockSpec((2 * VT_ROWS, s), lambda bi, p, i: (p, bi)),
        ],
        out_specs=pl.BlockSpec((None, tq, LANES), lambda bi, p, i: (bi, i, p)),
        out_shape=jax.ShapeDtypeStruct((b, s, MLA_WIDTH), BF16),
        scratch_shapes=[pltpu.VMEM((2, LANES, tq), BF16),
                        pltpu.VMEM((2, 1, tq), F32),
                        pltpu.VMEM((2, VT_ROWS, tq), F32),
                        pltpu.VMEM((2, tk, tq), F32),
                        pltpu.VMEM((2, tk, tq), F32)],
        compiler_params=pltpu.CompilerParams(
            dimension_semantics=("parallel", "parallel", "arbitrary"),
            vmem_limit_bytes=VMEM_LIMIT),
        name="attention_prompt",
    )(q, q, k, vt)


def _attn_sample_kernel(q_ref, kn_ref, latn_ref, lat_ref, kr_ref, w_uk_ref, gkn_ref, w_uv_ref, mseg_ref,
                        o_ref, k_sc, *, n_new):
    latb = lat_ref[...].astype(BF16)
    kr = kr_ref[...]
    zeros = lambda c: jnp.zeros((kr.shape[0], c), F32)
    kr_placed = jnp.concatenate([zeros(NOPE_DIM), kr, zeros(LANES - QK_DIM)], axis=1)
    kr_pair = jnp.concatenate([kr_placed, kr_placed], axis=1)
    def k_pair(p):
        cols = slice(p * HEAD_PAIR, (p + 1) * HEAD_PAIR)
        raw = jnp.dot(latb, w_uk_ref[:, cols], preferred_element_type=F32)
        yield
        mean_sq = _segment_mean_sq(raw, mseg_ref)
        yield
        y = raw * lax.rsqrt(mean_sq + EPS) * gkn_ref[:, cols]
        k_sc[:, cols] = (y + kr_pair).astype(BF16)
        yield

    _in_turn(*[k_pair(p) for p in range(MLA_HEADS // 2)])
    v_all = jnp.dot(latb, w_uv_ref[...], preferred_element_type=F32).astype(BF16)
    v_new = jnp.dot(latn_ref[...].astype(BF16), w_uv_ref[...], preferred_element_type=F32).astype(BF16)

    q = q_ref[...]
    qt = jnp.concatenate([q] * MLA_HEADS, axis=0)
    r_head = lax.broadcasted_iota(jnp.int32, qt.shape, 0) // n_new
    c_head = lax.broadcasted_iota(jnp.int32, qt.shape, 1) // LANES
    qm = jnp.where(r_head == c_head, qt, jnp.zeros_like(qt))

    nt = (((1,), (1,)), ((), ()))
    s_old = lax.dot_general(k_sc[...], qm, nt, preferred_element_type=F32)
    s_new = lax.dot_general(kn_ref[...], qm, nt, preferred_element_type=F32)
    mx = jnp.maximum(jnp.max(s_old, axis=0, keepdims=True), jnp.max(s_new, axis=0, keepdims=True))
    p_old = jnp.exp2(s_old - mx)
    p_new = jnp.exp2(s_new - mx)
    inv = 1.0 / (jnp.sum(p_old, axis=0, keepdims=True) + jnp.sum(p_new, axis=0, keepdims=True))
    p_old = (p_old * inv).astype(BF16)
    p_new = (p_new * inv).astype(BF16)
    tn = (((0,), (0,)), ((), ()))
    full = (lax.dot_general(p_old, v_all, tn, preferred_element_type=F32)
            + lax.dot_general(p_new, v_new, tn, preferred_element_type=F32))
    v_head = lax.broadcasted_iota(jnp.int32, (n_new, MLA_WIDTH), 1) // V_DIM
    out = jnp.zeros((n_new, MLA_WIDTH), F32)
    for h in range(MLA_HEADS):
        out = out + jnp.where(v_head == h, full[h * n_new:(h + 1) * n_new, :], 0.0)
    o_ref[...] = out.astype(o_ref.dtype)


def _attention_sample(q, k_new, lat_new, cache_lat, cache_kr, layer, w, n_new):
    _, b, past, _ = cache_lat.shape
    weights = [w["w_uk"], w["gkn"], w["w_uv"], w["mseg"]]
    tok = lambda c: pl.BlockSpec((n_new, c), lambda i: (i, 0))
    return pl.pallas_call(
        functools.partial(_attn_sample_kernel, n_new=n_new),
        grid=(b,),
        in_specs=[tok(MLA_HEADS * LANES), tok(MLA_HEADS * LANES), tok(KV_LORA),
                  pl.BlockSpec((None, None, past, KV_LORA), lambda i: (layer, i, 0, 0)),
                  pl.BlockSpec((None, None, past, ROPE_DIM), lambda i: (layer, i, 0, 0))]
                 + [_full_spec(a) for a in weights],
        out_specs=tok(MLA_WIDTH),
        out_shape=jax.ShapeDtypeStruct((b * n_new, MLA_WIDTH), BF16),
        scratch_shapes=[pltpu.VMEM((past, MLA_HEADS * LANES), BF16)],
        compiler_params=pltpu.CompilerParams(dimension_semantics=("parallel",),
                                             vmem_limit_bytes=VMEM_LIMIT),
        name="attention_sample",
    )(q, k_new, lat_new, cache_lat, cache_kr, *weights)


CONV_PAD = SUBLANES


def _mlstm_kernel(qk_ref, v_ref, o_ref, misc_ref, conv0_ref, c0_ref, n0_ref, m0_ref, wconv_ref, bconv_ref,
                  bgate_ref, glstm_ref,
                  h_out, conv_out, c_out, n_out, m_out,
                  full_sc, c_sc, n_sc, m_sc, *, L, G, nc):
    def sequence(g):
        tok = lambda ref: ref.at[g * L:(g + 1) * L]
        return _mlstm_sequence(tok(qk_ref), tok(v_ref), tok(o_ref), tok(misc_ref), conv0_ref.at[g], c0_ref.at[g],
                               n0_ref.at[g], m0_ref.at[g], wconv_ref, bconv_ref, bgate_ref, glstm_ref,
                               tok(h_out), conv_out.at[g], c_out.at[g], n_out.at[g], m_out.at[g],
                               full_sc.at[g], c_sc.at[g], n_sc.at[g], m_sc.at[g], L=L, nc=nc)

    _in_turn(*[sequence(g) for g in range(G)])


def _mlstm_sequence(qk_ref, v_ref, o_ref, misc_ref, conv0_ref, c0_ref, n0_ref, m0_ref, wconv_ref, bconv_ref,
                    bgate_ref, glstm_ref,
                    h_out, conv_out, c_out, n_out, m_out,
                    full_sc, c_sc, n_sc, m_sc, *, L, nc):
    hist = CONV_W - 1
    lo = CONV_PAD - hist
    first_chunk = (lambda f: f()) if nc == 1 else pl.when(pl.program_id(1) == 0)
    last_chunk = (lambda f: f()) if nc == 1 else pl.when(pl.program_id(1) == nc - 1)

    @first_chunk
    def _init():
        full_sc[lo:CONV_PAD, :] = conv0_ref[...]
        c_sc[...] = c0_ref[...]
        n_sc[...] = n0_ref[...]
        m_sc[...] = m0_ref[...]

    full_sc[CONV_PAD:CONV_PAD + L, :] = qk_ref[...]
    ext = full_sc[...]
    y = bconv_ref[...] + ext[CONV_PAD:, :] * wconv_ref[CONV_W - 1:CONV_W, :]
    for r in range(1, CONV_W):
        y = y + pltpu.roll(ext, r, 0)[CONV_PAD:, :] * wconv_ref[CONV_W - 1 - r:CONV_W - r, :]
    qk = y * jax.nn.sigmoid(y)
    tail = full_sc[lo + L:CONV_PAD + L, :]
    full_sc[lo:CONV_PAD, :] = tail
    yield

    LP = max(L, LANES)
    pad_rows = lambda a: a if L == LP else jnp.concatenate([a, jnp.zeros((LP - L, a.shape[1]), a.dtype)], axis=0)
    gs_t = pad_rows(misc_ref[...] + bgate_ref[...]).T
    g8 = gs_t[MISC_IG:MISC_IG + 2 * LSTM_HEADS, :]
    lf8 = jnp.minimum(g8, 0.0) - jnp.log1p(jnp.exp(-jnp.abs(g8)))
    s_idx = lax.broadcasted_iota(jnp.int32, (LP, LP), 0)
    t_idx = lax.broadcasted_iota(jnp.int32, (LP, LP), 1)
    causal = s_idx <= t_idx
    triu = jnp.where(causal, 1.0, 0.0).astype(BF16)
    lf_hi = lf8.astype(BF16)
    lf_mid = (lf8 - lf_hi.astype(F32)).astype(BF16)
    lf_lo = (lf8 - lf_hi.astype(F32) - lf_mid.astype(F32)).astype(BF16)
    b8 = (jnp.dot(lf_hi, triu, preferred_element_type=F32)
          + jnp.dot(lf_mid, triu, preferred_element_type=F32)
          + jnp.dot(lf_lo, triu, preferred_element_type=F32))
    yield
    c4 = g8[:LSTM_HEADS, :] - b8[LSTM_HEADS:, :]
    c_cols = jnp.concatenate([c4, jnp.zeros((LANES - LSTM_HEADS, LP), F32)], axis=0).T
    lane_t = lax.broadcasted_iota(jnp.int32, (1, LP), 1)

    def head(h):
        dk = slice(h * LSTM_DK, (h + 1) * LSTM_DK)
        dv = slice(h * LSTM_DV, (h + 1) * LSTM_DV)
        ig_row = g8[h:h + 1, :]
        b_row = b8[LSTM_HEADS + h:LSTM_HEADS + h + 1, :]
        c_col = c_cols[:, h:h + 1]
        m_prev = m_sc[:, h:h + 1]

        log_d = jnp.where(causal, c_col + b_row, -jnp.inf)
        inter = b_row + m_prev
        m_t = jnp.maximum(inter, jnp.max(log_d, axis=0, keepdims=True))
        decay = jnp.exp(log_d - m_t)
        inter_scale = jnp.exp(inter - m_t)

        qb = pad_rows(qk[:, dk]).astype(BF16)
        kb = pad_rows(qk[:, LSTM_HEADS * LSTM_DK + h * LSTM_DK:LSTM_HEADS * LSTM_DK + (h + 1) * LSTM_DK]
                      * (LSTM_DK ** -0.5)).astype(BF16)
        v_t = pad_rows(v_ref[:, dv]).T
        c_prev = c_sc[h]
        n_prev = n_sc[h:h + 1, :]
        nt = (((1,), (1,)), ((), ()))
        wgt = lax.dot_general(kb, qb, nt, preferred_element_type=F32) * decay
        cq = lax.dot_general(c_prev.astype(BF16), qb, nt, preferred_element_type=F32)
        nq = lax.dot_general(jnp.broadcast_to(n_prev, (SUBLANES, LSTM_DK)).astype(BF16), qb, nt,
                             preferred_element_type=F32)[:1, :]
        yield
        num = jnp.dot(v_t.astype(BF16), wgt.astype(BF16), preferred_element_type=F32) + inter_scale * cq
        den = jnp.sum(wgt, axis=0, keepdims=True) + inter_scale * nq
        hid = num / jnp.maximum(jnp.abs(den), jnp.exp(-m_t))
        yield

        m_new = m_t[:, L - 1:L]
        b_last = b_row[:, L - 1:L]
        carry = jnp.exp(b_last + m_prev - m_new)
        d_end = jnp.exp(b_last - b_row + ig_row - m_new)
        if L < LP:
            d_end = jnp.where(lane_t < L, d_end, 0.0)
        upd = jnp.dot((v_t * d_end).astype(BF16), kb, preferred_element_type=F32)
        nk = jnp.dot(jnp.broadcast_to(d_end, (SUBLANES, LP)).astype(BF16), kb,
                     preferred_element_type=F32)[:1, :]
        c_sc[h] = carry * c_prev + upd
        n_sc[h:h + 1, :] = carry * n_prev + nk
        m_sc[:, h:h + 1] = m_new

        g_col = jnp.concatenate([glstm_ref[dv, :]] * (LP // LANES), axis=1)
        hn = hid * lax.rsqrt(jnp.mean(hid * hid, axis=0, keepdims=True) + EPS) * g_col
        h_out[:, dv] = (hn.T[:L, :] * jax.nn.sigmoid(o_ref[:, dv])).astype(h_out.dtype)
        yield

    for _ in zip(*[head(h) for h in range(LSTM_HEADS)]):
        yield

    @last_chunk
    def _finish():
        conv_out[...] = tail
        c_out[...] = c_sc[...]
        n_out[...] = n_sc[...]
        m_out[...] = m_sc[...]

    yield


def _mlstm(qk_raw, v_raw, o_raw, misc, conv0, c0, n0, m0, layer, w, L, G):
    b = conv0.shape[1]
    t = qk_raw.shape[0]
    nc = t // (b * L)
    assert nc * b * L == t and b % G == 0 and (G == 1 or nc == 1)
    hist = CONV_W - 1
    tok = lambda cdim: pl.BlockSpec((G * L, cdim), lambda bi, ci: (bi * nc + ci, 0))
    m0 = m0.reshape(m0.shape[0], b, 1, LSTM_HEADS)
    state0 = lambda *dims: pl.BlockSpec((None, G) + dims, lambda bi, ci: (layer, bi) + (0,) * len(dims))
    weights = [w["w_conv"], w["b_conv"], w["bgate"], w["g_lstm_out"]]
    out_shape = (
        jax.ShapeDtypeStruct((t, LSTM_WIDTH), BF16),
        jax.ShapeDtypeStruct((b, hist, LSTM_QK), F32),
        jax.ShapeDtypeStruct((b, LSTM_HEADS, LSTM_DV, LSTM_DK), F32),
        jax.ShapeDtypeStruct((b, LSTM_HEADS, LSTM_DK), F32),
        jax.ShapeDtypeStruct((b, 1, LSTM_HEADS), F32),
    )
    state = lambda *dims: pl.BlockSpec((G,) + dims, lambda bi, ci: (bi,) + (0,) * len(dims))
    h, conv_new, c_new, n_new, m_new = pl.pallas_call(
        functools.partial(_mlstm_kernel, L=L, G=G, nc=nc),
        grid=(b // G, nc),
        in_specs=[tok(LSTM_QK), tok(LSTM_WIDTH), tok(LSTM_WIDTH), tok(LANES),
                  state0(hist, LSTM_QK), state0(LSTM_HEADS, LSTM_DV, LSTM_DK), state0(LSTM_HEADS, LSTM_DK),
                  state0(1, LSTM_HEADS)] + [_full_spec(a) for a in weights],
        out_specs=(tok(LSTM_WIDTH), state(hist, LSTM_QK), state(LSTM_HEADS, LSTM_DV, LSTM_DK),
                   state(LSTM_HEADS, LSTM_DK), state(1, LSTM_HEADS)),
        out_shape=out_shape,
        scratch_shapes=[pltpu.VMEM((G, CONV_PAD + L, LSTM_QK), F32),
                        pltpu.VMEM((G, LSTM_HEADS, LSTM_DV, LSTM_DK), F32),
                        pltpu.VMEM((G, LSTM_HEADS, LSTM_DK), F32),
                        pltpu.VMEM((G, 1, LSTM_HEADS), F32)],
        compiler_params=pltpu.CompilerParams(dimension_semantics=("parallel", "arbitrary"),
                                             vmem_limit_bytes=VMEM_LIMIT),
        name="mlstm",
    )(qk_raw, v_raw, o_raw, misc, conv0, c0, n0, m0, *weights)
    return h, conv_new, c_new, n_new, m_new.reshape(b, LSTM_HEADS)


FF_CHUNK = 1024


def _out_ffn_kernel(x_ref, attn_ref, lstm_ref, w_out_ref, g_ffn_ref, w_up_ref, w_down_ref, y_ref):
    mix = jnp.concatenate([attn_ref[...], lstm_ref[...]], axis=1)
    x1 = x_ref[...] + jnp.dot(mix, w_out_ref[...], preferred_element_type=F32)
    xb = _rms(x1, g_ffn_ref[...]).astype(BF16)
    d_ff = w_up_ref.shape[1]
    acc = x1
    for f in range(d_ff // FF_CHUNK):
        cols = slice(f * FF_CHUNK, (f + 1) * FF_CHUNK)
        u = jnp.maximum(jnp.dot(xb, w_up_ref[:, cols], preferred_element_type=F32), 0.0)
        acc = acc + jnp.dot((u * u).astype(BF16), w_down_ref[cols, :], preferred_element_type=F32)
    y_ref[...] = acc


def _out_ffn(x2, attn, lstm, w, tm):
    t, d = x2.shape
    assert t % tm == 0
    row = lambda c: pl.BlockSpec((tm, c), lambda i: (i, 0))
    const = lambda a: pl.BlockSpec(a.shape, lambda i: (0, 0), pipeline_mode=pl.Buffered(1))
    weights = [w["w_out"], w["g_ffn"], w["w_up"], w["w_down"]]
    return pl.pallas_call(
        _out_ffn_kernel,
        grid=(t // tm,),
        in_specs=[row(d), row(MLA_WIDTH), row(LSTM_WIDTH)] + [const(a) for a in weights],
        out_specs=row(d),
        out_shape=jax.ShapeDtypeStruct((t, d), F32),
        compiler_params=pltpu.CompilerParams(dimension_semantics=("parallel",),
                                             vmem_limit_bytes=VMEM_LIMIT),
        name="out_ffn",
    )(x2, attn, lstm, *weights)


def _pack_layer(l, g_mix, w_in, g_q_lat, w_uq, g_q_nope, g_q_rope, g_kv_lat, g_k_rope, w_uk, g_k_nope, w_uv,
                w_conv, b_conv, b_igate, b_fgate, g_lstm_out, w_out, g_ffn, w_up, w_down):
    d = w_in.shape[1]
    wi = w_in[l]
    o = 0
    parts = {}
    for name, width in (("cq", Q_LORA), ("ckv", KV_LORA), ("kpe", ROPE_DIM), ("qk", LSTM_QK), ("v", LSTM_WIDTH),
                        ("o", LSTM_WIDTH), ("ig", LSTM_HEADS), ("fg", LSTM_HEADS)):
        parts[name] = wi[:, o:o + width]
        o += width
    misc = jnp.concatenate([parts["kpe"], _rot_half(parts["kpe"]), parts["ig"], parts["fg"],
                            jnp.zeros((d, LANES - MISC_FG - LSTM_HEADS), F32)], axis=1)
    w_main = jnp.concatenate([parts["cq"], parts["ckv"], parts["qk"], parts["v"], parts["o"], misc],
                             axis=1).astype(BF16)

    uq = w_uq[l].reshape(Q_LORA, MLA_HEADS, QK_DIM)
    uq = jnp.concatenate([uq, _rot_half(uq[..., NOPE_DIM:])], axis=-1).reshape(Q_LORA, MLA_HEADS * LANES)
    gq_head = jnp.concatenate([g_q_nope[l], g_q_rope[l], _rot_half(g_q_rope[l])])
    uk = w_uk[l].reshape(KV_LORA, MLA_HEADS, NOPE_DIM)
    uk = jnp.concatenate([uk, jnp.zeros_like(uk)], axis=-1).reshape(KV_LORA, MLA_HEADS * LANES)
    gkn_head = jnp.concatenate([g_k_nope[l], jnp.zeros((LANES - NOPE_DIM,), F32)])
    uv = w_uv[l].reshape(KV_LORA, MLA_HEADS, V_DIM)
    uv_t = jnp.concatenate([uv, jnp.zeros((KV_LORA, MLA_HEADS, VT_ROWS - V_DIM), F32)], axis=-1)
    uv_t = uv_t.reshape(KV_LORA, MLA_HEADS * VT_ROWS).T
    vones_head = jnp.zeros((VT_ROWS,), F32).at[V_DIM].set(1.0)
    gkm = jnp.concatenate([g_k_rope[l], _rot_half(g_k_rope[l]), jnp.zeros((LANES - 2 * ROPE_DIM,), F32)])
    bgate = jnp.concatenate([jnp.zeros((MISC_IG,), F32), b_igate[l], b_fgate[l],
                             jnp.zeros((LANES - MISC_FG - LSTM_HEADS,), F32)])

    i = jnp.arange(HEAD_PAIR)
    same = (i[:, None] // LANES) == (i[None, :] // LANES)
    li, lj = i[:, None] % LANES, i[None, :] % LANES
    mseg = jnp.where(same & (li < NOPE_DIM) & (lj < NOPE_DIM), 1.0 / NOPE_DIM,
                     jnp.where(same & (li >= NOPE_DIM) & (li < QK_DIM) & (lj >= NOPE_DIM), 1.0 / ROPE_DIM, 0.0))
    a = jnp.arange(LANES)
    mmisc = jnp.where((a[:, None] < ROPE_DIM) & (a[None, :] < 2 * ROPE_DIM), 1.0 / ROPE_DIM, 0.0)

    row = lambda v: v.reshape(1, -1).astype(F32)
    return {
        "g_mix": row(g_mix[l]), "w_main": w_main, "g_q_lat": row(g_q_lat[l]), "w_uq": uq.astype(BF16),
        "gq": row(jnp.tile(gq_head, MLA_HEADS)), "g_kv_lat": row(g_kv_lat[l]), "gkm": row(gkm),
        "w_uk": uk.astype(BF16), "gkn": row(jnp.tile(gkn_head, MLA_HEADS)), "w_uv": w_uv[l].astype(BF16),
        "w_uv_t": uv_t.astype(BF16), "vones": jnp.tile(vones_head, MLA_HEADS).reshape(-1, 1),
        "mseg": mseg.astype(BF16), "mmisc": mmisc.astype(BF16),
        "w_conv": w_conv[l], "b_conv": row(b_conv[l]), "bgate": row(bgate), "g_lstm_out": jnp.broadcast_to(g_lstm_out[l][:, None], (LSTM_WIDTH, LANES)),
        "w_out": w_out[l].astype(BF16), "g_ffn": row(g_ffn[l]), "w_up": w_up[l].astype(BF16),
        "w_down": w_down[l].astype(BF16),
    }


def _rope_table(first_pos, n):
    half = ROPE_DIM // 2
    blk = min(n, LANES)
    assert n % blk == 0
    lane = jnp.arange(LANES, dtype=jnp.int32)
    inv = ROPE_BASE ** (-(lane % half).astype(F32) / half)
    ang_a = (first_pos + blk * jnp.arange(n // blk, dtype=jnp.int32)).astype(F32)[:, None] * inv[None, :]
    ang_b = jnp.arange(blk, dtype=jnp.int32).astype(F32)[:, None] * inv[None, :]
    ca, sa = jnp.cos(ang_a)[:, None, :], jnp.sin(ang_a)[:, None, :]
    cb, sb = jnp.cos(ang_b)[None, :, :], jnp.sin(ang_b)[None, :, :]
    cos = ca * cb - sa * sb
    sin = sa * cb + ca * sb
    kind = (lane // half) % 4
    tab = jnp.where(kind < 2, cos, jnp.where(kind == 2, -sin, sin))
    return tab.reshape(n, LANES)


def kernel(x_prompt, x_sample, cache_kv_latent, cache_k_rope, state_conv, state_C, state_n, state_m,
           g_mix, w_in, g_q_lat, w_uq, g_q_nope, g_q_rope, g_kv_lat, g_k_rope, w_uk, g_k_nope, w_uv,
           w_conv, b_conv, b_igate, b_fgate, g_lstm_out, w_out, g_ffn, w_up, w_down):
    depth = w_in.shape[0]
    bp, sp, d = x_prompt.shape
    bs, ls, _ = x_sample.shape
    past = cache_kv_latent.shape[2]
    hist = CONV_W - 1

    tabs_p = jnp.tile(_rope_table(0, sp), (bp, 1))
    tabs_s = jnp.tile(_rope_table(past, ls), (bs, 1))
    xp = x_prompt.reshape(bp * sp, d)
    xs = x_sample.reshape(bs * ls, d)
    zero_conv = jnp.zeros((1, bp, hist, LSTM_QK), F32)
    zero_c = jnp.zeros((1, bp, LSTM_HEADS, LSTM_DV, LSTM_DK), F32)
    zero_n = jnp.zeros((1, bp, LSTM_HEADS, LSTM_DK), F32)
    zero_m = jnp.zeros((1, bp, LSTM_HEADS), F32)

    outs = {k: [] for k in ("p_lat", "p_kr", "p_conv", "p_c", "p_n", "p_m",
                            "s_lat", "s_kr", "s_conv", "s_c", "s_n", "s_m")}
    for l in range(depth):
        w = _pack_layer(l, g_mix, w_in, g_q_lat, w_uq, g_q_nope, g_q_rope, g_kv_lat, g_k_rope, w_uk, g_k_nope,
                        w_uv, w_conv, b_conv, b_igate, b_fgate, g_lstm_out, w_out, g_ffn, w_up, w_down)
        q, k, v, lat, kr, qk_raw, v_raw, o_raw, misc = _projection(xp, tabs_p, w, tm=512)
        attn = _attention_prompt(q.reshape(bp, sp, -1), k.reshape(bp, sp, -1), v,
                                 tq=ATTN_TQ, tk=ATTN_TK)
        h, conv_new, c_new, n_new, m_new = _mlstm(qk_raw, v_raw, o_raw, misc, zero_conv, zero_c, zero_n, zero_m,
                                                  0, w, L=MLSTM_TILE, G=1)
        xp = _out_ffn(xp, attn.reshape(bp * sp, -1), h, w, tm=512)
        outs["p_lat"].append(lat.reshape(bp, sp, KV_LORA))
        outs["p_kr"].append(kr.reshape(bp, sp, ROPE_DIM))
        outs["p_conv"].append(conv_new)
        outs["p_c"].append(c_new)
        outs["p_n"].append(n_new)
        outs["p_m"].append(m_new)
        q, k, v, lat, kr, qk_raw, v_raw, o_raw, misc = _projection(xs, tabs_s, w, tm=bs * ls)
        attn = _attention_sample(q, k, lat, cache_kv_latent, cache_k_rope, l, w, n_new=ls)
        h, conv_new, c_new, n_new, m_new = _mlstm(qk_raw, v_raw, o_raw, misc, state_conv, state_C,
                                                  state_n, state_m, l, w, L=ls, G=MLSTM_SAMPLE_GROUP)
        xs = _out_ffn(xs, attn, h, w, tm=bs * ls)
        outs["s_lat"].append(lat.reshape(bs, ls, KV_LORA))
        outs["s_kr"].append(kr.reshape(bs, ls, ROPE_DIM))
        outs["s_conv"].append(conv_new)
        outs["s_c"].append(c_new)
        outs["s_n"].append(n_new)
        outs["s_m"].append(m_new)

    st = lambda key: jnp.stack(outs[key])
    return (xp.reshape(bp, sp, d), xs.reshape(bs, ls, d),
            st("p_lat"), st("p_kr"), st("p_conv"), st("p_c"), st("p_n"), st("p_m"),
            st("s_lat"), st("s_kr"), st("s_conv"), st("s_c"), st("s_n"), st("s_m"))
```

```python
import functools

import jax
import jax.numpy as jnp
from jax import lax
from jax.experimental import pallas as pl
from jax.experimental.pallas import tpu as pltpu

F32 = jnp.float32
BF16 = jnp.bfloat16

EPS = 1e-6
CHUNK = 64
MLA_HEADS = 8
NOPE_DIM = 64
ROPE_DIM = 32
QK_DIM = NOPE_DIM + ROPE_DIM
V_DIM = 64
Q_LORA = 256
KV_LORA = 128
ROPE_BASE = 10000.0
LSTM_HEADS = 4
LSTM_DK = 128
LSTM_DV = 128
CONV_W = 4
LSTM_WIDTH = LSTM_HEADS * LSTM_DV
LSTM_QK = 2 * LSTM_HEADS * LSTM_DK
MLA_WIDTH = MLA_HEADS * V_DIM

LANES = 128
SUBLANES = 8
HEAD_PAIR = 2 * LANES
VMEM_LIMIT = 52 * 1024 * 1024
Q_LOG2_SCALE = (QK_DIM ** -0.5) * 1.4426950408889634
ATTN_TQ = 512
ATTN_TK = 512
PAIRS_PER_TRIP = 4
MLSTM_TILE = 256
ATTN_SAMPLE_GROUP = 4
MLSTM_SAMPLE_GROUP = 4
VT_ROWS = 80

MISC_KPE = 0
MISC_KPE_ROT = ROPE_DIM
MISC_IG = 2 * ROPE_DIM
MISC_FG = MISC_IG + LSTM_HEADS

COL_CQ = ("lat", 0, Q_LORA)
COL_CKV = ("lat", Q_LORA, Q_LORA + KV_LORA)
COL_QK = ("lstm", 0, LSTM_QK)
COL_V = ("lstm", LSTM_QK, LSTM_QK + LSTM_WIDTH)
COL_O = ("lstm", LSTM_QK + LSTM_WIDTH, LSTM_QK + 2 * LSTM_WIDTH)
COL_MISC = ("misc", 0, LANES)
W_IN_LAT = (0, Q_LORA + KV_LORA)
W_IN_LSTM = (Q_LORA + KV_LORA + ROPE_DIM, Q_LORA + KV_LORA + ROPE_DIM + LSTM_QK + 2 * LSTM_WIDTH)


def _rot_half(a):
    half = ROPE_DIM // 2
    return jnp.concatenate([a[..., half:], a[..., :half]], axis=-1)


def _rms(x, g):
    return x * lax.rsqrt(jnp.mean(x * x, axis=-1, keepdims=True) + EPS) * g


def _in_turn(*stage_generators):
    for _ in zip(*stage_generators):
        pass


def _segment_mean_sq(y, m_ref):
    return jnp.dot((y * y).astype(BF16), m_ref[...], preferred_element_type=F32)


def _proj_kernel(x_ref, rope_ref, g_mix_ref, w_lat_ref, w_lstm_ref, w_misc_ref, g_qlat_ref, w_uq_ref,
                 gq_ref, g_kvlat_ref, gkm_ref, w_uk_ref, gkn_ref, w_uv_ref, vones_ref, mseg_ref, mmisc_ref,
                 q_out, k_out, v_out, lat_out, kr_out, qk_out, vl_out, o_out, misc_out):
    x = x_ref[...]
    xb = _rms(x, g_mix_ref[...]).astype(BF16)

    blocks = {"lat": w_lat_ref, "lstm": w_lstm_ref, "misc": w_misc_ref}

    def proj(col):
        return jnp.dot(xb, blocks[col[0]][:, col[1]:col[2]], preferred_element_type=F32)

    c_q = proj(COL_CQ)
    c_kv = proj(COL_CKV)
    misc = proj(COL_MISC)
    qk_out[...] = proj(COL_QK)
    misc_out[...] = misc

    tab = rope_ref[...]
    tab_next = pltpu.roll(tab, LANES - ROPE_DIM, 1)
    lane = lax.broadcasted_iota(jnp.int32, tab.shape, 1)
    in_rope = (lane >= NOPE_DIM) & (lane < QK_DIM)
    cq_tab = jnp.where(lane < NOPE_DIM, 1.0, jnp.where(in_rope, tab, 0.0))
    sq_tab = jnp.where(in_rope, tab_next, 0.0)
    ck_tab = jnp.where(lane < ROPE_DIM, tab, 0.0)
    sk_tab = jnp.where(lane < ROPE_DIM, tab_next, 0.0)

    cqn = _rms(c_q, g_qlat_ref[...]).astype(BF16)
    lat = _rms(c_kv, g_kvlat_ref[...])
    lat_out[...] = lat
    latb = lat.astype(BF16)
    lat_t = lat.T.astype(BF16)
    pairs = range(MLA_HEADS // 2)
    cols = [slice(p * HEAD_PAIR, (p + 1) * HEAD_PAIR) for p in pairs]

    q_raw = [jnp.dot(cqn, w_uq_ref[:, cols[p]], preferred_element_type=F32) for p in pairs]
    misc_mean = _segment_mean_sq(misc, mmisc_ref)
    vl_out[...] = proj(COL_V)
    q_mean = [_segment_mean_sq(q_raw[p], mseg_ref) for p in pairs]
    k_raw = [jnp.dot(latb, w_uk_ref[:, cols[p]], preferred_element_type=F32) for p in pairs]
    o_out[...] = proj(COL_O)
    k_mean = [_segment_mean_sq(k_raw[p], mseg_ref) for p in pairs]
    v_out[...] = (jnp.dot(w_uv_ref[...], lat_t, preferred_element_type=F32) + vones_ref[...]).astype(BF16)

    for p in pairs:
        y = q_raw[p] * lax.rsqrt(q_mean[p] + EPS) * gq_ref[:, cols[p]]
        for j in range(2):
            yh = y[:, j * LANES:(j + 1) * LANES]
            qh = yh * cq_tab + pltpu.roll(yh, LANES - ROPE_DIM, 1) * sq_tab
            h = 2 * p + j
            q_out[:, h * LANES:(h + 1) * LANES] = (qh * Q_LOG2_SCALE).astype(BF16)

    ms = misc * lax.rsqrt(misc_mean + EPS) * gkm_ref[...]
    kr = ms * ck_tab + pltpu.roll(ms, LANES - ROPE_DIM, 1) * sk_tab
    kr_out[...] = kr.T[:ROPE_DIM, :]
    kr_placed = pltpu.roll(kr, NOPE_DIM, 1)
    kr_pair = jnp.concatenate([kr_placed, kr_placed], axis=1)
    for p in pairs:
        y = k_raw[p] * lax.rsqrt(k_mean[p] + EPS) * gkn_ref[:, cols[p]]
        k_out[:, cols[p]] = (y + kr_pair).astype(BF16)


def _full_spec(a):
    nd = a.ndim
    return pl.BlockSpec(a.shape, lambda *_: (0,) * nd)


def _projection(x2, rope_tab, w, tm):
    t = x2.shape[0]
    assert t % tm == 0
    row = lambda c: pl.BlockSpec((tm, c), lambda i: (i, 0))
    weights = [w["g_mix"], w["w_lat"], w["w_lstm"], w["w_misc"], w["g_q_lat"], w["w_uq"], w["gq"], w["g_kv_lat"],
               w["gkm"], w["w_uk"], w["gkn"], w["w_uv_t"], w["vones"], w["mseg"], w["mmisc"]]
    out_shape = (
        jax.ShapeDtypeStruct((t, MLA_HEADS * LANES), BF16),
        jax.ShapeDtypeStruct((t, MLA_HEADS * LANES), BF16),
        jax.ShapeDtypeStruct((MLA_HEADS * VT_ROWS, t), BF16),
        jax.ShapeDtypeStruct((t, KV_LORA), F32),
        jax.ShapeDtypeStruct((ROPE_DIM, t), F32),
        jax.ShapeDtypeStruct((t, LSTM_QK), F32),
        jax.ShapeDtypeStruct((t, LSTM_WIDTH), F32),
        jax.ShapeDtypeStruct((t, LSTM_WIDTH), F32),
        jax.ShapeDtypeStruct((t, LANES), F32),
    )
    return pl.pallas_call(
        _proj_kernel,
        grid=(t // tm,),
        in_specs=[row(x2.shape[1]), row(LANES)] + [_full_spec(a) for a in weights],
        out_specs=tuple(pl.BlockSpec((s.shape[0], tm), lambda i: (0, i)) if n in (2, 4) else row(s.shape[1])
                        for n, s in enumerate(out_shape)),
        out_shape=out_shape,
        compiler_params=pltpu.CompilerParams(dimension_semantics=("parallel",),
                                             vmem_limit_bytes=VMEM_LIMIT),
        name="projection",
    )(x2, rope_tab, *weights)


def _attn_prompt_kernel(q_ref, qnext_ref, k_ref, vt_ref, o_ref, qt_sc, m_sc, acc_sc, sa_sc, sb_sc, *, tq, tk):
    qi = pl.program_id(2)
    m_sc[...] = jnp.full(m_sc.shape, -1e30, F32)
    acc_sc[...] = jnp.zeros(acc_sc.shape, F32)
    q_col0 = qi * tq
    for j in range(2):
        qt_sc[j] = q_ref[:, j * LANES:(j + 1) * LANES].astype(F32).T.astype(BF16)

    def scores_head(j, k0, s_ref):
        k = k_ref[pl.ds(k0, tk), j * LANES:(j + 1) * LANES]
        s_ref[j] = jnp.dot(k, qt_sc[j], preferred_element_type=F32)

    def scores(k0, s_ref):
        for j in range(2):
            scores_head(j, k0, s_ref)

    def next_tile_scores():
        for j in range(2):
            qt = qnext_ref[:, j * LANES:(j + 1) * LANES].astype(F32).T.astype(BF16)
            sa_sc[j] = jnp.dot(k_ref[pl.ds(0, tk), j * LANES:(j + 1) * LANES], qt, preferred_element_type=F32)

    def consume_head(j, k0, s_ref, masked):
        vt = vt_ref[j * VT_ROWS:(j + 1) * VT_ROWS, pl.ds(k0, tk)]
        s = s_ref[j]
        if masked:
            kc = (k0 + lax.broadcasted_iota(jnp.int32, (tk, tq), 0)) // CHUNK
            qc = (q_col0 + lax.broadcasted_iota(jnp.int32, (tk, tq), 1)) // CHUNK
            s = jnp.where(kc <= qc, s, -1e30)
        m_prev = m_sc[j]
        m_next = jnp.maximum(m_prev, jnp.max(s, axis=0, keepdims=True))
        alpha = jnp.exp2(m_prev - m_next)
        p = jnp.exp2(s - m_next)
        m_sc[j] = m_next
        acc_sc[j] = alpha * acc_sc[j] + jnp.dot(vt, p.astype(BF16), preferred_element_type=F32)

    def consume(k0, s_ref, masked):
        for j in range(2):
            consume_head(j, k0, s_ref, masked)

    n_full = q_col0 // tk
    at = lambda t: pl.multiple_of(t * tk, tk)

    @pl.when(qi == 0)
    def _first_tile_of_pair():
        scores(0, sa_sc)

    def pair(t):
        scores(at(t + 1), sb_sc)
        for j in range(2):
            consume_head(j, at(t), sa_sc, False)
            scores_head(j, at(t + 2), sa_sc)
        consume(at(t + 1), sb_sc, False)

    def body(i, carry):
        for u in range(PAIRS_PER_TRIP):
            pair(2 * PAIRS_PER_TRIP * i + 2 * u)
        return carry

    trips = n_full // (2 * PAIRS_PER_TRIP)
    lax.fori_loop(0, trips, body, 0)

    def rest(i, carry):
        pair(2 * PAIRS_PER_TRIP * trips + 2 * i)
        return carry

    lax.fori_loop(0, (n_full % (2 * PAIRS_PER_TRIP)) // 2, rest, 0)

    t0 = 2 * (n_full // 2)

    @pl.when(n_full % 2 == 1)
    def _odd_tail():
        scores(at(t0 + 1), sb_sc)
        consume(at(t0), sa_sc, False)
        next_tile_scores()
        consume(at(t0 + 1), sb_sc, True)

    @pl.when(n_full % 2 == 0)
    def _even_tail():
        consume(at(t0), sa_sc, True)
        next_tile_scores()

    outs = []
    for j in range(2):
        acc = acc_sc[j]
        outs.append(acc[:V_DIM, :] / acc[V_DIM:V_DIM + 1, :])
    o_ref[...] = jnp.concatenate(outs, axis=0).T.astype(o_ref.dtype)


def _attention_prompt(q, k, vt, tq, tk):
    b, s, _ = q.shape
    assert tq == tk and s % tq == 0 and tq % CHUNK == 0
    pairs = MLA_HEADS // 2
    return pl.pallas_call(
        functools.partial(_attn_prompt_kernel, tq=tq, tk=tk),
        grid=(b, pairs, s // tq),
        in_specs=[
            pl.BlockSpec((None, tq, HEAD_PAIR), lambda bi, p, i: (bi, i, p)),
            pl.BlockSpec((None, tq, HEAD_PAIR), lambda bi, p, i: (bi, jnp.minimum(i + 1, s // tq - 1), p)),
            pl.BlockSpec((None, s, HEAD_PAIR), lambda bi, p, i: (bi, 0, p)),
            pl.BlockSpec((2 * VT_ROWS, s), lambda bi, p, i: (p, bi)),
        ],
        out_specs=pl.BlockSpec((None, tq, LANES), lambda bi, p, i: (bi, i, p)),
        out_shape=jax.ShapeDtypeStruct((b, s, MLA_WIDTH), BF16),
        scratch_shapes=[pltpu.VMEM((2, LANES, tq), BF16),
                        pltpu.VMEM((2, 1, tq), F32),
                        pltpu.VMEM((2, VT_ROWS, tq), F32),
                        pltpu.VMEM((2, tk, tq), F32),
                        pltpu.VMEM((2, tk, tq), F32)],
        compiler_params=pltpu.CompilerParams(
            dimension_semantics=("parallel", "parallel", "arbitrary"),
            vmem_limit_bytes=VMEM_LIMIT),
        name="attention_prompt",
    )(q, q, k, vt)


def _attn_sample_kernel(q_ref, kn_ref, latn_ref, lat_ref, kr_ref, w_uk_ref, gkn_ref, w_uv_ref, mseg_ref,
                        o_ref, k_sc, *, n_new, G):
    def sequence(g):
        tok = lambda ref: ref.at[g * n_new:(g + 1) * n_new]
        return _attn_sample_sequence(tok(q_ref), tok(kn_ref), tok(latn_ref), lat_ref.at[g], kr_ref.at[g],
                                     w_uk_ref, gkn_ref, w_uv_ref, mseg_ref, tok(o_ref), k_sc.at[g], n_new=n_new)

    _in_turn(*[sequence(g) for g in range(G)])


def _attn_sample_sequence(q_ref, kn_ref, latn_ref, lat_ref, kr_ref, w_uk_ref, gkn_ref, w_uv_ref, mseg_ref,
                          o_ref, k_sc, *, n_new):
    latb = lat_ref[...].astype(BF16)
    kr_t = kr_ref[...]
    zeros = lambda r: jnp.zeros((r, kr_t.shape[1]), F32)
    kr_placed = jnp.concatenate([zeros(NOPE_DIM), kr_t, zeros(LANES - QK_DIM)], axis=0).T
    kr_pair = jnp.concatenate([kr_placed, kr_placed], axis=1)
    pairs = range(MLA_HEADS // 2)
    cols = [slice(p * HEAD_PAIR, (p + 1) * HEAD_PAIR) for p in pairs]
    raw = [jnp.dot(latb, w_uk_ref[:, cols[p]], preferred_element_type=F32) for p in pairs]
    yield
    mean_sq = [_segment_mean_sq(raw[p], mseg_ref) for p in pairs]
    yield
    for p in pairs:
        y = raw[p] * lax.rsqrt(mean_sq[p] + EPS) * gkn_ref[:, cols[p]]
        k_sc[:, cols[p]] = (y + kr_pair).astype(BF16)

    q = q_ref[...]
    qt = jnp.concatenate([q] * MLA_HEADS, axis=0)
    r_head = lax.broadcasted_iota(jnp.int32, qt.shape, 0) // n_new
    c_head = lax.broadcasted_iota(jnp.int32, qt.shape, 1) // LANES
    qm = jnp.where(r_head == c_head, qt, jnp.zeros_like(qt))

    nt = (((1,), (1,)), ((), ()))
    s_old = lax.dot_general(k_sc[...], qm, nt, preferred_element_type=F32)
    s_new = lax.dot_general(kn_ref[...], qm, nt, preferred_element_type=F32)
    yield
    mx = jnp.maximum(jnp.max(s_old, axis=0, keepdims=True), jnp.max(s_new, axis=0, keepdims=True))
    p_old = jnp.exp2(s_old - mx)
    p_new = jnp.exp2(s_new - mx)
    inv = 1.0 / (jnp.sum(p_old, axis=0, keepdims=True) + jnp.sum(p_new, axis=0, keepdims=True))
    p_old = (p_old * inv).astype(BF16)
    p_new = (p_new * inv).astype(BF16)
    tn = (((0,), (0,)), ((), ()))
    ctx = (lax.dot_general(p_old, latb, tn, preferred_element_type=F32)
           + lax.dot_general(p_new, latn_ref[...].astype(BF16), tn, preferred_element_type=F32))
    yield
    full = jnp.dot(ctx.astype(BF16), w_uv_ref[...], preferred_element_type=F32)
    v_head = lax.broadcasted_iota(jnp.int32, (n_new, MLA_WIDTH), 1) // V_DIM
    out = jnp.zeros((n_new, MLA_WIDTH), F32)
    for h in range(MLA_HEADS):
        out = out + jnp.where(v_head == h, full[h * n_new:(h + 1) * n_new, :], 0.0)
    o_ref[...] = out.astype(o_ref.dtype)
    yield


def _attention_sample(q, k_new, lat_new, cache_lat, cache_kr, layer, w, n_new, G):
    _, b, past, _ = cache_lat.shape
    assert b % G == 0
    weights = [w["w_uk"], w["gkn"], w["w_uv"], w["mseg"]]
    tok = lambda c: pl.BlockSpec((G * n_new, c), lambda i: (i, 0))
    return pl.pallas_call(
        functools.partial(_attn_sample_kernel, n_new=n_new, G=G),
        grid=(b // G,),
        in_specs=[tok(MLA_HEADS * LANES), tok(MLA_HEADS * LANES), tok(KV_LORA),
                  pl.BlockSpec((None, G, past, KV_LORA), lambda i: (layer, i, 0, 0)),
                  pl.BlockSpec((None, G, ROPE_DIM, past), lambda i: (layer, i, 0, 0))]
                 + [_full_spec(a) for a in weights],
        out_specs=tok(MLA_WIDTH),
        out_shape=jax.ShapeDtypeStruct((b * n_new, MLA_WIDTH), BF16),
        scratch_shapes=[pltpu.VMEM((G, past, MLA_HEADS * LANES), BF16)],
        compiler_params=pltpu.CompilerParams(dimension_semantics=("parallel",),
                                             vmem_limit_bytes=VMEM_LIMIT),
        name="attention_sample",
    )(q, k_new, lat_new, cache_lat, cache_kr, *weights)


CONV_PAD = SUBLANES


def _mlstm_kernel(qk_ref, v_ref, o_ref, misc_ref, conv0_ref, c0_ref, n0_ref, m0_ref, wconv_ref, bconv_ref,
                  bgate_ref, glstm_ref,
                  h_out, conv_out, c_out, n_out, m_out,
                  full_sc, c_sc, n_sc, m_sc, *, L, G, nc):
    def sequence(g):
        tok = lambda ref: ref.at[g * L:(g + 1) * L]
        return _mlstm_sequence(tok(qk_ref), tok(v_ref), tok(o_ref), tok(misc_ref), conv0_ref.at[g], c0_ref.at[g],
                               n0_ref.at[g], m0_ref.at[g], wconv_ref, bconv_ref, bgate_ref, glstm_ref,
                               tok(h_out), conv_out.at[g], c_out.at[g], n_out.at[g], m_out.at[g],
                               full_sc.at[g], c_sc.at[g], n_sc.at[g], m_sc.at[g], L=L, nc=nc)

    _in_turn(*[sequence(g) for g in range(G)])


def _mlstm_sequence(qk_ref, v_ref, o_ref, misc_ref, conv0_ref, c0_ref, n0_ref, m0_ref, wconv_ref, bconv_ref,
                    bgate_ref, glstm_ref,
                    h_out, conv_out, c_out, n_out, m_out,
                    full_sc, c_sc, n_sc, m_sc, *, L, nc):
    hist = CONV_W - 1
    lo = CONV_PAD - hist
    first_chunk = (lambda f: f()) if nc == 1 else pl.when(pl.program_id(1) == 0)
    last_chunk = (lambda f: f()) if nc == 1 else pl.when(pl.program_id(1) == nc - 1)

    @first_chunk
    def _init():
        full_sc[lo:CONV_PAD, :] = conv0_ref[...]
        c_sc[...] = c0_ref[...]
        n_sc[...] = n0_ref[...]
        m_sc[...] = m0_ref[...]

    full_sc[CONV_PAD:CONV_PAD + L, :] = qk_ref[...]
    ext = full_sc[...]
    y = bconv_ref[...] + ext[CONV_PAD:, :] * wconv_ref[CONV_W - 1:CONV_W, :]
    for r in range(1, CONV_W):
        y = y + pltpu.roll(ext, r, 0)[CONV_PAD:, :] * wconv_ref[CONV_W - 1 - r:CONV_W - r, :]
    qk = y * jax.nn.sigmoid(y)
    tail = full_sc[lo + L:CONV_PAD + L, :]
    full_sc[lo:CONV_PAD, :] = tail
    yield

    LP = max(L, LANES)
    pad_rows = lambda a: a if L == LP else jnp.concatenate([a, jnp.zeros((LP - L, a.shape[1]), a.dtype)], axis=0)
    gs_t = pad_rows(misc_ref[...] + bgate_ref[...]).T
    g8 = gs_t[MISC_IG:MISC_IG + 2 * LSTM_HEADS, :]
    lf8 = jnp.minimum(g8, 0.0) - jnp.log1p(jnp.exp(-jnp.abs(g8)))
    s_idx = lax.broadcasted_iota(jnp.int32, (LP, LP), 0)
    t_idx = lax.broadcasted_iota(jnp.int32, (LP, LP), 1)
    causal = s_idx <= t_idx
    triu = jnp.where(causal, 1.0, 0.0).astype(BF16)
    lf_hi = lf8.astype(BF16)
    lf_mid = (lf8 - lf_hi.astype(F32)).astype(BF16)
    lf_lo = (lf8 - lf_hi.astype(F32) - lf_mid.astype(F32)).astype(BF16)
    b8 = (jnp.dot(lf_hi, triu, preferred_element_type=F32)
          + jnp.dot(lf_mid, triu, preferred_element_type=F32)
          + jnp.dot(lf_lo, triu, preferred_element_type=F32))
    yield
    c4 = g8[:LSTM_HEADS, :] - b8[LSTM_HEADS:, :]
    c_cols = jnp.concatenate([c4, jnp.zeros((LANES - LSTM_HEADS, LP), F32)], axis=0).T
    lane_t = lax.broadcasted_iota(jnp.int32, (1, LP), 1)

    def head(h):
        dk = slice(h * LSTM_DK, (h + 1) * LSTM_DK)
        dv = slice(h * LSTM_DV, (h + 1) * LSTM_DV)
        ig_row = g8[h:h + 1, :]
        b_row = b8[LSTM_HEADS + h:LSTM_HEADS + h + 1, :]
        c_col = c_cols[:, h:h + 1]
        m_prev = m_sc[:, h:h + 1]

        log_d = jnp.where(causal, c_col + b_row, -jnp.inf)
        inter = b_row + m_prev
        m_t = jnp.maximum(inter, jnp.max(log_d, axis=0, keepdims=True))
        decay = jnp.exp(log_d - m_t)
        inter_scale = jnp.exp(inter - m_t)

        qb = pad_rows(qk[:, dk]).astype(BF16)
        kb = pad_rows(qk[:, LSTM_HEADS * LSTM_DK + h * LSTM_DK:LSTM_HEADS * LSTM_DK + (h + 1) * LSTM_DK]
                      * (LSTM_DK ** -0.5)).astype(BF16)
        v_t = pad_rows(v_ref[:, dv]).T
        c_prev = c_sc[h]
        n_prev = n_sc[h:h + 1, :]
        nt = (((1,), (1,)), ((), ()))
        wgt = lax.dot_general(kb, qb, nt, preferred_element_type=F32) * decay
        cq = lax.dot_general(c_prev.astype(BF16), qb, nt, preferred_element_type=F32)
        nq = lax.dot_general(jnp.broadcast_to(n_prev, (SUBLANES, LSTM_DK)).astype(BF16), qb, nt,
                             preferred_element_type=F32)[:1, :]
        yield
        num = jnp.dot(v_t.astype(BF16), wgt.astype(BF16), preferred_element_type=F32) + inter_scale * cq
        den = jnp.sum(wgt, axis=0, keepdims=True) + inter_scale * nq
        hid = num / jnp.maximum(jnp.abs(den), jnp.exp(-m_t))
        yield

        m_new = m_t[:, L - 1:L]
        b_last = b_row[:, L - 1:L]
        carry = jnp.exp(b_last + m_prev - m_new)
        d_end = jnp.exp(b_last - b_row + ig_row - m_new)
        if L < LP:
            d_end = jnp.where(lane_t < L, d_end, 0.0)
        upd = jnp.dot((v_t * d_end).astype(BF16), kb, preferred_element_type=F32)
        nk = jnp.dot(jnp.broadcast_to(d_end, (SUBLANES, LP)).astype(BF16), kb,
                     preferred_element_type=F32)[:1, :]
        c_sc[h] = carry * c_prev + upd
        n_sc[h:h + 1, :] = carry * n_prev + nk
        m_sc[:, h:h + 1] = m_new

        g_col = jnp.concatenate([glstm_ref[dv, :]] * (LP // LANES), axis=1)
        hn = hid * lax.rsqrt(jnp.mean(hid * hid, axis=0, keepdims=True) + EPS) * g_col
        h_out[:, dv] = (hn.T[:L, :] * jax.nn.sigmoid(o_ref[:, dv])).astype(h_out.dtype)
        yield

    for _ in zip(*[head(h) for h in range(LSTM_HEADS)]):
        yield

    @last_chunk
    def _finish():
        conv_out[...] = tail
        c_out[...] = c_sc[...]
        n_out[...] = n_sc[...]
        m_out[...] = m_sc[...]

    yield


def _mlstm(qk_raw, v_raw, o_raw, misc, conv0, c0, n0, m0, layer, w, L, G):
    b = conv0.shape[1]
    t = qk_raw.shape[0]
    nc = t // (b * L)
    assert nc * b * L == t and b % G == 0 and (G == 1 or nc == 1)
    hist = CONV_W - 1
    tok = lambda cdim: pl.BlockSpec((G * L, cdim), lambda bi, ci: (bi * nc + ci, 0))
    m0 = m0.reshape(m0.shape[0], b, 1, LSTM_HEADS)
    state0 = lambda *dims: pl.BlockSpec((None, G) + dims, lambda bi, ci: (layer, bi) + (0,) * len(dims))
    weights = [w["w_conv"], w["b_conv"], w["bgate"], w["g_lstm_out"]]
    out_shape = (
        jax.ShapeDtypeStruct((t, LSTM_WIDTH), BF16),
        jax.ShapeDtypeStruct((b, hist, LSTM_QK), F32),
        jax.ShapeDtypeStruct((b, LSTM_HEADS, LSTM_DV, LSTM_DK), F32),
        jax.ShapeDtypeStruct((b, LSTM_HEADS, LSTM_DK), F32),
        jax.ShapeDtypeStruct((b, 1, LSTM_HEADS), F32),
    )
    state = lambda *dims: pl.BlockSpec((G,) + dims, lambda bi, ci: (bi,) + (0,) * len(dims))
    h, conv_new, c_new, n_new, m_new = pl.pallas_call(
        functools.partial(_mlstm_kernel, L=L, G=G, nc=nc),
        grid=(b // G, nc),
        in_specs=[tok(LSTM_QK), tok(LSTM_WIDTH), tok(LSTM_WIDTH), tok(LANES),
                  state0(hist, LSTM_QK), state0(LSTM_HEADS, LSTM_DV, LSTM_DK), state0(LSTM_HEADS, LSTM_DK),
                  state0(1, LSTM_HEADS)] + [_full_spec(a) for a in weights],
        out_specs=(tok(LSTM_WIDTH), state(hist, LSTM_QK), state(LSTM_HEADS, LSTM_DV, LSTM_DK),
                   state(LSTM_HEADS, LSTM_DK), state(1, LSTM_HEADS)),
        out_shape=out_shape,
        scratch_shapes=[pltpu.VMEM((G, CONV_PAD + L, LSTM_QK), F32),
                        pltpu.VMEM((G, LSTM_HEADS, LSTM_DV, LSTM_DK), F32),
                        pltpu.VMEM((G, LSTM_HEADS, LSTM_DK), F32),
                        pltpu.VMEM((G, 1, LSTM_HEADS), F32)],
        compiler_params=pltpu.CompilerParams(dimension_semantics=("parallel", "arbitrary"),
                                             vmem_limit_bytes=VMEM_LIMIT),
        name="mlstm",
    )(qk_raw, v_raw, o_raw, misc, conv0, c0, n0, m0, *weights)
    return h, conv_new, c_new, n_new, m_new.reshape(b, LSTM_HEADS)


FF_CHUNK = 1024


def _out_ffn_kernel(x_ref, attn_ref, lstm_ref, w_out_ref, g_ffn_ref, w_up_ref, w_down_ref, y_ref):
    mix = jnp.concatenate([attn_ref[...], lstm_ref[...]], axis=1)
    x1 = x_ref[...] + jnp.dot(mix, w_out_ref[...], preferred_element_type=F32)
    xb = _rms(x1, g_ffn_ref[...]).astype(BF16)
    d_ff = w_up_ref.shape[1]
    acc = x1
    for f in range(d_ff // FF_CHUNK):
        cols = slice(f * FF_CHUNK, (f + 1) * FF_CHUNK)
        u = jnp.maximum(jnp.dot(xb, w_up_ref[:, cols], preferred_element_type=F32), 0.0)
        acc = acc + jnp.dot((u * u).astype(BF16), w_down_ref[cols, :], preferred_element_type=F32)
    y_ref[...] = acc


def _out_ffn(x2, attn, lstm, w, tm):
    t, d = x2.shape
    assert t % tm == 0
    row = lambda c: pl.BlockSpec((tm, c), lambda i: (i, 0))
    const = lambda a: pl.BlockSpec(a.shape, lambda i: (0, 0), pipeline_mode=pl.Buffered(1))
    weights = [w["w_out"], w["g_ffn"], w["w_up"], w["w_down"]]
    return pl.pallas_call(
        _out_ffn_kernel,
        grid=(t // tm,),
        in_specs=[row(d), row(MLA_WIDTH), row(LSTM_WIDTH)] + [const(a) for a in weights],
        out_specs=row(d),
        out_shape=jax.ShapeDtypeStruct((t, d), F32),
        compiler_params=pltpu.CompilerParams(dimension_semantics=("parallel",),
                                             vmem_limit_bytes=VMEM_LIMIT),
        name="out_ffn",
    )(x2, attn, lstm, *weights)


def _pack_layer(l, g_mix, w_in, g_q_lat, w_uq, g_q_nope, g_q_rope, g_kv_lat, g_k_rope, w_uk, g_k_nope, w_uv,
                w_conv, b_conv, b_igate, b_fgate, g_lstm_out, w_out, g_ffn, w_up, w_down):
    d = w_in.shape[1]
    wi = w_in[l]
    kpe = wi[:, W_IN_LAT[1]:W_IN_LAT[1] + ROPE_DIM]
    gates = wi[:, W_IN_LSTM[1]:W_IN_LSTM[1] + 2 * LSTM_HEADS]
    misc = jnp.concatenate([kpe, _rot_half(kpe), gates,
                            jnp.zeros((d, LANES - MISC_FG - LSTM_HEADS), F32)], axis=1)

    uq = w_uq[l].reshape(Q_LORA, MLA_HEADS, QK_DIM)
    uq = jnp.concatenate([uq, _rot_half(uq[..., NOPE_DIM:])], axis=-1).reshape(Q_LORA, MLA_HEADS * LANES)
    gq_head = jnp.concatenate([g_q_nope[l], g_q_rope[l], _rot_half(g_q_rope[l])])
    uk = w_uk[l].reshape(KV_LORA, MLA_HEADS, NOPE_DIM)
    uk = jnp.concatenate([uk, jnp.zeros_like(uk)], axis=-1).reshape(KV_LORA, MLA_HEADS * LANES)
    gkn_head = jnp.concatenate([g_k_nope[l], jnp.zeros((LANES - NOPE_DIM,), F32)])
    uv = w_uv[l].reshape(KV_LORA, MLA_HEADS, V_DIM)
    uv_t = jnp.concatenate([uv, jnp.zeros((KV_LORA, MLA_HEADS, VT_ROWS - V_DIM), F32)], axis=-1)
    uv_t = uv_t.reshape(KV_LORA, MLA_HEADS * VT_ROWS).T
    vones_head = jnp.zeros((VT_ROWS,), F32).at[V_DIM].set(1.0)
    gkm = jnp.concatenate([g_k_rope[l], _rot_half(g_k_rope[l]), jnp.zeros((LANES - 2 * ROPE_DIM,), F32)])
    bgate = jnp.concatenate([jnp.zeros((MISC_IG,), F32), b_igate[l], b_fgate[l],
                             jnp.zeros((LANES - MISC_FG - LSTM_HEADS,), F32)])

    i = jnp.arange(HEAD_PAIR)
    same = (i[:, None] // LANES) == (i[None, :] // LANES)
    li, lj = i[:, None] % LANES, i[None, :] % LANES
    mseg = jnp.where(same & (li < NOPE_DIM) & (lj < NOPE_DIM), 1.0 / NOPE_DIM,
                     jnp.where(same & (li >= NOPE_DIM) & (li < QK_DIM) & (lj >= NOPE_DIM), 1.0 / ROPE_DIM, 0.0))
    a = jnp.arange(LANES)
    mmisc = jnp.where((a[:, None] < ROPE_DIM) & (a[None, :] < 2 * ROPE_DIM), 1.0 / ROPE_DIM, 0.0)

    row = lambda v: v.reshape(1, -1).astype(F32)
    return {
        "g_mix": row(g_mix[l]), "w_lat": wi[:, W_IN_LAT[0]:W_IN_LAT[1]].astype(BF16),
        "w_lstm": wi[:, W_IN_LSTM[0]:W_IN_LSTM[1]].astype(BF16), "w_misc": misc.astype(BF16),
        "g_q_lat": row(g_q_lat[l]), "w_uq": uq.astype(BF16),
        "gq": row(jnp.tile(gq_head, MLA_HEADS)), "g_kv_lat": row(g_kv_lat[l]), "gkm": row(gkm),
        "w_uk": uk.astype(BF16), "gkn": row(jnp.tile(gkn_head, MLA_HEADS)), "w_uv": w_uv[l].astype(BF16),
        "w_uv_t": uv_t.astype(BF16), "vones": jnp.tile(vones_head, MLA_HEADS).reshape(-1, 1),
        "mseg": mseg.astype(BF16), "mmisc": mmisc.astype(BF16),
        "w_conv": w_conv[l], "b_conv": row(b_conv[l]), "bgate": row(bgate), "g_lstm_out": jnp.broadcast_to(g_lstm_out[l][:, None], (LSTM_WIDTH, LANES)),
        "w_out": w_out[l].astype(BF16), "g_ffn": row(g_ffn[l]), "w_up": w_up[l].astype(BF16),
        "w_down": w_down[l].astype(BF16),
    }


def _rope_table(first_pos, n):
    half = ROPE_DIM // 2
    blk = min(n, LANES)
    assert n % blk == 0
    lane = jnp.arange(LANES, dtype=jnp.int32)
    inv = ROPE_BASE ** (-(lane % half).astype(F32) / half)
    ang_a = (first_pos + blk * jnp.arange(n // blk, dtype=jnp.int32)).astype(F32)[:, None] * inv[None, :]
    ang_b = jnp.arange(blk, dtype=jnp.int32).astype(F32)[:, None] * inv[None, :]
    ca, sa = jnp.cos(ang_a)[:, None, :], jnp.sin(ang_a)[:, None, :]
    cb, sb = jnp.cos(ang_b)[None, :, :], jnp.sin(ang_b)[None, :, :]
    cos = ca * cb - sa * sb
    sin = sa * cb + ca * sb
    kind = (lane // half) % 4
    tab = jnp.where(kind < 2, cos, jnp.where(kind == 2, -sin, sin))
    return tab.reshape(n, LANES)


def kernel(x_prompt, x_sample, cache_kv_latent, cache_k_rope, state_conv, state_C, state_n, state_m,
           g_mix, w_in, g_q_lat, w_uq, g_q_nope, g_q_rope, g_kv_lat, g_k_rope, w_uk, g_k_nope, w_uv,
           w_conv, b_conv, b_igate, b_fgate, g_lstm_out, w_out, g_ffn, w_up, w_down):
    depth = w_in.shape[0]
    bp, sp, d = x_prompt.shape
    bs, ls, _ = x_sample.shape
    past = cache_kv_latent.shape[2]
    hist = CONV_W - 1

    tabs_p = jnp.tile(_rope_table(0, sp), (bp, 1))
    tabs_s = jnp.tile(_rope_table(past, ls), (bs, 1))
    cache_kr_t = jnp.swapaxes(cache_k_rope, 2, 3)
    xp = x_prompt.reshape(bp * sp, d)
    xs = x_sample.reshape(bs * ls, d)
    zero_conv = jnp.zeros((1, bp, hist, LSTM_QK), F32)
    zero_c = jnp.zeros((1, bp, LSTM_HEADS, LSTM_DV, LSTM_DK), F32)
    zero_n = jnp.zeros((1, bp, LSTM_HEADS, LSTM_DK), F32)
    zero_m = jnp.zeros((1, bp, LSTM_HEADS), F32)

    outs = {k: [] for k in ("p_lat", "p_kr", "p_conv", "p_c", "p_n", "p_m",
                            "s_lat", "s_kr", "s_conv", "s_c", "s_n", "s_m")}
    for l in range(depth):
        w = _pack_layer(l, g_mix, w_in, g_q_lat, w_uq, g_q_nope, g_q_rope, g_kv_lat, g_k_rope, w_uk, g_k_nope,
                        w_uv, w_conv, b_conv, b_igate, b_fgate, g_lstm_out, w_out, g_ffn, w_up, w_down)
        q, k, v, lat, kr, qk_raw, v_raw, o_raw, misc = _projection(xp, tabs_p, w, tm=512)
        attn = _attention_prompt(q.reshape(bp, sp, -1), k.reshape(bp, sp, -1), v,
                                 tq=ATTN_TQ, tk=ATTN_TK)
        h, conv_new, c_new, n_new, m_new = _mlstm(qk_raw, v_raw, o_raw, misc, zero_conv, zero_c, zero_n, zero_m,
                                                  0, w, L=MLSTM_TILE, G=1)
        xp = _out_ffn(xp, attn.reshape(bp * sp, -1), h, w, tm=512)
        outs["p_lat"].append(lat.reshape(bp, sp, KV_LORA))
        outs["p_kr"].append(kr.reshape(ROPE_DIM, bp, sp))
        outs["p_conv"].append(conv_new)
        outs["p_c"].append(c_new)
        outs["p_n"].append(n_new)
        outs["p_m"].append(m_new)
        q, k, v, lat, kr, qk_raw, v_raw, o_raw, misc = _projection(xs, tabs_s, w, tm=bs * ls)
        attn = _attention_sample(q, k, lat, cache_kv_latent, cache_kr_t, l, w, n_new=ls, G=ATTN_SAMPLE_GROUP)
        h, conv_new, c_new, n_new, m_new = _mlstm(qk_raw, v_raw, o_raw, misc, state_conv, state_C,
                                                  state_n, state_m, l, w, L=ls, G=MLSTM_SAMPLE_GROUP)
        xs = _out_ffn(xs, attn, h, w, tm=bs * ls)
        outs["s_lat"].append(lat.reshape(bs, ls, KV_LORA))
        outs["s_kr"].append(kr.reshape(ROPE_DIM, bs, ls))
        outs["s_conv"].append(conv_new)
        outs["s_c"].append(c_new)
        outs["s_n"].append(n_new)
        outs["s_m"].append(m_new)

    st = lambda key: jnp.stack(outs[key])
    kr_st = lambda key: jnp.transpose(jnp.stack(outs[key]), (0, 2, 3, 1))
    return (xp.reshape(bp, sp, d), xs.reshape(bs, ls, d),
            st("p_lat"), kr_st("p_kr"), st("p_conv"), st("p_c"), st("p_n"), st("p_m"),
            st("s_lat"), kr_st("s_kr"), st("s_conv"), st("s_c"), st("s_n"), st("s_m"))
```

```python
import functools

import jax
import jax.numpy as jnp
from jax import lax
from jax.experimental import pallas as pl
from jax.experimental.pallas import tpu as pltpu

F32 = jnp.float32
BF16 = jnp.bfloat16

EPS = 1e-6
CHUNK = 64
MLA_HEADS = 8
NOPE_DIM = 64
ROPE_DIM = 32
QK_DIM = NOPE_DIM + ROPE_DIM
V_DIM = 64
Q_LORA = 256
KV_LORA = 128
ROPE_BASE = 10000.0
LSTM_HEADS = 4
LSTM_DK = 128
LSTM_DV = 128
CONV_W = 4
LSTM_WIDTH = LSTM_HEADS * LSTM_DV
LSTM_QK = 2 * LSTM_HEADS * LSTM_DK
MLA_WIDTH = MLA_HEADS * V_DIM

LANES = 128
SUBLANES = 8
HEAD_PAIR = 2 * LANES
VMEM_LIMIT = 52 * 1024 * 1024
Q_LOG2_SCALE = (QK_DIM ** -0.5) * 1.4426950408889634
ATTN_TQ = 512
ATTN_TK = 512
PAIRS_PER_TRIP = 4
MLSTM_TILE = 256
ATTN_SAMPLE_GROUP = 4
MLSTM_SAMPLE_GROUP = 4
VT_ROWS = 80

MISC_KPE = 0
MISC_KPE_ROT = ROPE_DIM
MISC_IG = 2 * ROPE_DIM
MISC_FG = MISC_IG + LSTM_HEADS

COL_CQ = ("lat", 0, Q_LORA)
COL_CKV = ("lat", Q_LORA, Q_LORA + KV_LORA)
COL_QK = ("lstm", 0, LSTM_QK)
COL_V = ("lstm", LSTM_QK, LSTM_QK + LSTM_WIDTH)
COL_O = ("lstm", LSTM_QK + LSTM_WIDTH, LSTM_QK + 2 * LSTM_WIDTH)
COL_MISC = ("misc", 0, LANES)
W_IN_LAT = (0, Q_LORA + KV_LORA)
W_IN_LSTM = (Q_LORA + KV_LORA + ROPE_DIM, Q_LORA + KV_LORA + ROPE_DIM + LSTM_QK + 2 * LSTM_WIDTH)


def _rot_half(a):
    half = ROPE_DIM // 2
    return jnp.concatenate([a[..., half:], a[..., :half]], axis=-1)


def _rms(x, g):
    return x * lax.rsqrt(jnp.mean(x * x, axis=-1, keepdims=True) + EPS) * g


def _in_turn(*stage_generators):
    for _ in zip(*stage_generators):
        pass


def _segment_mean_sq(y, m_ref):
    return jnp.dot((y * y).astype(BF16), m_ref[...], preferred_element_type=F32)


def _proj_kernel(x_ref, rope_ref, g_mix_ref, w_lat_ref, w_lstm_ref, w_misc_ref, g_qlat_ref, w_uq_ref,
                 gq_ref, g_kvlat_ref, gkm_ref, w_uk_ref, gkn_ref, w_uv_ref, vones_ref, mseg_ref, mmisc_ref,
                 q_out, k_out, v_out, lat_out, kr_out, qk_out, vl_out, o_out, misc_out):
    x = x_ref[...]
    xb = _rms(x, g_mix_ref[...]).astype(BF16)

    blocks = {"lat": w_lat_ref, "lstm": w_lstm_ref, "misc": w_misc_ref}

    def proj(col):
        return jnp.dot(xb, blocks[col[0]][:, col[1]:col[2]], preferred_element_type=F32)

    c_q = proj(COL_CQ)
    c_kv = proj(COL_CKV)
    misc = proj(COL_MISC)
    qk_out[...] = proj(COL_QK)
    misc_out[...] = misc

    tab = rope_ref[...]
    tab_next = pltpu.roll(tab, LANES - ROPE_DIM, 1)
    lane = lax.broadcasted_iota(jnp.int32, tab.shape, 1)
    in_rope = (lane >= NOPE_DIM) & (lane < QK_DIM)
    cq_tab = jnp.where(lane < NOPE_DIM, 1.0, jnp.where(in_rope, tab, 0.0))
    sq_tab = jnp.where(in_rope, tab_next, 0.0)
    ck_tab = jnp.where(lane < ROPE_DIM, tab, 0.0)
    sk_tab = jnp.where(lane < ROPE_DIM, tab_next, 0.0)

    cqn = _rms(c_q, g_qlat_ref[...]).astype(BF16)
    lat = _rms(c_kv, g_kvlat_ref[...])
    lat_out[...] = lat
    latb = lat.astype(BF16)
    lat_t = lat.T.astype(BF16)
    pairs = range(MLA_HEADS // 2)
    cols = [slice(p * HEAD_PAIR, (p + 1) * HEAD_PAIR) for p in pairs]

    q_raw = [jnp.dot(cqn, w_uq_ref[:, cols[p]], preferred_element_type=F32) for p in pairs]
    misc_mean = _segment_mean_sq(misc, mmisc_ref)
    vl_out[...] = proj(COL_V)
    q_mean = [_segment_mean_sq(q_raw[p], mseg_ref) for p in pairs]
    k_raw = [jnp.dot(latb, w_uk_ref[:, cols[p]], preferred_element_type=F32) for p in pairs]
    o_out[...] = proj(COL_O)
    k_mean = [_segment_mean_sq(k_raw[p], mseg_ref) for p in pairs]
    v_out[...] = (jnp.dot(w_uv_ref[...], lat_t, preferred_element_type=F32) + vones_ref[...]).astype(BF16)

    for p in pairs:
        y = q_raw[p] * lax.rsqrt(q_mean[p] + EPS) * gq_ref[:, cols[p]]
        for j in range(2):
            yh = y[:, j * LANES:(j + 1) * LANES]
            qh = yh * cq_tab + pltpu.roll(yh, LANES - ROPE_DIM, 1) * sq_tab
            h = 2 * p + j
            q_out[:, h * LANES:(h + 1) * LANES] = (qh * Q_LOG2_SCALE).astype(BF16)

    ms = misc * lax.rsqrt(misc_mean + EPS) * gkm_ref[...]
    kr = ms * ck_tab + pltpu.roll(ms, LANES - ROPE_DIM, 1) * sk_tab
    kr_out[...] = kr.T[:ROPE_DIM, :]
    kr_placed = pltpu.roll(kr, NOPE_DIM, 1)
    kr_pair = jnp.concatenate([kr_placed, kr_placed], axis=1)
    for p in pairs:
        y = k_raw[p] * lax.rsqrt(k_mean[p] + EPS) * gkn_ref[:, cols[p]]
        k_out[:, cols[p]] = (y + kr_pair).astype(BF16)


def _full_spec(a):
    nd = a.ndim
    return pl.BlockSpec(a.shape, lambda *_: (0,) * nd)


def _projection(x2, rope_tab, w, tm):
    t = x2.shape[0]
    assert t % tm == 0
    row = lambda c: pl.BlockSpec((tm, c), lambda i: (i, 0))
    weights = [w["g_mix"], w["w_lat"], w["w_lstm"], w["w_misc"], w["g_q_lat"], w["w_uq"], w["gq"], w["g_kv_lat"],
               w["gkm"], w["w_uk"], w["gkn"], w["w_uv_t"], w["vones"], w["mseg"], w["mmisc"]]
    out_shape = (
        jax.ShapeDtypeStruct((t, MLA_HEADS * LANES), BF16),
        jax.ShapeDtypeStruct((t, MLA_HEADS * LANES), BF16),
        jax.ShapeDtypeStruct((MLA_HEADS * VT_ROWS, t), BF16),
        jax.ShapeDtypeStruct((t, KV_LORA), F32),
        jax.ShapeDtypeStruct((ROPE_DIM, t), F32),
        jax.ShapeDtypeStruct((t, LSTM_QK), F32),
        jax.ShapeDtypeStruct((t, LSTM_WIDTH), F32),
        jax.ShapeDtypeStruct((t, LSTM_WIDTH), F32),
        jax.ShapeDtypeStruct((t, LANES), F32),
    )
    return pl.pallas_call(
        _proj_kernel,
        grid=(t // tm,),
        in_specs=[row(x2.shape[1]), row(LANES)] + [_full_spec(a) for a in weights],
        out_specs=tuple(pl.BlockSpec((s.shape[0], tm), lambda i: (0, i)) if n in (2, 4) else row(s.shape[1])
                        for n, s in enumerate(out_shape)),
        out_shape=out_shape,
        compiler_params=pltpu.CompilerParams(dimension_semantics=("parallel",),
                                             vmem_limit_bytes=VMEM_LIMIT),
        name="projection",
    )(x2, rope_tab, *weights)


def _attn_prompt_kernel(q_ref, qnext_ref, k_ref, vt_ref, o_ref, qt_sc, m_sc, acc_sc, sa_sc, sb_sc, *, tq, tk):
    qi = pl.program_id(2)
    m_sc[...] = jnp.full(m_sc.shape, -1e30, F32)
    acc_sc[...] = jnp.zeros(acc_sc.shape, F32)
    q_col0 = qi * tq
    for j in range(2):
        qt_sc[j] = q_ref[:, j * LANES:(j + 1) * LANES].astype(F32).T.astype(BF16)

    def scores_head(j, k0, s_ref):
        k = k_ref[pl.ds(k0, tk), j * LANES:(j + 1) * LANES]
        s_ref[j] = jnp.dot(k, qt_sc[j], preferred_element_type=F32)

    def scores(k0, s_ref):
        for j in range(2):
            scores_head(j, k0, s_ref)

    def next_tile_scores():
        for j in range(2):
            qt = qnext_ref[:, j * LANES:(j + 1) * LANES].astype(F32).T.astype(BF16)
            sa_sc[j] = jnp.dot(k_ref[pl.ds(0, tk), j * LANES:(j + 1) * LANES], qt, preferred_element_type=F32)

    def consume_head(j, k0, s_ref, masked):
        vt = vt_ref[j * VT_ROWS:(j + 1) * VT_ROWS, pl.ds(k0, tk)]
        s = s_ref[j]
        if masked:
            kc = (k0 + lax.broadcasted_iota(jnp.int32, (tk, tq), 0)) // CHUNK
            qc = (q_col0 + lax.broadcasted_iota(jnp.int32, (tk, tq), 1)) // CHUNK
            s = jnp.where(kc <= qc, s, -1e30)
        m_prev = m_sc[j]
        m_next = jnp.maximum(m_prev, jnp.max(s, axis=0, keepdims=True))
        alpha = jnp.exp2(m_prev - m_next)
        p = jnp.exp2(s - m_next)
        m_sc[j] = m_next
        acc_sc[j] = alpha * acc_sc[j] + jnp.dot(vt, p.astype(BF16), preferred_element_type=F32)

    def consume(k0, s_ref, masked):
        for j in range(2):
            consume_head(j, k0, s_ref, masked)

    n_full = q_col0 // tk
    at = lambda t: pl.multiple_of(t * tk, tk)

    @pl.when(qi == 0)
    def _first_tile_of_pair():
        scores(0, sa_sc)

    def pair(t):
        scores(at(t + 1), sb_sc)
        for j in range(2):
            consume_head(j, at(t), sa_sc, False)
            scores_head(j, at(t + 2), sa_sc)
        consume(at(t + 1), sb_sc, False)

    def body(i, carry):
        for u in range(PAIRS_PER_TRIP):
            pair(2 * PAIRS_PER_TRIP * i + 2 * u)
        return carry

    trips = n_full // (2 * PAIRS_PER_TRIP)
    lax.fori_loop(0, trips, body, 0)

    def rest(i, carry):
        pair(2 * PAIRS_PER_TRIP * trips + 2 * i)
        return carry

    lax.fori_loop(0, (n_full % (2 * PAIRS_PER_TRIP)) // 2, rest, 0)

    t0 = 2 * (n_full // 2)

    @pl.when(n_full % 2 == 1)
    def _odd_tail():
        scores(at(t0 + 1), sb_sc)
        consume(at(t0), sa_sc, False)
        next_tile_scores()
        consume(at(t0 + 1), sb_sc, True)

    @pl.when(n_full % 2 == 0)
    def _even_tail():
        consume(at(t0), sa_sc, True)
        next_tile_scores()

    outs = []
    for j in range(2):
        acc = acc_sc[j]
        outs.append(acc[:V_DIM, :] / acc[V_DIM:V_DIM + 1, :])
    o_ref[...] = jnp.concatenate(outs, axis=0).T.astype(o_ref.dtype)


def _attention_prompt(q, k, vt, tq, tk):
    b, s, _ = q.shape
    assert tq == tk and s % tq == 0 and tq % CHUNK == 0
    pairs = MLA_HEADS // 2
    return pl.pallas_call(
        functools.partial(_attn_prompt_kernel, tq=tq, tk=tk),
        grid=(b, pairs, s // tq),
        in_specs=[
            pl.BlockSpec((None, tq, HEAD_PAIR), lambda bi, p, i: (bi, i, p)),
            pl.BlockSpec((None, tq, HEAD_PAIR), lambda bi, p, i: (bi, jnp.minimum(i + 1, s // tq - 1), p)),
            pl.BlockSpec((None, s, HEAD_PAIR), lambda bi, p, i: (bi, 0, p)),
            pl.BlockSpec((2 * VT_ROWS, s), lambda bi, p, i: (p, bi)),
        ],
        out_specs=pl.BlockSpec((None, tq, LANES), lambda bi, p, i: (bi, i, p)),
        out_shape=jax.ShapeDtypeStruct((b, s, MLA_WIDTH), BF16),
        scratch_shapes=[pltpu.VMEM((2, LANES, tq), BF16),
                        pltpu.VMEM((2, 1, tq), F32),
                        pltpu.VMEM((2, VT_ROWS, tq), F32),
                        pltpu.VMEM((2, tk, tq), F32),
                        pltpu.VMEM((2, tk, tq), F32)],
        compiler_params=pltpu.CompilerParams(
            dimension_semantics=("parallel", "parallel", "arbitrary"),
            vmem_limit_bytes=VMEM_LIMIT),
        name="attention_prompt",
    )(q, q, k, vt)


def _attn_sample_kernel(q_ref, kn_ref, latn_ref, lat_ref, kr_ref, w_uk_ref, gkn_ref, w_uv_ref, mseg_ref,
                        o_ref, k_sc, *, n_new, G):
    def sequence(g):
        tok = lambda ref: ref.at[g * n_new:(g + 1) * n_new]
        return _attn_sample_sequence(tok(q_ref), tok(kn_ref), tok(latn_ref), lat_ref.at[g], kr_ref.at[g],
                                     w_uk_ref, gkn_ref, w_uv_ref, mseg_ref, tok(o_ref), k_sc.at[g], n_new=n_new)

    _in_turn(*[sequence(g) for g in range(G)])


def _attn_sample_sequence(q_ref, kn_ref, latn_ref, lat_ref, kr_ref, w_uk_ref, gkn_ref, w_uv_ref, mseg_ref,
                          o_ref, k_sc, *, n_new):
    latb = lat_ref[...].astype(BF16)
    kr_t = kr_ref[...]
    zeros = lambda r: jnp.zeros((r, kr_t.shape[1]), F32)
    kr_placed = jnp.concatenate([zeros(NOPE_DIM), kr_t, zeros(LANES - QK_DIM)], axis=0).T
    kr_pair = jnp.concatenate([kr_placed, kr_placed], axis=1)
    pairs = range(MLA_HEADS // 2)
    cols = [slice(p * HEAD_PAIR, (p + 1) * HEAD_PAIR) for p in pairs]
    raw = [jnp.dot(latb, w_uk_ref[:, cols[p]], preferred_element_type=F32) for p in pairs]
    yield
    mean_sq = [_segment_mean_sq(raw[p], mseg_ref) for p in pairs]
    yield
    for p in pairs:
        y = raw[p] * lax.rsqrt(mean_sq[p] + EPS) * gkn_ref[:, cols[p]]
        k_sc[:, cols[p]] = (y + kr_pair).astype(BF16)

    q = q_ref[...]
    qt = jnp.concatenate([q] * MLA_HEADS, axis=0)
    r_head = lax.broadcasted_iota(jnp.int32, qt.shape, 0) // n_new
    c_head = lax.broadcasted_iota(jnp.int32, qt.shape, 1) // LANES
    qm = jnp.where(r_head == c_head, qt, jnp.zeros_like(qt))

    nt = (((1,), (1,)), ((), ()))
    s_old = lax.dot_general(k_sc[...], qm, nt, preferred_element_type=F32)
    s_new = lax.dot_general(kn_ref[...], qm, nt, preferred_element_type=F32)
    yield
    mx = jnp.maximum(jnp.max(s_old, axis=0, keepdims=True), jnp.max(s_new, axis=0, keepdims=True))
    p_old = jnp.exp2(s_old - mx)
    p_new = jnp.exp2(s_new - mx)
    inv = 1.0 / (jnp.sum(p_old, axis=0, keepdims=True) + jnp.sum(p_new, axis=0, keepdims=True))
    p_old = (p_old * inv).astype(BF16)
    p_new = (p_new * inv).astype(BF16)
    tn = (((0,), (0,)), ((), ()))
    ctx = (lax.dot_general(p_old, latb, tn, preferred_element_type=F32)
           + lax.dot_general(p_new, latn_ref[...].astype(BF16), tn, preferred_element_type=F32))
    yield
    full = jnp.dot(ctx.astype(BF16), w_uv_ref[...], preferred_element_type=F32)
    v_head = lax.broadcasted_iota(jnp.int32, (n_new, MLA_WIDTH), 1) // V_DIM
    out = jnp.zeros((n_new, MLA_WIDTH), F32)
    for h in range(MLA_HEADS):
        out = out + jnp.where(v_head == h, full[h * n_new:(h + 1) * n_new, :], 0.0)
    o_ref[...] = out.astype(o_ref.dtype)
    yield


def _attention_sample(q, k_new, lat_new, cache_lat, cache_kr, layer, w, n_new, G):
    _, b, past, _ = cache_lat.shape
    assert b % G == 0
    weights = [w["w_uk"], w["gkn"], w["w_uv"], w["mseg"]]
    tok = lambda c: pl.BlockSpec((G * n_new, c), lambda i: (i, 0))
    return pl.pallas_call(
        functools.partial(_attn_sample_kernel, n_new=n_new, G=G),
        grid=(b // G,),
        in_specs=[tok(MLA_HEADS * LANES), tok(MLA_HEADS * LANES), tok(KV_LORA),
                  pl.BlockSpec((None, G, past, KV_LORA), lambda i: (layer, i, 0, 0)),
                  pl.BlockSpec((None, G, ROPE_DIM, past), lambda i: (layer, i, 0, 0))]
                 + [_full_spec(a) for a in weights],
        out_specs=tok(MLA_WIDTH),
        out_shape=jax.ShapeDtypeStruct((b * n_new, MLA_WIDTH), BF16),
        scratch_shapes=[pltpu.VMEM((G, past, MLA_HEADS * LANES), BF16)],
        compiler_params=pltpu.CompilerParams(dimension_semantics=("parallel",),
                                             vmem_limit_bytes=VMEM_LIMIT),
        name="attention_sample",
    )(q, k_new, lat_new, cache_lat, cache_kr, *weights)


CONV_PAD = SUBLANES


def _mlstm_kernel(qk_ref, v_ref, o_ref, misc_ref, conv0_ref, c0_ref, n0_ref, m0_ref, wconv_ref, bconv_ref,
                  bgate_ref, glstm_ref,
                  h_out, conv_out, c_out, n_out, m_out,
                  full_sc, c_sc, n_sc, m_sc, *, L, G, nc):
    def sequence(g):
        tok = lambda ref: ref.at[g * L:(g + 1) * L]
        return _mlstm_sequence(tok(qk_ref), tok(v_ref), tok(o_ref), tok(misc_ref), conv0_ref.at[g], c0_ref.at[g],
                               n0_ref.at[g], m0_ref.at[g], wconv_ref, bconv_ref, bgate_ref, glstm_ref,
                               tok(h_out), conv_out.at[g], c_out.at[g], n_out.at[g], m_out.at[g],
                               full_sc.at[g], c_sc.at[g], n_sc.at[g], m_sc.at[g], L=L, nc=nc)

    _in_turn(*[sequence(g) for g in range(G)])


def _mlstm_sequence(qk_ref, v_ref, o_ref, misc_ref, conv0_ref, c0_ref, n0_ref, m0_ref, wconv_ref, bconv_ref,
                    bgate_ref, glstm_ref,
                    h_out, conv_out, c_out, n_out, m_out,
                    full_sc, c_sc, n_sc, m_sc, *, L, nc):
    hist = CONV_W - 1
    lo = CONV_PAD - hist
    first_chunk = (lambda f: f()) if nc == 1 else pl.when(pl.program_id(1) == 0)
    last_chunk = (lambda f: f()) if nc == 1 else pl.when(pl.program_id(1) == nc - 1)

    @first_chunk
    def _init():
        full_sc[lo:CONV_PAD, :] = conv0_ref[...]
        c_sc[...] = c0_ref[...]
        n_sc[...] = n0_ref[...]
        m_sc[...] = m0_ref[...]

    full_sc[CONV_PAD:CONV_PAD + L, :] = qk_ref[...]
    ext = full_sc[...]
    y = bconv_ref[...] + ext[CONV_PAD:, :] * wconv_ref[CONV_W - 1:CONV_W, :]
    for r in range(1, CONV_W):
        y = y + pltpu.roll(ext, r, 0)[CONV_PAD:, :] * wconv_ref[CONV_W - 1 - r:CONV_W - r, :]
    qk = y * jax.nn.sigmoid(y)
    tail = full_sc[lo + L:CONV_PAD + L, :]
    full_sc[lo:CONV_PAD, :] = tail
    yield

    LP = max(L, LANES)
    pad_rows = lambda a: a if L == LP else jnp.concatenate([a, jnp.zeros((LP - L, a.shape[1]), a.dtype)], axis=0)
    gs_t = pad_rows(misc_ref[...] + bgate_ref[...]).T
    g8 = gs_t[MISC_IG:MISC_IG + 2 * LSTM_HEADS, :]
    lf8 = jnp.minimum(g8, 0.0) - jnp.log1p(jnp.exp(-jnp.abs(g8)))
    s_idx = lax.broadcasted_iota(jnp.int32, (LP, LP), 0)
    t_idx = lax.broadcasted_iota(jnp.int32, (LP, LP), 1)
    causal = s_idx <= t_idx
    triu = jnp.where(causal, 1.0, 0.0).astype(BF16)
    lf_hi = lf8.astype(BF16)
    lf_mid = (lf8 - lf_hi.astype(F32)).astype(BF16)
    lf_lo = (lf8 - lf_hi.astype(F32) - lf_mid.astype(F32)).astype(BF16)
    b8 = (jnp.dot(lf_hi, triu, preferred_element_type=F32)
          + jnp.dot(lf_mid, triu, preferred_element_type=F32)
          + jnp.dot(lf_lo, triu, preferred_element_type=F32))
    yield
    c4 = g8[:LSTM_HEADS, :] - b8[LSTM_HEADS:, :]
    c_cols = jnp.concatenate([c4, jnp.zeros((LANES - LSTM_HEADS, LP), F32)], axis=0).T
    lane_t = lax.broadcasted_iota(jnp.int32, (1, LP), 1)

    def head(h):
        dk = slice(h * LSTM_DK, (h + 1) * LSTM_DK)
        dv = slice(h * LSTM_DV, (h + 1) * LSTM_DV)
        ig_row = g8[h:h + 1, :]
        b_row = b8[LSTM_HEADS + h:LSTM_HEADS + h + 1, :]
        c_col = c_cols[:, h:h + 1]
        m_prev = m_sc[:, h:h + 1]

        log_d = jnp.where(causal, c_col + b_row, -jnp.inf)
        inter = b_row + m_prev
        m_t = jnp.maximum(inter, jnp.max(log_d, axis=0, keepdims=True))
        decay = jnp.exp(log_d - m_t)
        inter_scale = jnp.exp(inter - m_t)

        qb = pad_rows(qk[:, dk]).astype(BF16)
        kb = pad_rows(qk[:, LSTM_HEADS * LSTM_DK + h * LSTM_DK:LSTM_HEADS * LSTM_DK + (h + 1) * LSTM_DK]
                      * (LSTM_DK ** -0.5)).astype(BF16)
        v_t = pad_rows(v_ref[:, dv]).T
        c_prev = c_sc[h]
        n_prev = n_sc[h:h + 1, :]
        nt = (((1,), (1,)), ((), ()))
        wgt = lax.dot_general(kb, qb, nt, preferred_element_type=F32) * decay
        cq = lax.dot_general(c_prev.astype(BF16), qb, nt, preferred_element_type=F32)
        nq = lax.dot_general(jnp.broadcast_to(n_prev, (SUBLANES, LSTM_DK)).astype(BF16), qb, nt,
                             preferred_element_type=F32)[:1, :]
        yield
        num = jnp.dot(v_t.astype(BF16), wgt.astype(BF16), preferred_element_type=F32) + inter_scale * cq
        den = jnp.sum(wgt, axis=0, keepdims=True) + inter_scale * nq
        hid = num / jnp.maximum(jnp.abs(den), jnp.exp(-m_t))
        yield

        m_new = m_t[:, L - 1:L]
        b_last = b_row[:, L - 1:L]
        carry = jnp.exp(b_last + m_prev - m_new)
        d_end = jnp.exp(b_last - b_row + ig_row - m_new)
        if L < LP:
            d_end = jnp.where(lane_t < L, d_end, 0.0)
        upd = jnp.dot((v_t * d_end).astype(BF16), kb, preferred_element_type=F32)
        nk = jnp.dot(jnp.broadcast_to(d_end, (SUBLANES, LP)).astype(BF16), kb,
                     preferred_element_type=F32)[:1, :]
        c_sc[h] = carry * c_prev + upd
        n_sc[h:h + 1, :] = carry * n_prev + nk
        m_sc[:, h:h + 1] = m_new

        g_col = jnp.concatenate([glstm_ref[dv, :]] * (LP // LANES), axis=1)
        hn = hid * lax.rsqrt(jnp.mean(hid * hid, axis=0, keepdims=True) + EPS) * g_col
        h_out[:, dv] = (hn.T[:L, :] * jax.nn.sigmoid(o_ref[:, dv])).astype(h_out.dtype)
        yield

    for _ in zip(*[head(h) for h in range(LSTM_HEADS)]):
        yield

    @last_chunk
    def _finish():
        conv_out[...] = tail
        c_out[...] = c_sc[...]
        n_out[...] = n_sc[...]
        m_out[...] = m_sc[...]

    yield


def _mlstm(qk_raw, v_raw, o_raw, misc, conv0, c0, n0, m0, layer, w, L, G):
    b = conv0.shape[1]
    t = qk_raw.shape[0]
    nc = t // (b * L)
    assert nc * b * L == t and b % G == 0 and (G == 1 or nc == 1)
    hist = CONV_W - 1
    tok = lambda cdim: pl.BlockSpec((G * L, cdim), lambda bi, ci: (bi * nc + ci, 0))
    m0 = m0.reshape(m0.shape[0], b, 1, LSTM_HEADS)
    state0 = lambda *dims: pl.BlockSpec((None, G) + dims, lambda bi, ci: (layer, bi) + (0,) * len(dims))
    weights = [w["w_conv"], w["b_conv"], w["bgate"], w["g_lstm_out"]]
    out_shape = (
        jax.ShapeDtypeStruct((t, LSTM_WIDTH), BF16),
        jax.ShapeDtypeStruct((b, hist, LSTM_QK), F32),
        jax.ShapeDtypeStruct((b, LSTM_HEADS, LSTM_DV, LSTM_DK), F32),
        jax.ShapeDtypeStruct((b, LSTM_HEADS, LSTM_DK), F32),
        jax.ShapeDtypeStruct((b, 1, LSTM_HEADS), F32),
    )
    state = lambda *dims: pl.BlockSpec((G,) + dims, lambda bi, ci: (bi,) + (0,) * len(dims))
    h, conv_new, c_new, n_new, m_new = pl.pallas_call(
        functools.partial(_mlstm_kernel, L=L, G=G, nc=nc),
        grid=(b // G, nc),
        in_specs=[tok(LSTM_QK), tok(LSTM_WIDTH), tok(LSTM_WIDTH), tok(LANES),
                  state0(hist, LSTM_QK), state0(LSTM_HEADS, LSTM_DV, LSTM_DK), state0(LSTM_HEADS, LSTM_DK),
                  state0(1, LSTM_HEADS)] + [_full_spec(a) for a in weights],
        out_specs=(tok(LSTM_WIDTH), state(hist, LSTM_QK), state(LSTM_HEADS, LSTM_DV, LSTM_DK),
                   state(LSTM_HEADS, LSTM_DK), state(1, LSTM_HEADS)),
        out_shape=out_shape,
        scratch_shapes=[pltpu.VMEM((G, CONV_PAD + L, LSTM_QK), F32),
                        pltpu.VMEM((G, LSTM_HEADS, LSTM_DV, LSTM_DK), F32),
                        pltpu.VMEM((G, LSTM_HEADS, LSTM_DK), F32),
                        pltpu.VMEM((G, 1, LSTM_HEADS), F32)],
        compiler_params=pltpu.CompilerParams(dimension_semantics=("parallel", "arbitrary"),
                                             vmem_limit_bytes=VMEM_LIMIT),
        name="mlstm",
    )(qk_raw, v_raw, o_raw, misc, conv0, c0, n0, m0, *weights)
    return h, conv_new, c_new, n_new, m_new.reshape(b, LSTM_HEADS)


FF_CHUNK = 1024


def _out_ffn_kernel(x_ref, attn_ref, lstm_ref, w_out_ref, g_ffn_ref, w_up_ref, w_down_ref, y_ref):
    mix = jnp.concatenate([attn_ref[...], lstm_ref[...]], axis=1)
    x1 = x_ref[...] + jnp.dot(mix, w_out_ref[...], preferred_element_type=F32)
    xb = _rms(x1, g_ffn_ref[...]).astype(BF16)
    d_ff = w_up_ref.shape[1]
    acc = x1
    for f in range(d_ff // FF_CHUNK):
        cols = slice(f * FF_CHUNK, (f + 1) * FF_CHUNK)
        u = jnp.maximum(jnp.dot(xb, w_up_ref[:, cols], preferred_element_type=F32), 0.0)
        acc = acc + jnp.dot((u * u).astype(BF16), w_down_ref[cols, :], preferred_element_type=F32)
    y_ref[...] = acc


def _out_ffn(x2, attn, lstm, ffn_weights, layer, tm):
    t, d = x2.shape
    assert t % tm == 0
    row = lambda c: pl.BlockSpec((tm, c), lambda i: (i, 0))
    const = lambda a: pl.BlockSpec((None,) + a.shape[1:], lambda i: (layer, 0, 0), pipeline_mode=pl.Buffered(1))
    weights = list(ffn_weights)
    return pl.pallas_call(
        _out_ffn_kernel,
        grid=(t // tm,),
        in_specs=[row(d), row(MLA_WIDTH), row(LSTM_WIDTH)] + [const(a) for a in weights],
        out_specs=row(d),
        out_shape=jax.ShapeDtypeStruct((t, d), F32),
        compiler_params=pltpu.CompilerParams(dimension_semantics=("parallel",),
                                             vmem_limit_bytes=VMEM_LIMIT),
        name="out_ffn",
    )(x2, attn, lstm, *weights)


def _pack_layer(l, g_mix, w_in, g_q_lat, w_uq, g_q_nope, g_q_rope, g_kv_lat, g_k_rope, w_uk, g_k_nope, w_uv,
                w_conv, b_conv, b_igate, b_fgate, g_lstm_out):
    d = w_in.shape[1]
    wi = w_in[l]
    kpe = wi[:, W_IN_LAT[1]:W_IN_LAT[1] + ROPE_DIM]
    gates = wi[:, W_IN_LSTM[1]:W_IN_LSTM[1] + 2 * LSTM_HEADS]
    misc = jnp.concatenate([kpe, _rot_half(kpe), gates,
                            jnp.zeros((d, LANES - MISC_FG - LSTM_HEADS), F32)], axis=1)

    uq = w_uq[l].reshape(Q_LORA, MLA_HEADS, QK_DIM)
    uq = jnp.concatenate([uq, _rot_half(uq[..., NOPE_DIM:])], axis=-1).reshape(Q_LORA, MLA_HEADS * LANES)
    gq_head = jnp.concatenate([g_q_nope[l], g_q_rope[l], _rot_half(g_q_rope[l])])
    uk = w_uk[l].reshape(KV_LORA, MLA_HEADS, NOPE_DIM)
    uk = jnp.concatenate([uk, jnp.zeros_like(uk)], axis=-1).reshape(KV_LORA, MLA_HEADS * LANES)
    gkn_head = jnp.concatenate([g_k_nope[l], jnp.zeros((LANES - NOPE_DIM,), F32)])
    uv = w_uv[l].reshape(KV_LORA, MLA_HEADS, V_DIM)
    uv_t = jnp.concatenate([uv, jnp.zeros((KV_LORA, MLA_HEADS, VT_ROWS - V_DIM), F32)], axis=-1)
    uv_t = uv_t.reshape(KV_LORA, MLA_HEADS * VT_ROWS).T
    vones_head = jnp.zeros((VT_ROWS,), F32).at[V_DIM].set(1.0)
    gkm = jnp.concatenate([g_k_rope[l], _rot_half(g_k_rope[l]), jnp.zeros((LANES - 2 * ROPE_DIM,), F32)])
    bgate = jnp.concatenate([jnp.zeros((MISC_IG,), F32), b_igate[l], b_fgate[l],
                             jnp.zeros((LANES - MISC_FG - LSTM_HEADS,), F32)])

    i = jnp.arange(HEAD_PAIR)
    same = (i[:, None] // LANES) == (i[None, :] // LANES)
    li, lj = i[:, None] % LANES, i[None, :] % LANES
    mseg = jnp.where(same & (li < NOPE_DIM) & (lj < NOPE_DIM), 1.0 / NOPE_DIM,
                     jnp.where(same & (li >= NOPE_DIM) & (li < QK_DIM) & (lj >= NOPE_DIM), 1.0 / ROPE_DIM, 0.0))
    a = jnp.arange(LANES)
    mmisc = jnp.where((a[:, None] < ROPE_DIM) & (a[None, :] < 2 * ROPE_DIM), 1.0 / ROPE_DIM, 0.0)

    row = lambda v: v.reshape(1, -1).astype(F32)
    return {
        "g_mix": row(g_mix[l]), "w_lat": wi[:, W_IN_LAT[0]:W_IN_LAT[1]].astype(BF16),
        "w_lstm": wi[:, W_IN_LSTM[0]:W_IN_LSTM[1]].astype(BF16), "w_misc": misc.astype(BF16),
        "g_q_lat": row(g_q_lat[l]), "w_uq": uq.astype(BF16),
        "gq": row(jnp.tile(gq_head, MLA_HEADS)), "g_kv_lat": row(g_kv_lat[l]), "gkm": row(gkm),
        "w_uk": uk.astype(BF16), "gkn": row(jnp.tile(gkn_head, MLA_HEADS)), "w_uv": w_uv[l].astype(BF16),
        "w_uv_t": uv_t.astype(BF16), "vones": jnp.tile(vones_head, MLA_HEADS).reshape(-1, 1),
        "mseg": mseg.astype(BF16), "mmisc": mmisc.astype(BF16),
        "w_conv": w_conv[l], "b_conv": row(b_conv[l]), "bgate": row(bgate), "g_lstm_out": jnp.broadcast_to(g_lstm_out[l][:, None], (LSTM_WIDTH, LANES)),
    }


def _rope_table(first_pos, n):
    half = ROPE_DIM // 2
    blk = min(n, LANES)
    assert n % blk == 0
    lane = jnp.arange(LANES, dtype=jnp.int32)
    inv = ROPE_BASE ** (-(lane % half).astype(F32) / half)
    ang_a = (first_pos + blk * jnp.arange(n // blk, dtype=jnp.int32)).astype(F32)[:, None] * inv[None, :]
    ang_b = jnp.arange(blk, dtype=jnp.int32).astype(F32)[:, None] * inv[None, :]
    ca, sa = jnp.cos(ang_a)[:, None, :], jnp.sin(ang_a)[:, None, :]
    cb, sb = jnp.cos(ang_b)[None, :, :], jnp.sin(ang_b)[None, :, :]
    cos = ca * cb - sa * sb
    sin = sa * cb + ca * sb
    kind = (lane // half) % 4
    tab = jnp.where(kind < 2, cos, jnp.where(kind == 2, -sin, sin))
    return tab.reshape(n, LANES)


def kernel(x_prompt, x_sample, cache_kv_latent, cache_k_rope, state_conv, state_C, state_n, state_m,
           g_mix, w_in, g_q_lat, w_uq, g_q_nope, g_q_rope, g_kv_lat, g_k_rope, w_uk, g_k_nope, w_uv,
           w_conv, b_conv, b_igate, b_fgate, g_lstm_out, w_out, g_ffn, w_up, w_down):
    depth = w_in.shape[0]
    bp, sp, d = x_prompt.shape
    bs, ls, _ = x_sample.shape
    past = cache_kv_latent.shape[2]
    hist = CONV_W - 1

    tabs_p = jnp.tile(_rope_table(0, sp), (bp, 1))
    tabs_s = jnp.tile(_rope_table(past, ls), (bs, 1))
    cache_kr_t = jnp.swapaxes(cache_k_rope, 2, 3)
    xp = x_prompt.reshape(bp * sp, d)
    xs = x_sample.reshape(bs * ls, d)
    zero_conv = jnp.zeros((1, bp, hist, LSTM_QK), F32)
    zero_c = jnp.zeros((1, bp, LSTM_HEADS, LSTM_DV, LSTM_DK), F32)
    zero_n = jnp.zeros((1, bp, LSTM_HEADS, LSTM_DK), F32)
    zero_m = jnp.zeros((1, bp, LSTM_HEADS), F32)
    ffn_weights = (w_out.astype(BF16), g_ffn.reshape(depth, 1, d), w_up.astype(BF16), w_down.astype(BF16))

    outs = {k: [] for k in ("p_lat", "p_kr", "p_conv", "p_c", "p_n", "p_m",
                            "s_lat", "s_kr", "s_conv", "s_c", "s_n", "s_m")}
    for l in range(depth):
        w = _pack_layer(l, g_mix, w_in, g_q_lat, w_uq, g_q_nope, g_q_rope, g_kv_lat, g_k_rope, w_uk, g_k_nope,
                        w_uv, w_conv, b_conv, b_igate, b_fgate, g_lstm_out)
        q, k, v, lat, kr, qk_raw, v_raw, o_raw, misc = _projection(xp, tabs_p, w, tm=512)
        attn = _attention_prompt(q.reshape(bp, sp, -1), k.reshape(bp, sp, -1), v,
                                 tq=ATTN_TQ, tk=ATTN_TK)
        h, conv_new, c_new, n_new, m_new = _mlstm(qk_raw, v_raw, o_raw, misc, zero_conv, zero_c, zero_n, zero_m,
                                                  0, w, L=MLSTM_TILE, G=1)
        xp = _out_ffn(xp, attn.reshape(bp * sp, -1), h, ffn_weights, l, tm=512)
        outs["p_lat"].append(lat.reshape(bp, sp, KV_LORA))
        outs["p_kr"].append(kr.reshape(ROPE_DIM, bp, sp))
        outs["p_conv"].append(conv_new)
        outs["p_c"].append(c_new)
        outs["p_n"].append(n_new)
        outs["p_m"].append(m_new)
        q, k, v, lat, kr, qk_raw, v_raw, o_raw, misc = _projection(xs, tabs_s, w, tm=bs * ls)
        attn = _attention_sample(q, k, lat, cache_kv_latent, cache_kr_t, l, w, n_new=ls, G=ATTN_SAMPLE_GROUP)
        h, conv_new, c_new, n_new, m_new = _mlstm(qk_raw, v_raw, o_raw, misc, state_conv, state_C,
                                                  state_n, state_m, l, w, L=ls, G=MLSTM_SAMPLE_GROUP)
        xs = _out_ffn(xs, attn, h, ffn_weights, l, tm=bs * ls)
        outs["s_lat"].append(lat.reshape(bs, ls, KV_LORA))
        outs["s_kr"].append(kr.reshape(ROPE_DIM, bs, ls))
        outs["s_conv"].append(conv_new)
        outs["s_c"].append(c_new)
        outs["s_n"].append(n_new)
        outs["s_m"].append(m_new)

    st = lambda key: jnp.stack(outs[key])
    kr_st = lambda key: jnp.transpose(jnp.stack(outs[key]), (0, 2, 3, 1))
    return (xp.reshape(bp, sp, d), xs.reshape(bs, ls, d),
            st("p_lat"), kr_st("p_kr"), st("p_conv"), st("p_c"), st("p_n"), st("p_m"),
            st("s_lat"), kr_st("s_kr"), st("s_conv"), st("s_c"), st("s_n"), st("s_m"))
```

```python
import functools

import jax
import jax.numpy as jnp
from jax import lax
from jax.experimental import pallas as pl
from jax.experimental.pallas import tpu as pltpu

F32 = jnp.float32
BF16 = jnp.bfloat16

EPS = 1e-6
CHUNK = 64
MLA_HEADS = 8
NOPE_DIM = 64
ROPE_DIM = 32
QK_DIM = NOPE_DIM + ROPE_DIM
V_DIM = 64
Q_LORA = 256
KV_LORA = 128
ROPE_BASE = 10000.0
LSTM_HEADS = 4
LSTM_DK = 128
LSTM_DV = 128
CONV_W = 4
LSTM_WIDTH = LSTM_HEADS * LSTM_DV
LSTM_QK = 2 * LSTM_HEADS * LSTM_DK
MLA_WIDTH = MLA_HEADS * V_DIM

LANES = 128
SUBLANES = 8
HEAD_PAIR = 2 * LANES
VMEM_LIMIT = 52 * 1024 * 1024
Q_LOG2_SCALE = (QK_DIM ** -0.5) * 1.4426950408889634
ATTN_TQ = 512
ATTN_TK = 512
PAIRS_PER_TRIP = 4
MLSTM_TILE = 256
ATTN_SAMPLE_GROUP = 4
MLSTM_SAMPLE_GROUP = 4
VT_ROWS = 80

MISC_KPE = 0
MISC_KPE_ROT = ROPE_DIM
MISC_IG = 2 * ROPE_DIM
MISC_FG = MISC_IG + LSTM_HEADS

COL_CQ = ("lat", 0, Q_LORA)
COL_CKV = ("lat", Q_LORA, Q_LORA + KV_LORA)
COL_QK = ("lstm", 0, LSTM_QK)
COL_V = ("lstm", LSTM_QK, LSTM_QK + LSTM_WIDTH)
COL_O = ("lstm", LSTM_QK + LSTM_WIDTH, LSTM_QK + 2 * LSTM_WIDTH)
COL_MISC = ("misc", 0, LANES)
W_IN_LAT = (0, Q_LORA + KV_LORA)
W_IN_LSTM = (Q_LORA + KV_LORA + ROPE_DIM, Q_LORA + KV_LORA + ROPE_DIM + LSTM_QK + 2 * LSTM_WIDTH)


def _rot_half(a):
    half = ROPE_DIM // 2
    return jnp.concatenate([a[..., half:], a[..., :half]], axis=-1)


def _rms(x, g):
    return x * lax.rsqrt(jnp.mean(x * x, axis=-1, keepdims=True) + EPS) * g


def _in_turn(*stage_generators):
    for _ in zip(*stage_generators):
        pass


def _segment_mean_sq(y, m_ref):
    return jnp.dot((y * y).astype(BF16), m_ref[...], preferred_element_type=F32)


def _proj_kernel(x_ref, rope_ref, g_mix_ref, w_lat_ref, w_lstm_ref, w_misc_ref, g_qlat_ref, w_uq_ref,
                 gq_ref, g_kvlat_ref, gkm_ref, w_uk_ref, gkn_ref, w_uv_ref, vones_ref, mseg_ref, mmisc_ref,
                 q_out, k_out, v_out, lat_out, kr_out, qk_out, vl_out, o_out, misc_out):
    x = x_ref[...]
    xb = _rms(x, g_mix_ref[...]).astype(BF16)

    blocks = {"lat": w_lat_ref, "lstm": w_lstm_ref, "misc": w_misc_ref}

    def proj(col):
        return jnp.dot(xb, blocks[col[0]][:, col[1]:col[2]], preferred_element_type=F32)

    c_q = proj(COL_CQ)
    c_kv = proj(COL_CKV)
    misc = proj(COL_MISC)
    qk_out[...] = proj(COL_QK)
    misc_out[...] = misc

    tab = rope_ref[...]
    tab_next = pltpu.roll(tab, LANES - ROPE_DIM, 1)
    lane = lax.broadcasted_iota(jnp.int32, tab.shape, 1)
    in_rope = (lane >= NOPE_DIM) & (lane < QK_DIM)
    cq_tab = jnp.where(lane < NOPE_DIM, 1.0, jnp.where(in_rope, tab, 0.0))
    sq_tab = jnp.where(in_rope, tab_next, 0.0)
    ck_tab = jnp.where(lane < ROPE_DIM, tab, 0.0)
    sk_tab = jnp.where(lane < ROPE_DIM, tab_next, 0.0)

    cqn = _rms(c_q, g_qlat_ref[...]).astype(BF16)
    lat = _rms(c_kv, g_kvlat_ref[...])
    lat_out[...] = lat
    latb = lat.astype(BF16)
    lat_t = lat.T.astype(BF16)
    pairs = range(MLA_HEADS // 2)
    cols = [slice(p * HEAD_PAIR, (p + 1) * HEAD_PAIR) for p in pairs]

    q_raw = [jnp.dot(cqn, w_uq_ref[:, cols[p]], preferred_element_type=F32) for p in pairs]
    misc_mean = _segment_mean_sq(misc, mmisc_ref)
    vl_out[...] = proj(COL_V)
    q_mean = [_segment_mean_sq(q_raw[p], mseg_ref) for p in pairs]
    k_raw = [jnp.dot(latb, w_uk_ref[:, cols[p]], preferred_element_type=F32) for p in pairs]
    o_out[...] = proj(COL_O)
    k_mean = [_segment_mean_sq(k_raw[p], mseg_ref) for p in pairs]
    v_out[...] = (jnp.dot(w_uv_ref[...], lat_t, preferred_element_type=F32) + vones_ref[...]).astype(BF16)

    for p in pairs:
        y = q_raw[p] * lax.rsqrt(q_mean[p] + EPS) * gq_ref[:, cols[p]]
        for j in range(2):
            yh = y[:, j * LANES:(j + 1) * LANES]
            qh = yh * cq_tab + pltpu.roll(yh, LANES - ROPE_DIM, 1) * sq_tab
            h = 2 * p + j
            q_out[:, h * LANES:(h + 1) * LANES] = (qh * Q_LOG2_SCALE).astype(BF16)

    ms = misc * lax.rsqrt(misc_mean + EPS) * gkm_ref[...]
    kr = ms * ck_tab + pltpu.roll(ms, LANES - ROPE_DIM, 1) * sk_tab
    kr_out[...] = kr.T[:ROPE_DIM, :]
    kr_placed = pltpu.roll(kr, NOPE_DIM, 1)
    kr_pair = jnp.concatenate([kr_placed, kr_placed], axis=1)
    for p in pairs:
        y = k_raw[p] * lax.rsqrt(k_mean[p] + EPS) * gkn_ref[:, cols[p]]
        k_out[:, cols[p]] = (y + kr_pair).astype(BF16)


def _full_spec(a):
    nd = a.ndim
    return pl.BlockSpec(a.shape, lambda *_: (0,) * nd)


def _projection(x2, rope_tab, w, tm):
    t = x2.shape[0]
    assert t % tm == 0
    row = lambda c: pl.BlockSpec((tm, c), lambda i: (i, 0))
    weights = [w["g_mix"], w["w_lat"], w["w_lstm"], w["w_misc"], w["g_q_lat"], w["w_uq"], w["gq"], w["g_kv_lat"],
               w["gkm"], w["w_uk"], w["gkn"], w["w_uv_t"], w["vones"], w["mseg"], w["mmisc"]]
    out_shape = (
        jax.ShapeDtypeStruct((t, MLA_HEADS * LANES), BF16),
        jax.ShapeDtypeStruct((t, MLA_HEADS * LANES), BF16),
        jax.ShapeDtypeStruct((MLA_HEADS * VT_ROWS, t), BF16),
        jax.ShapeDtypeStruct((t, KV_LORA), F32),
        jax.ShapeDtypeStruct((ROPE_DIM, t), F32),
        jax.ShapeDtypeStruct((t, LSTM_QK), F32),
        jax.ShapeDtypeStruct((t, LSTM_WIDTH), F32),
        jax.ShapeDtypeStruct((t, LSTM_WIDTH), F32),
        jax.ShapeDtypeStruct((t, LANES), F32),
    )
    return pl.pallas_call(
        _proj_kernel,
        grid=(t // tm,),
        in_specs=[row(x2.shape[1]), row(LANES)] + [_full_spec(a) for a in weights],
        out_specs=tuple(pl.BlockSpec((s.shape[0], tm), lambda i: (0, i)) if n in (2, 4) else row(s.shape[1])
                        for n, s in enumerate(out_shape)),
        out_shape=out_shape,
        compiler_params=pltpu.CompilerParams(dimension_semantics=("parallel",),
                                             vmem_limit_bytes=VMEM_LIMIT),
        name="projection",
    )(x2, rope_tab, *weights)


def _attn_prompt_kernel(q_ref, qnext_ref, k_ref, vt_ref, o_ref, qt_sc, m_sc, acc_sc, sa_sc, sb_sc, *, tq, tk):
    qi = pl.program_id(2)
    m_sc[...] = jnp.full(m_sc.shape, -1e30, F32)
    acc_sc[...] = jnp.zeros(acc_sc.shape, F32)
    q_col0 = qi * tq
    for j in range(2):
        qt_sc[j] = q_ref[:, j * LANES:(j + 1) * LANES].astype(F32).T.astype(BF16)

    def scores_head(j, k0, s_ref):
        k = k_ref[pl.ds(k0, tk), j * LANES:(j + 1) * LANES]
        s_ref[j] = jnp.dot(k, qt_sc[j], preferred_element_type=F32)

    def scores(k0, s_ref):
        for j in range(2):
            scores_head(j, k0, s_ref)

    def next_tile_scores(j):
        qt = qnext_ref[:, j * LANES:(j + 1) * LANES].astype(F32).T.astype(BF16)
        sa_sc[j] = jnp.dot(k_ref[pl.ds(0, tk), j * LANES:(j + 1) * LANES], qt, preferred_element_type=F32)

    def consume_head(j, k0, s_ref, masked):
        vt = vt_ref[j * VT_ROWS:(j + 1) * VT_ROWS, pl.ds(k0, tk)]
        s = s_ref[j]
        if masked:
            kc = (k0 + lax.broadcasted_iota(jnp.int32, (tk, tq), 0)) // CHUNK
            qc = (q_col0 + lax.broadcasted_iota(jnp.int32, (tk, tq), 1)) // CHUNK
            s = jnp.where(kc <= qc, s, -1e30)
        m_prev = m_sc[j]
        m_next = jnp.maximum(m_prev, jnp.max(s, axis=0, keepdims=True))
        alpha = jnp.exp2(m_prev - m_next)
        p = jnp.exp2(s - m_next)
        m_sc[j] = m_next
        acc_sc[j] = alpha * acc_sc[j] + jnp.dot(vt, p.astype(BF16), preferred_element_type=F32)

    def consume(k0, s_ref, masked):
        for j in range(2):
            consume_head(j, k0, s_ref, masked)

    n_full = q_col0 // tk
    at = lambda t: pl.multiple_of(t * tk, tk)

    @pl.when(qi == 0)
    def _first_tile_of_pair():
        scores(0, sa_sc)

    def pair(t):
        scores(at(t + 1), sb_sc)
        for j in range(2):
            consume_head(j, at(t), sa_sc, False)
            scores_head(j, at(t + 2), sa_sc)
        consume(at(t + 1), sb_sc, False)

    def body(i, carry):
        for u in range(PAIRS_PER_TRIP):
            pair(2 * PAIRS_PER_TRIP * i + 2 * u)
        return carry

    trips = n_full // (2 * PAIRS_PER_TRIP)
    lax.fori_loop(0, trips, body, 0)

    def rest(i, carry):
        pair(2 * PAIRS_PER_TRIP * trips + 2 * i)
        return carry

    lax.fori_loop(0, (n_full % (2 * PAIRS_PER_TRIP)) // 2, rest, 0)

    t0 = 2 * (n_full // 2)

    @pl.when(n_full % 2 == 1)
    def _odd_tail():
        scores(at(t0 + 1), sb_sc)
        for j in range(2):
            consume_head(j, at(t0), sa_sc, False)
            next_tile_scores(j)
        consume(at(t0 + 1), sb_sc, True)

    @pl.when(n_full % 2 == 0)
    def _even_tail():
        for j in range(2):
            consume_head(j, at(t0), sa_sc, True)
            next_tile_scores(j)

    outs = []
    for j in range(2):
        acc = acc_sc[j]
        outs.append(acc[:V_DIM, :] / acc[V_DIM:V_DIM + 1, :])
    o_ref[...] = jnp.concatenate(outs, axis=0).T.astype(o_ref.dtype)


def _attention_prompt(q, k, vt, tq, tk):
    b, s, _ = q.shape
    assert tq == tk and s % tq == 0 and tq % CHUNK == 0
    pairs = MLA_HEADS // 2
    return pl.pallas_call(
        functools.partial(_attn_prompt_kernel, tq=tq, tk=tk),
        grid=(b, pairs, s // tq),
        in_specs=[
            pl.BlockSpec((None, tq, HEAD_PAIR), lambda bi, p, i: (bi, i, p)),
            pl.BlockSpec((None, tq, HEAD_PAIR), lambda bi, p, i: (bi, jnp.minimum(i + 1, s // tq - 1), p)),
            pl.BlockSpec((None, s, HEAD_PAIR), lambda bi, p, i: (bi, 0, p)),
            pl.BlockSpec((2 * VT_ROWS, s), lambda bi, p, i: (p, bi)),
        ],
        out_specs=pl.BlockSpec((None, tq, LANES), lambda bi, p, i: (bi, i, p)),
        out_shape=jax.ShapeDtypeStruct((b, s, MLA_WIDTH), BF16),
        scratch_shapes=[pltpu.VMEM((2, LANES, tq), BF16),
                        pltpu.VMEM((2, 1, tq), F32),
                        pltpu.VMEM((2, VT_ROWS, tq), F32),
                        pltpu.VMEM((2, tk, tq), F32),
                        pltpu.VMEM((2, tk, tq), F32)],
        compiler_params=pltpu.CompilerParams(
            dimension_semantics=("parallel", "parallel", "arbitrary"),
            vmem_limit_bytes=VMEM_LIMIT),
        name="attention_prompt",
    )(q, q, k, vt)


def _attn_sample_kernel(q_ref, kn_ref, latn_ref, lat_ref, kr_ref, w_uk_ref, gkn_ref, w_uv_ref, mseg_ref,
                        o_ref, k_sc, *, n_new, G):
    def sequence(g):
        tok = lambda ref: ref.at[g * n_new:(g + 1) * n_new]
        return _attn_sample_sequence(tok(q_ref), tok(kn_ref), tok(latn_ref), lat_ref.at[g], kr_ref.at[g],
                                     w_uk_ref, gkn_ref, w_uv_ref, mseg_ref, tok(o_ref), k_sc.at[g], n_new=n_new)

    _in_turn(*[sequence(g) for g in range(G)])


def _attn_sample_sequence(q_ref, kn_ref, latn_ref, lat_ref, kr_ref, w_uk_ref, gkn_ref, w_uv_ref, mseg_ref,
                          o_ref, k_sc, *, n_new):
    latb = lat_ref[...].astype(BF16)
    kr_t = kr_ref[...]
    zeros = lambda r: jnp.zeros((r, kr_t.shape[1]), F32)
    kr_placed = jnp.concatenate([zeros(NOPE_DIM), kr_t, zeros(LANES - QK_DIM)], axis=0).T
    kr_pair = jnp.concatenate([kr_placed, kr_placed], axis=1)
    pairs = range(MLA_HEADS // 2)
    cols = [slice(p * HEAD_PAIR, (p + 1) * HEAD_PAIR) for p in pairs]
    raw = [jnp.dot(latb, w_uk_ref[:, cols[p]], preferred_element_type=F32) for p in pairs]
    yield
    mean_sq = [_segment_mean_sq(raw[p], mseg_ref) for p in pairs]
    yield
    for p in pairs:
        y = raw[p] * lax.rsqrt(mean_sq[p] + EPS) * gkn_ref[:, cols[p]]
        k_sc[:, cols[p]] = (y + kr_pair).astype(BF16)

    q = q_ref[...]
    qt = jnp.concatenate([q] * MLA_HEADS, axis=0)
    r_head = lax.broadcasted_iota(jnp.int32, qt.shape, 0) // n_new
    c_head = lax.broadcasted_iota(jnp.int32, qt.shape, 1) // LANES
    qm = jnp.where(r_head == c_head, qt, jnp.zeros_like(qt))

    nt = (((1,), (1,)), ((), ()))
    s_old = lax.dot_general(k_sc[...], qm, nt, preferred_element_type=F32)
    s_new = lax.dot_general(kn_ref[...], qm, nt, preferred_element_type=F32)
    yield
    mx = jnp.maximum(jnp.max(s_old, axis=0, keepdims=True), jnp.max(s_new, axis=0, keepdims=True))
    p_old = jnp.exp2(s_old - mx)
    p_new = jnp.exp2(s_new - mx)
    inv = 1.0 / (jnp.sum(p_old, axis=0, keepdims=True) + jnp.sum(p_new, axis=0, keepdims=True))
    p_old = (p_old * inv).astype(BF16)
    p_new = (p_new * inv).astype(BF16)
    tn = (((0,), (0,)), ((), ()))
    ctx = (lax.dot_general(p_old, latb, tn, preferred_element_type=F32)
           + lax.dot_general(p_new, latn_ref[...].astype(BF16), tn, preferred_element_type=F32))
    yield
    full = jnp.dot(ctx.astype(BF16), w_uv_ref[...], preferred_element_type=F32)
    v_head = lax.broadcasted_iota(jnp.int32, (n_new, MLA_WIDTH), 1) // V_DIM
    out = jnp.zeros((n_new, MLA_WIDTH), F32)
    for h in range(MLA_HEADS):
        out = out + jnp.where(v_head == h, full[h * n_new:(h + 1) * n_new, :], 0.0)
    o_ref[...] = out.astype(o_ref.dtype)
    yield


def _attention_sample(q, k_new, lat_new, cache_lat, cache_kr, layer, w, n_new, G):
    _, b, past, _ = cache_lat.shape
    assert b % G == 0
    weights = [w["w_uk"], w["gkn"], w["w_uv"], w["mseg"]]
    tok = lambda c: pl.BlockSpec((G * n_new, c), lambda i: (i, 0))
    return pl.pallas_call(
        functools.partial(_attn_sample_kernel, n_new=n_new, G=G),
        grid=(b // G,),
        in_specs=[tok(MLA_HEADS * LANES), tok(MLA_HEADS * LANES), tok(KV_LORA),
                  pl.BlockSpec((None, G, past, KV_LORA), lambda i: (layer, i, 0, 0)),
                  pl.BlockSpec((None, G, ROPE_DIM, past), lambda i: (layer, i, 0, 0))]
                 + [_full_spec(a) for a in weights],
        out_specs=tok(MLA_WIDTH),
        out_shape=jax.ShapeDtypeStruct((b * n_new, MLA_WIDTH), BF16),
        scratch_shapes=[pltpu.VMEM((G, past, MLA_HEADS * LANES), BF16)],
        compiler_params=pltpu.CompilerParams(dimension_semantics=("parallel",),
                                             vmem_limit_bytes=VMEM_LIMIT),
        name="attention_sample",
    )(q, k_new, lat_new, cache_lat, cache_kr, *weights)


CONV_PAD = SUBLANES


def _mlstm_kernel(qk_ref, v_ref, o_ref, misc_ref, conv0_ref, c0_ref, n0_ref, m0_ref, wconv_ref, bconv_ref,
                  bgate_ref, glstm_ref,
                  h_out, conv_out, c_out, n_out, m_out,
                  full_sc, c_sc, n_sc, m_sc, *, L, G, nc):
    def sequence(g):
        tok = lambda ref: ref.at[g * L:(g + 1) * L]
        return _mlstm_sequence(tok(qk_ref), tok(v_ref), tok(o_ref), tok(misc_ref), conv0_ref.at[g], c0_ref.at[g],
                               n0_ref.at[g], m0_ref.at[g], wconv_ref, bconv_ref, bgate_ref, glstm_ref,
                               tok(h_out), conv_out.at[g], c_out.at[g], n_out.at[g], m_out.at[g],
                               full_sc.at[g], c_sc.at[g], n_sc.at[g], m_sc.at[g], L=L, nc=nc)

    _in_turn(*[sequence(g) for g in range(G)])


def _mlstm_sequence(qk_ref, v_ref, o_ref, misc_ref, conv0_ref, c0_ref, n0_ref, m0_ref, wconv_ref, bconv_ref,
                    bgate_ref, glstm_ref,
                    h_out, conv_out, c_out, n_out, m_out,
                    full_sc, c_sc, n_sc, m_sc, *, L, nc):
    hist = CONV_W - 1
    lo = CONV_PAD - hist
    first_chunk = (lambda f: f()) if nc == 1 else pl.when(pl.program_id(1) == 0)
    last_chunk = (lambda f: f()) if nc == 1 else pl.when(pl.program_id(1) == nc - 1)

    @first_chunk
    def _init():
        full_sc[lo:CONV_PAD, :] = conv0_ref[...]
        c_sc[...] = c0_ref[...]
        n_sc[...] = n0_ref[...]
        m_sc[...] = m0_ref[...]

    full_sc[CONV_PAD:CONV_PAD + L, :] = qk_ref[...]
    ext = full_sc[...]
    y = bconv_ref[...] + ext[CONV_PAD:, :] * wconv_ref[CONV_W - 1:CONV_W, :]
    for r in range(1, CONV_W):
        y = y + pltpu.roll(ext, r, 0)[CONV_PAD:, :] * wconv_ref[CONV_W - 1 - r:CONV_W - r, :]
    qk = y * jax.nn.sigmoid(y)
    tail = full_sc[lo + L:CONV_PAD + L, :]
    full_sc[lo:CONV_PAD, :] = tail
    yield

    LP = max(L, LANES)
    pad_rows = lambda a: a if L == LP else jnp.concatenate([a, jnp.zeros((LP - L, a.shape[1]), a.dtype)], axis=0)
    gs_t = pad_rows(misc_ref[...] + bgate_ref[...]).T
    g8 = gs_t[MISC_IG:MISC_IG + 2 * LSTM_HEADS, :]
    lf8 = jnp.minimum(g8, 0.0) - jnp.log1p(jnp.exp(-jnp.abs(g8)))
    s_idx = lax.broadcasted_iota(jnp.int32, (LP, LP), 0)
    t_idx = lax.broadcasted_iota(jnp.int32, (LP, LP), 1)
    causal = s_idx <= t_idx
    triu = jnp.where(causal, 1.0, 0.0).astype(BF16)
    lf_hi = lf8.astype(BF16)
    lf_mid = (lf8 - lf_hi.astype(F32)).astype(BF16)
    lf_lo = (lf8 - lf_hi.astype(F32) - lf_mid.astype(F32)).astype(BF16)
    b8 = (jnp.dot(lf_hi, triu, preferred_element_type=F32)
          + jnp.dot(lf_mid, triu, preferred_element_type=F32)
          + jnp.dot(lf_lo, triu, preferred_element_type=F32))
    yield
    c4 = g8[:LSTM_HEADS, :] - b8[LSTM_HEADS:, :]
    c_cols = jnp.concatenate([c4, jnp.zeros((LANES - LSTM_HEADS, LP), F32)], axis=0).T
    lane_t = lax.broadcasted_iota(jnp.int32, (1, LP), 1)

    def head(h):
        dk = slice(h * LSTM_DK, (h + 1) * LSTM_DK)
        dv = slice(h * LSTM_DV, (h + 1) * LSTM_DV)
        ig_row = g8[h:h + 1, :]
        b_row = b8[LSTM_HEADS + h:LSTM_HEADS + h + 1, :]
        c_col = c_cols[:, h:h + 1]
        m_prev = m_sc[:, h:h + 1]

        log_d = jnp.where(causal, c_col + b_row, -jnp.inf)
        inter = b_row + m_prev
        m_t = jnp.maximum(inter, jnp.max(log_d, axis=0, keepdims=True))
        decay = jnp.exp(log_d - m_t)
        inter_scale = jnp.exp(inter - m_t)

        qb = pad_rows(qk[:, dk]).astype(BF16)
        kb = pad_rows(qk[:, LSTM_HEADS * LSTM_DK + h * LSTM_DK:LSTM_HEADS * LSTM_DK + (h + 1) * LSTM_DK]
                      * (LSTM_DK ** -0.5)).astype(BF16)
        v_t = pad_rows(v_ref[:, dv]).T
        c_prev = c_sc[h]
        n_prev = n_sc[h:h + 1, :]
        nt = (((1,), (1,)), ((), ()))
        wgt = lax.dot_general(kb, qb, nt, preferred_element_type=F32) * decay
        cq = lax.dot_general(c_prev.astype(BF16), qb, nt, preferred_element_type=F32)
        nq = lax.dot_general(jnp.broadcast_to(n_prev, (SUBLANES, LSTM_DK)).astype(BF16), qb, nt,
                             preferred_element_type=F32)[:1, :]
        yield
        num = jnp.dot(v_t.astype(BF16), wgt.astype(BF16), preferred_element_type=F32) + inter_scale * cq
        den = jnp.sum(wgt, axis=0, keepdims=True) + inter_scale * nq
        hid = num / jnp.maximum(jnp.abs(den), jnp.exp(-m_t))
        yield

        m_new = m_t[:, L - 1:L]
        b_last = b_row[:, L - 1:L]
        carry = jnp.exp(b_last + m_prev - m_new)
        d_end = jnp.exp(b_last - b_row + ig_row - m_new)
        if L < LP:
            d_end = jnp.where(lane_t < L, d_end, 0.0)
        upd = jnp.dot((v_t * d_end).astype(BF16), kb, preferred_element_type=F32)
        nk = jnp.dot(jnp.broadcast_to(d_end, (SUBLANES, LP)).astype(BF16), kb,
                     preferred_element_type=F32)[:1, :]
        c_sc[h] = carry * c_prev + upd
        n_sc[h:h + 1, :] = carry * n_prev + nk
        m_sc[:, h:h + 1] = m_new

        g_col = jnp.concatenate([glstm_ref[dv, :]] * (LP // LANES), axis=1)
        hn = hid * lax.rsqrt(jnp.mean(hid * hid, axis=0, keepdims=True) + EPS) * g_col
        h_out[:, dv] = (hn.T[:L, :] * jax.nn.sigmoid(o_ref[:, dv])).astype(h_out.dtype)
        yield

    for _ in zip(*[head(h) for h in range(LSTM_HEADS)]):
        yield

    @last_chunk
    def _finish():
        conv_out[...] = tail
        c_out[...] = c_sc[...]
        n_out[...] = n_sc[...]
        m_out[...] = m_sc[...]

    yield


def _mlstm(qk_raw, v_raw, o_raw, misc, conv0, c0, n0, m0, layer, w, L, G):
    b = conv0.shape[1]
    t = qk_raw.shape[0]
    nc = t // (b * L)
    assert nc * b * L == t and b % G == 0 and (G == 1 or nc == 1)
    hist = CONV_W - 1
    tok = lambda cdim: pl.BlockSpec((G * L, cdim), lambda bi, ci: (bi * nc + ci, 0))
    m0 = m0.reshape(m0.shape[0], b, 1, LSTM_HEADS)
    state0 = lambda *dims: pl.BlockSpec((None, G) + dims, lambda bi, ci: (layer, bi) + (0,) * len(dims))
    weights = [w["w_conv"], w["b_conv"], w["bgate"], w["g_lstm_out"]]
    out_shape = (
        jax.ShapeDtypeStruct((t, LSTM_WIDTH), BF16),
        jax.ShapeDtypeStruct((b, hist, LSTM_QK), F32),
        jax.ShapeDtypeStruct((b, LSTM_HEADS, LSTM_DV, LSTM_DK), F32),
        jax.ShapeDtypeStruct((b, LSTM_HEADS, LSTM_DK), F32),
        jax.ShapeDtypeStruct((b, 1, LSTM_HEADS), F32),
    )
    state = lambda *dims: pl.BlockSpec((G,) + dims, lambda bi, ci: (bi,) + (0,) * len(dims))
    h, conv_new, c_new, n_new, m_new = pl.pallas_call(
        functools.partial(_mlstm_kernel, L=L, G=G, nc=nc),
        grid=(b // G, nc),
        in_specs=[tok(LSTM_QK), tok(LSTM_WIDTH), tok(LSTM_WIDTH), tok(LANES),
                  state0(hist, LSTM_QK), state0(LSTM_HEADS, LSTM_DV, LSTM_DK), state0(LSTM_HEADS, LSTM_DK),
                  state0(1, LSTM_HEADS)] + [_full_spec(a) for a in weights],
        out_specs=(tok(LSTM_WIDTH), state(hist, LSTM_QK), state(LSTM_HEADS, LSTM_DV, LSTM_DK),
                   state(LSTM_HEADS, LSTM_DK), state(1, LSTM_HEADS)),
        out_shape=out_shape,
        scratch_shapes=[pltpu.VMEM((G, CONV_PAD + L, LSTM_QK), F32),
                        pltpu.VMEM((G, LSTM_HEADS, LSTM_DV, LSTM_DK), F32),
                        pltpu.VMEM((G, LSTM_HEADS, LSTM_DK), F32),
                        pltpu.VMEM((G, 1, LSTM_HEADS), F32)],
        compiler_params=pltpu.CompilerParams(dimension_semantics=("parallel", "arbitrary"),
                                             vmem_limit_bytes=VMEM_LIMIT),
        name="mlstm",
    )(qk_raw, v_raw, o_raw, misc, conv0, c0, n0, m0, *weights)
    return h, conv_new, c_new, n_new, m_new.reshape(b, LSTM_HEADS)


FF_CHUNK = 1024


def _out_ffn_kernel(x_ref, attn_ref, lstm_ref, w_out_ref, g_ffn_ref, w_up_ref, w_down_ref, y_ref):
    mix = jnp.concatenate([attn_ref[...], lstm_ref[...]], axis=1)
    x1 = x_ref[...] + jnp.dot(mix, w_out_ref[...], preferred_element_type=F32)
    xb = _rms(x1, g_ffn_ref[...]).astype(BF16)
    d_ff = w_up_ref.shape[1]
    acc = x1
    for f in range(d_ff // FF_CHUNK):
        cols = slice(f * FF_CHUNK, (f + 1) * FF_CHUNK)
        u = jnp.maximum(jnp.dot(xb, w_up_ref[:, cols], preferred_element_type=F32), 0.0)
        acc = acc + jnp.dot((u * u).astype(BF16), w_down_ref[cols, :], preferred_element_type=F32)
    y_ref[...] = acc


def _out_ffn(x2, attn, lstm, ffn_weights, layer, tm):
    t, d = x2.shape
    assert t % tm == 0
    row = lambda c: pl.BlockSpec((tm, c), lambda i: (i, 0))
    const = lambda a: pl.BlockSpec((None,) + a.shape[1:], lambda i: (layer, 0, 0), pipeline_mode=pl.Buffered(1))
    weights = list(ffn_weights)
    return pl.pallas_call(
        _out_ffn_kernel,
        grid=(t // tm,),
        in_specs=[row(d), row(MLA_WIDTH), row(LSTM_WIDTH)] + [const(a) for a in weights],
        out_specs=row(d),
        out_shape=jax.ShapeDtypeStruct((t, d), F32),
        compiler_params=pltpu.CompilerParams(dimension_semantics=("parallel",),
                                             vmem_limit_bytes=VMEM_LIMIT),
        name="out_ffn",
    )(x2, attn, lstm, *weights)


def _pack_layer(l, g_mix, w_in, g_q_lat, w_uq, g_q_nope, g_q_rope, g_kv_lat, g_k_rope, w_uk, g_k_nope, w_uv,
                w_conv, b_conv, b_igate, b_fgate, g_lstm_out):
    d = w_in.shape[1]
    wi = w_in[l]
    kpe = wi[:, W_IN_LAT[1]:W_IN_LAT[1] + ROPE_DIM]
    gates = wi[:, W_IN_LSTM[1]:W_IN_LSTM[1] + 2 * LSTM_HEADS]
    misc = jnp.concatenate([kpe, _rot_half(kpe), gates,
                            jnp.zeros((d, LANES - MISC_FG - LSTM_HEADS), F32)], axis=1)

    uq = w_uq[l].reshape(Q_LORA, MLA_HEADS, QK_DIM)
    uq = jnp.concatenate([uq, _rot_half(uq[..., NOPE_DIM:])], axis=-1).reshape(Q_LORA, MLA_HEADS * LANES)
    gq_head = jnp.concatenate([g_q_nope[l], g_q_rope[l], _rot_half(g_q_rope[l])])
    uk = w_uk[l].reshape(KV_LORA, MLA_HEADS, NOPE_DIM)
    uk = jnp.concatenate([uk, jnp.zeros_like(uk)], axis=-1).reshape(KV_LORA, MLA_HEADS * LANES)
    gkn_head = jnp.concatenate([g_k_nope[l], jnp.zeros((LANES - NOPE_DIM,), F32)])
    uv = w_uv[l].reshape(KV_LORA, MLA_HEADS, V_DIM)
    uv_t = jnp.concatenate([uv, jnp.zeros((KV_LORA, MLA_HEADS, VT_ROWS - V_DIM), F32)], axis=-1)
    uv_t = uv_t.reshape(KV_LORA, MLA_HEADS * VT_ROWS).T
    vones_head = jnp.zeros((VT_ROWS,), F32).at[V_DIM].set(1.0)
    gkm = jnp.concatenate([g_k_rope[l], _rot_half(g_k_rope[l]), jnp.zeros((LANES - 2 * ROPE_DIM,), F32)])
    bgate = jnp.concatenate([jnp.zeros((MISC_IG,), F32), b_igate[l], b_fgate[l],
                             jnp.zeros((LANES - MISC_FG - LSTM_HEADS,), F32)])

    i = jnp.arange(HEAD_PAIR)
    same = (i[:, None] // LANES) == (i[None, :] // LANES)
    li, lj = i[:, None] % LANES, i[None, :] % LANES
    mseg = jnp.where(same & (li < NOPE_DIM) & (lj < NOPE_DIM), 1.0 / NOPE_DIM,
                     jnp.where(same & (li >= NOPE_DIM) & (li < QK_DIM) & (lj >= NOPE_DIM), 1.0 / ROPE_DIM, 0.0))
    a = jnp.arange(LANES)
    mmisc = jnp.where((a[:, None] < ROPE_DIM) & (a[None, :] < 2 * ROPE_DIM), 1.0 / ROPE_DIM, 0.0)

    row = lambda v: v.reshape(1, -1).astype(F32)
    return {
        "g_mix": row(g_mix[l]), "w_lat": wi[:, W_IN_LAT[0]:W_IN_LAT[1]].astype(BF16),
        "w_lstm": wi[:, W_IN_LSTM[0]:W_IN_LSTM[1]].astype(BF16), "w_misc": misc.astype(BF16),
        "g_q_lat": row(g_q_lat[l]), "w_uq": uq.astype(BF16),
        "gq": row(jnp.tile(gq_head, MLA_HEADS)), "g_kv_lat": row(g_kv_lat[l]), "gkm": row(gkm),
        "w_uk": uk.astype(BF16), "gkn": row(jnp.tile(gkn_head, MLA_HEADS)), "w_uv": w_uv[l].astype(BF16),
        "w_uv_t": uv_t.astype(BF16), "vones": jnp.tile(vones_head, MLA_HEADS).reshape(-1, 1),
        "mseg": mseg.astype(BF16), "mmisc": mmisc.astype(BF16),
        "w_conv": w_conv[l], "b_conv": row(b_conv[l]), "bgate": row(bgate), "g_lstm_out": jnp.broadcast_to(g_lstm_out[l][:, None], (LSTM_WIDTH, LANES)),
    }


def _rope_table(first_pos, n):
    half = ROPE_DIM // 2
    blk = min(n, LANES)
    assert n % blk == 0
    lane = jnp.arange(LANES, dtype=jnp.int32)
    inv = ROPE_BASE ** (-(lane % half).astype(F32) / half)
    ang_a = (first_pos + blk * jnp.arange(n // blk, dtype=jnp.int32)).astype(F32)[:, None] * inv[None, :]
    ang_b = jnp.arange(blk, dtype=jnp.int32).astype(F32)[:, None] * inv[None, :]
    ca, sa = jnp.cos(ang_a)[:, None, :], jnp.sin(ang_a)[:, None, :]
    cb, sb = jnp.cos(ang_b)[None, :, :], jnp.sin(ang_b)[None, :, :]
    cos = ca * cb - sa * sb
    sin = sa * cb + ca * sb
    kind = (lane // half) % 4
    tab = jnp.where(kind < 2, cos, jnp.where(kind == 2, -sin, sin))
    return tab.reshape(n, LANES)


def kernel(x_prompt, x_sample, cache_kv_latent, cache_k_rope, state_conv, state_C, state_n, state_m,
           g_mix, w_in, g_q_lat, w_uq, g_q_nope, g_q_rope, g_kv_lat, g_k_rope, w_uk, g_k_nope, w_uv,
           w_conv, b_conv, b_igate, b_fgate, g_lstm_out, w_out, g_ffn, w_up, w_down):
    depth = w_in.shape[0]
    bp, sp, d = x_prompt.shape
    bs, ls, _ = x_sample.shape
    past = cache_kv_latent.shape[2]
    hist = CONV_W - 1

    tabs_p = jnp.tile(_rope_table(0, sp), (bp, 1))
    tabs_s = jnp.tile(_rope_table(past, ls), (bs, 1))
    cache_kr_t = jnp.swapaxes(cache_k_rope, 2, 3)
    xp = x_prompt.reshape(bp * sp, d)
    xs = x_sample.reshape(bs * ls, d)
    zero_conv = jnp.zeros((1, bp, hist, LSTM_QK), F32)
    zero_c = jnp.zeros((1, bp, LSTM_HEADS, LSTM_DV, LSTM_DK), F32)
    zero_n = jnp.zeros((1, bp, LSTM_HEADS, LSTM_DK), F32)
    zero_m = jnp.zeros((1, bp, LSTM_HEADS), F32)
    ffn_weights = (w_out.astype(BF16), g_ffn.reshape(depth, 1, d), w_up.astype(BF16), w_down.astype(BF16))

    outs = {k: [] for k in ("p_lat", "p_kr", "p_conv", "p_c", "p_n", "p_m",
                            "s_lat", "s_kr", "s_conv", "s_c", "s_n", "s_m")}
    for l in range(depth):
        w = _pack_layer(l, g_mix, w_in, g_q_lat, w_uq, g_q_nope, g_q_rope, g_kv_lat, g_k_rope, w_uk, g_k_nope,
                        w_uv, w_conv, b_conv, b_igate, b_fgate, g_lstm_out)
        q, k, v, lat, kr, qk_raw, v_raw, o_raw, misc = _projection(xp, tabs_p, w, tm=512)
        attn = _attention_prompt(q.reshape(bp, sp, -1), k.reshape(bp, sp, -1), v,
                                 tq=ATTN_TQ, tk=ATTN_TK)
        h, conv_new, c_new, n_new, m_new = _mlstm(qk_raw, v_raw, o_raw, misc, zero_conv, zero_c, zero_n, zero_m,
                                                  0, w, L=MLSTM_TILE, G=1)
        xp = _out_ffn(xp, attn.reshape(bp * sp, -1), h, ffn_weights, l, tm=512)
        outs["p_lat"].append(lat.reshape(bp, sp, KV_LORA))
        outs["p_kr"].append(kr.reshape(ROPE_DIM, bp, sp))
        outs["p_conv"].append(conv_new)
        outs["p_c"].append(c_new)
        outs["p_n"].append(n_new)
        outs["p_m"].append(m_new)
        q, k, v, lat, kr, qk_raw, v_raw, o_raw, misc = _projection(xs, tabs_s, w, tm=bs * ls)
        attn = _attention_sample(q, k, lat, cache_kv_latent, cache_kr_t, l, w, n_new=ls, G=ATTN_SAMPLE_GROUP)
        h, conv_new, c_new, n_new, m_new = _mlstm(qk_raw, v_raw, o_raw, misc, state_conv, state_C,
                                                  state_n, state_m, l, w, L=ls, G=MLSTM_SAMPLE_GROUP)
        xs = _out_ffn(xs, attn, h, ffn_weights, l, tm=bs * ls)
        outs["s_lat"].append(lat.reshape(bs, ls, KV_LORA))
        outs["s_kr"].append(kr.reshape(ROPE_DIM, bs, ls))
        outs["s_conv"].append(conv_new)
        outs["s_c"].append(c_new)
        outs["s_n"].append(n_new)
        outs["s_m"].append(m_new)

    st = lambda key: jnp.stack(outs[key])
    kr_st = lambda key: jnp.transpose(jnp.stack(outs[key]), (0, 2, 3, 1))
    return (xp.reshape(bp, sp, d), xs.reshape(bs, ls, d),
            st("p_lat"), kr_st("p_kr"), st("p_conv"), st("p_c"), st("p_n"), st("p_m"),
            st("s_lat"), kr_st("s_kr"), st("s_conv"), st("s_c"), st("s_n"), st("s_m"))
```

```python
import functools

import jax
import jax.numpy as jnp
from jax import lax
from jax.experimental import pallas as pl
from jax.experimental.pallas import tpu as pltpu

F32 = jnp.float32
BF16 = jnp.bfloat16

EPS = 1e-6
CHUNK = 64
MLA_HEADS = 8
NOPE_DIM = 64
ROPE_DIM = 32
QK_DIM = NOPE_DIM + ROPE_DIM
V_DIM = 64
Q_LORA = 256
KV_LORA = 128
ROPE_BASE = 10000.0
LSTM_HEADS = 4
LSTM_DK = 128
LSTM_DV = 128
CONV_W = 4
LSTM_WIDTH = LSTM_HEADS * LSTM_DV
LSTM_QK = 2 * LSTM_HEADS * LSTM_DK
MLA_WIDTH = MLA_HEADS * V_DIM

LANES = 128
SUBLANES = 8
HEAD_PAIR = 2 * LANES
VMEM_LIMIT = 52 * 1024 * 1024
Q_LOG2_SCALE = (QK_DIM ** -0.5) * 1.4426950408889634
ATTN_TQ = 512
ATTN_TK = 512
PAIRS_PER_TRIP = 4
MLSTM_TILE = 256
ATTN_SAMPLE_GROUP = 4
MLSTM_SAMPLE_GROUP = 4
VT_ROWS = 80

MISC_KPE = 0
MISC_KPE_ROT = ROPE_DIM
MISC_IG = 2 * ROPE_DIM
MISC_FG = MISC_IG + LSTM_HEADS

COL_CQ = ("lat", 0, Q_LORA)
COL_CKV = ("lat", Q_LORA, Q_LORA + KV_LORA)
COL_QK = ("lstm", 0, LSTM_QK)
COL_V = ("lstm", LSTM_QK, LSTM_QK + LSTM_WIDTH)
COL_O = ("lstm", LSTM_QK + LSTM_WIDTH, LSTM_QK + 2 * LSTM_WIDTH)
COL_MISC = ("misc", 0, LANES)
W_IN_LAT = (0, Q_LORA + KV_LORA)
W_IN_LSTM = (Q_LORA + KV_LORA + ROPE_DIM, Q_LORA + KV_LORA + ROPE_DIM + LSTM_QK + 2 * LSTM_WIDTH)


def _rot_half(a):
    half = ROPE_DIM // 2
    return jnp.concatenate([a[..., half:], a[..., :half]], axis=-1)


def _rms(x, g):
    return x * lax.rsqrt(jnp.mean(x * x, axis=-1, keepdims=True) + EPS) * g


def _in_turn(*stage_generators):
    for _ in zip(*stage_generators):
        pass


def _segment_mean_sq(y, m_ref):
    return jnp.dot((y * y).astype(BF16), m_ref[...], preferred_element_type=F32)


def _proj_kernel(x_ref, rope_ref, g_mix_ref, w_lat_ref, w_lstm_ref, w_misc_ref, g_qlat_ref, w_uq_ref,
                 gq_ref, g_kvlat_ref, gkm_ref, w_uk_ref, gkn_ref, w_uv_ref, vones_ref, mseg_ref, mmisc_ref,
                 q_out, k_out, v_out, lat_out, kr_out, qk_out, vl_out, o_out, misc_out):
    x = x_ref[...]
    xb = _rms(x, g_mix_ref[...]).astype(BF16)

    blocks = {"lat": w_lat_ref, "lstm": w_lstm_ref, "misc": w_misc_ref}

    def proj(col):
        return jnp.dot(xb, blocks[col[0]][:, col[1]:col[2]], preferred_element_type=F32)

    c_q = proj(COL_CQ)
    c_kv = proj(COL_CKV)
    misc = proj(COL_MISC)
    qk_out[...] = proj(COL_QK)
    misc_out[...] = misc

    tab = rope_ref[...]
    tab_next = pltpu.roll(tab, LANES - ROPE_DIM, 1)
    lane = lax.broadcasted_iota(jnp.int32, tab.shape, 1)
    in_rope = (lane >= NOPE_DIM) & (lane < QK_DIM)
    cq_tab = jnp.where(lane < NOPE_DIM, 1.0, jnp.where(in_rope, tab, 0.0))
    sq_tab = jnp.where(in_rope, tab_next, 0.0)
    ck_tab = jnp.where(lane < ROPE_DIM, tab, 0.0)
    sk_tab = jnp.where(lane < ROPE_DIM, tab_next, 0.0)

    cqn = _rms(c_q, g_qlat_ref[...]).astype(BF16)
    lat = _rms(c_kv, g_kvlat_ref[...])
    lat_out[...] = lat
    latb = lat.astype(BF16)
    lat_t = lat.T.astype(BF16)
    pairs = range(MLA_HEADS // 2)
    cols = [slice(p * HEAD_PAIR, (p + 1) * HEAD_PAIR) for p in pairs]

    q_raw = [jnp.dot(cqn, w_uq_ref[:, cols[p]], preferred_element_type=F32) for p in pairs]
    misc_mean = _segment_mean_sq(misc, mmisc_ref)
    vl_out[...] = proj(COL_V)
    q_mean = [_segment_mean_sq(q_raw[p], mseg_ref) for p in pairs]
    k_raw = [jnp.dot(latb, w_uk_ref[:, cols[p]], preferred_element_type=F32) for p in pairs]
    o_out[...] = proj(COL_O)
    k_mean = [_segment_mean_sq(k_raw[p], mseg_ref) for p in pairs]
    v_out[...] = (jnp.dot(w_uv_ref[...], lat_t, preferred_element_type=F32) + vones_ref[...]).astype(BF16)

    for p in pairs:
        y = q_raw[p] * lax.rsqrt(q_mean[p] + EPS) * gq_ref[:, cols[p]]
        for j in range(2):
            yh = y[:, j * LANES:(j + 1) * LANES]
            qh = yh * cq_tab + pltpu.roll(yh, LANES - ROPE_DIM, 1) * sq_tab
            h = 2 * p + j
            q_out[:, h * LANES:(h + 1) * LANES] = (qh * Q_LOG2_SCALE).astype(BF16)

    ms = misc * lax.rsqrt(misc_mean + EPS) * gkm_ref[...]
    kr = ms * ck_tab + pltpu.roll(ms, LANES - ROPE_DIM, 1) * sk_tab
    kr_out[...] = kr.T[:ROPE_DIM, :]
    kr_placed = pltpu.roll(kr, NOPE_DIM, 1)
    kr_pair = jnp.concatenate([kr_placed, kr_placed], axis=1)
    for p in pairs:
        y = k_raw[p] * lax.rsqrt(k_mean[p] + EPS) * gkn_ref[:, cols[p]]
        k_out[:, cols[p]] = (y + kr_pair).astype(BF16)


def _full_spec(a):
    nd = a.ndim
    return pl.BlockSpec(a.shape, lambda *_: (0,) * nd)


def _projection(x2, rope_tab, w, tm):
    t = x2.shape[0]
    assert t % tm == 0
    row = lambda c: pl.BlockSpec((tm, c), lambda i: (i, 0))
    weights = [w["g_mix"], w["w_lat"], w["w_lstm"], w["w_misc"], w["g_q_lat"], w["w_uq"], w["gq"], w["g_kv_lat"],
               w["gkm"], w["w_uk"], w["gkn"], w["w_uv_t"], w["vones"], w["mseg"], w["mmisc"]]
    out_shape = (
        jax.ShapeDtypeStruct((t, MLA_HEADS * LANES), BF16),
        jax.ShapeDtypeStruct((t, MLA_HEADS * LANES), BF16),
        jax.ShapeDtypeStruct((MLA_HEADS * VT_ROWS, t), BF16),
        jax.ShapeDtypeStruct((t, KV_LORA), F32),
        jax.ShapeDtypeStruct((ROPE_DIM, t), F32),
        jax.ShapeDtypeStruct((t, LSTM_QK), F32),
        jax.ShapeDtypeStruct((t, LSTM_WIDTH), F32),
        jax.ShapeDtypeStruct((t, LSTM_WIDTH), F32),
        jax.ShapeDtypeStruct((t, LANES), F32),
    )
    return pl.pallas_call(
        _proj_kernel,
        grid=(t // tm,),
        in_specs=[row(x2.shape[1]), row(LANES)] + [_full_spec(a) for a in weights],
        out_specs=tuple(pl.BlockSpec((s.shape[0], tm), lambda i: (0, i)) if n in (2, 4) else row(s.shape[1])
                        for n, s in enumerate(out_shape)),
        out_shape=out_shape,
        compiler_params=pltpu.CompilerParams(dimension_semantics=("parallel",),
                                             vmem_limit_bytes=VMEM_LIMIT),
        name="projection",
    )(x2, rope_tab, *weights)


def _attn_prompt_kernel(q_ref, qnext_ref, k_ref, vt_ref, o_ref, qt_sc, m_sc, acc_sc, sa_sc, sb_sc, xa_sc, xb_sc,
                        *, tq, tk):
    qi = pl.program_id(2)
    m_sc[...] = jnp.full(m_sc.shape, -1e30, F32)
    acc_sc[...] = jnp.zeros(acc_sc.shape, F32)
    q_col0 = qi * tq
    for j in range(2):
        qt_sc[j] = q_ref[:, j * LANES:(j + 1) * LANES].astype(F32).T.astype(BF16)
    buf_a = (sa_sc, xa_sc)
    buf_b = (sb_sc, xb_sc)

    def put_scores(j, k0, qt, buf):
        s = jnp.dot(k_ref[pl.ds(k0, tk), j * LANES:(j + 1) * LANES], qt, preferred_element_type=F32)
        buf[0][j] = s
        buf[1][j] = jnp.max(s, axis=0, keepdims=True)

    def scores_head(j, k0, buf):
        put_scores(j, k0, qt_sc[j], buf)

    def scores(k0, buf):
        for j in range(2):
            scores_head(j, k0, buf)

    def next_tile_scores(j):
        put_scores(j, 0, qnext_ref[:, j * LANES:(j + 1) * LANES].astype(F32).T.astype(BF16), buf_a)

    def consume_head(j, k0, buf, masked):
        vt = vt_ref[j * VT_ROWS:(j + 1) * VT_ROWS, pl.ds(k0, tk)]
        s = buf[0][j]
        if masked:
            kc = (k0 + lax.broadcasted_iota(jnp.int32, (tk, tq), 0)) // CHUNK
            qc = (q_col0 + lax.broadcasted_iota(jnp.int32, (tk, tq), 1)) // CHUNK
            s = jnp.where(kc <= qc, s, -1e30)
            s_max = jnp.max(s, axis=0, keepdims=True)
        else:
            s_max = buf[1][j]
        m_prev = m_sc[j]
        m_next = jnp.maximum(m_prev, s_max)
        alpha = jnp.exp2(m_prev - m_next)
        p = jnp.exp2(s - m_next)
        m_sc[j] = m_next
        acc_sc[j] = alpha * acc_sc[j] + jnp.dot(vt, p.astype(BF16), preferred_element_type=F32)

    def consume(k0, buf, masked):
        for j in range(2):
            consume_head(j, k0, buf, masked)

    n_full = q_col0 // tk
    at = lambda t: pl.multiple_of(t * tk, tk)

    @pl.when(qi == 0)
    def _first_tile_of_pair():
        scores(0, buf_a)

    def pair(t):
        scores(at(t + 1), buf_b)
        for j in range(2):
            consume_head(j, at(t), buf_a, False)
            scores_head(j, at(t + 2), buf_a)
        consume(at(t + 1), buf_b, False)

    def body(i, carry):
        for u in range(PAIRS_PER_TRIP):
            pair(2 * PAIRS_PER_TRIP * i + 2 * u)
        return carry

    trips = n_full // (2 * PAIRS_PER_TRIP)
    lax.fori_loop(0, trips, body, 0)

    def rest(i, carry):
        pair(2 * PAIRS_PER_TRIP * trips + 2 * i)
        return carry

    lax.fori_loop(0, (n_full % (2 * PAIRS_PER_TRIP)) // 2, rest, 0)

    t0 = 2 * (n_full // 2)

    @pl.when(n_full % 2 == 1)
    def _odd_tail():
        scores(at(t0 + 1), buf_b)
        for j in range(2):
            consume_head(j, at(t0), buf_a, False)
            next_tile_scores(j)
        consume(at(t0 + 1), buf_b, True)

    @pl.when(n_full % 2 == 0)
    def _even_tail():
        for j in range(2):
            consume_head(j, at(t0), buf_a, True)
            next_tile_scores(j)

    outs = []
    for j in range(2):
        acc = acc_sc[j]
        outs.append(acc[:V_DIM, :] / acc[V_DIM:V_DIM + 1, :])
    o_ref[...] = jnp.concatenate(outs, axis=0).T.astype(o_ref.dtype)


def _attention_prompt(q, k, vt, tq, tk):
    b, s, _ = q.shape
    assert tq == tk and s % tq == 0 and tq % CHUNK == 0
    pairs = MLA_HEADS // 2
    return pl.pallas_call(
        functools.partial(_attn_prompt_kernel, tq=tq, tk=tk),
        grid=(b, pairs, s // tq),
        in_specs=[
            pl.BlockSpec((None, tq, HEAD_PAIR), lambda bi, p, i: (bi, i, p)),
            pl.BlockSpec((None, tq, HEAD_PAIR), lambda bi, p, i: (bi, jnp.minimum(i + 1, s // tq - 1), p)),
            pl.BlockSpec((None, s, HEAD_PAIR), lambda bi, p, i: (bi, 0, p)),
            pl.BlockSpec((2 * VT_ROWS, s), lambda bi, p, i: (p, bi)),
        ],
        out_specs=pl.BlockSpec((None, tq, LANES), lambda bi, p, i: (bi, i, p)),
        out_shape=jax.ShapeDtypeStruct((b, s, MLA_WIDTH), BF16),
        scratch_shapes=[pltpu.VMEM((2, LANES, tq), BF16),
                        pltpu.VMEM((2, 1, tq), F32),
                        pltpu.VMEM((2, VT_ROWS, tq), F32),
                        pltpu.VMEM((2, tk, tq), F32),
                        pltpu.VMEM((2, tk, tq), F32),
                        pltpu.VMEM((2, 1, tq), F32),
                        pltpu.VMEM((2, 1, tq), F32)],
        compiler_params=pltpu.CompilerParams(
            dimension_semantics=("parallel", "parallel", "arbitrary"),
            vmem_limit_bytes=VMEM_LIMIT),
        name="attention_prompt",
    )(q, q, k, vt)


def _attn_sample_kernel(q_ref, kn_ref, latn_ref, lat_ref, kr_ref, w_uk_ref, gkn_ref, w_uv_ref, mseg_ref,
                        o_ref, k_sc, *, n_new, G):
    def sequence(g):
        tok = lambda ref: ref.at[g * n_new:(g + 1) * n_new]
        return _attn_sample_sequence(tok(q_ref), tok(kn_ref), tok(latn_ref), lat_ref.at[g], kr_ref.at[g],
                                     w_uk_ref, gkn_ref, w_uv_ref, mseg_ref, tok(o_ref), k_sc.at[g], n_new=n_new)

    _in_turn(*[sequence(g) for g in range(G)])


def _attn_sample_sequence(q_ref, kn_ref, latn_ref, lat_ref, kr_ref, w_uk_ref, gkn_ref, w_uv_ref, mseg_ref,
                          o_ref, k_sc, *, n_new):
    latb = lat_ref[...].astype(BF16)
    kr_t = kr_ref[...]
    zeros = lambda r: jnp.zeros((r, kr_t.shape[1]), F32)
    kr_placed = jnp.concatenate([zeros(NOPE_DIM), kr_t, zeros(LANES - QK_DIM)], axis=0).T
    kr_pair = jnp.concatenate([kr_placed, kr_placed], axis=1)
    pairs = range(MLA_HEADS // 2)
    cols = [slice(p * HEAD_PAIR, (p + 1) * HEAD_PAIR) for p in pairs]
    raw = [jnp.dot(latb, w_uk_ref[:, cols[p]], preferred_element_type=F32) for p in pairs]
    yield
    mean_sq = [_segment_mean_sq(raw[p], mseg_ref) for p in pairs]
    yield
    for p in pairs:
        y = raw[p] * lax.rsqrt(mean_sq[p] + EPS) * gkn_ref[:, cols[p]]
        k_sc[:, cols[p]] = (y + kr_pair).astype(BF16)

    q = q_ref[...]
    qt = jnp.concatenate([q] * MLA_HEADS, axis=0)
    r_head = lax.broadcasted_iota(jnp.int32, qt.shape, 0) // n_new
    c_head = lax.broadcasted_iota(jnp.int32, qt.shape, 1) // LANES
    qm = jnp.where(r_head == c_head, qt, jnp.zeros_like(qt))

    nt = (((1,), (1,)), ((), ()))
    s_old = lax.dot_general(k_sc[...], qm, nt, preferred_element_type=F32)
    s_new = lax.dot_general(kn_ref[...], qm, nt, preferred_element_type=F32)
    yield
    mx = jnp.maximum(jnp.max(s_old, axis=0, keepdims=True), jnp.max(s_new, axis=0, keepdims=True))
    p_old = jnp.exp2(s_old - mx)
    p_new = jnp.exp2(s_new - mx)
    inv = 1.0 / (jnp.sum(p_old, axis=0, keepdims=True) + jnp.sum(p_new, axis=0, keepdims=True))
    p_old = (p_old * inv).astype(BF16)
    p_new = (p_new * inv).astype(BF16)
    tn = (((0,), (0,)), ((), ()))
    ctx = (lax.dot_general(p_old, latb, tn, preferred_element_type=F32)
           + lax.dot_general(p_new, latn_ref[...].astype(BF16), tn, preferred_element_type=F32))
    yield
    full = jnp.dot(ctx.astype(BF16), w_uv_ref[...], preferred_element_type=F32)
    v_head = lax.broadcasted_iota(jnp.int32, (n_new, MLA_WIDTH), 1) // V_DIM
    out = jnp.zeros((n_new, MLA_WIDTH), F32)
    for h in range(MLA_HEADS):
        out = out + jnp.where(v_head == h, full[h * n_new:(h + 1) * n_new, :], 0.0)
    o_ref[...] = out.astype(o_ref.dtype)
    yield


def _attention_sample(q, k_new, lat_new, cache_lat, cache_kr, layer, w, n_new, G):
    _, b, past, _ = cache_lat.shape
    assert b % G == 0
    weights = [w["w_uk"], w["gkn"], w["w_uv"], w["mseg"]]
    tok = lambda c: pl.BlockSpec((G * n_new, c), lambda i: (i, 0))
    return pl.pallas_call(
        functools.partial(_attn_sample_kernel, n_new=n_new, G=G),
        grid=(b // G,),
        in_specs=[tok(MLA_HEADS * LANES), tok(MLA_HEADS * LANES), tok(KV_LORA),
                  pl.BlockSpec((None, G, past, KV_LORA), lambda i: (layer, i, 0, 0)),
                  pl.BlockSpec((None, G, ROPE_DIM, past), lambda i: (layer, i, 0, 0))]
                 + [_full_spec(a) for a in weights],
        out_specs=tok(MLA_WIDTH),
        out_shape=jax.ShapeDtypeStruct((b * n_new, MLA_WIDTH), BF16),
        scratch_shapes=[pltpu.VMEM((G, past, MLA_HEADS * LANES), BF16)],
        compiler_params=pltpu.CompilerParams(dimension_semantics=("parallel",),
                                             vmem_limit_bytes=VMEM_LIMIT),
        name="attention_sample",
    )(q, k_new, lat_new, cache_lat, cache_kr, *weights)


CONV_PAD = SUBLANES


def _mlstm_kernel(qk_ref, v_ref, o_ref, misc_ref, conv0_ref, c0_ref, n0_ref, m0_ref, wconv_ref, bconv_ref,
                  bgate_ref, glstm_ref,
                  h_out, conv_out, c_out, n_out, m_out,
                  full_sc, c_sc, n_sc, m_sc, *, L, G, nc):
    def sequence(g):
        tok = lambda ref: ref.at[g * L:(g + 1) * L]
        return _mlstm_sequence(tok(qk_ref), tok(v_ref), tok(o_ref), tok(misc_ref), conv0_ref.at[g], c0_ref.at[g],
                               n0_ref.at[g], m0_ref.at[g], wconv_ref, bconv_ref, bgate_ref, glstm_ref,
                               tok(h_out), conv_out.at[g], c_out.at[g], n_out.at[g], m_out.at[g],
                               full_sc.at[g], c_sc.at[g], n_sc.at[g], m_sc.at[g], L=L, nc=nc)

    _in_turn(*[sequence(g) for g in range(G)])


def _mlstm_sequence(qk_ref, v_ref, o_ref, misc_ref, conv0_ref, c0_ref, n0_ref, m0_ref, wconv_ref, bconv_ref,
                    bgate_ref, glstm_ref,
                    h_out, conv_out, c_out, n_out, m_out,
                    full_sc, c_sc, n_sc, m_sc, *, L, nc):
    hist = CONV_W - 1
    lo = CONV_PAD - hist
    first_chunk = (lambda f: f()) if nc == 1 else pl.when(pl.program_id(1) == 0)
    last_chunk = (lambda f: f()) if nc == 1 else pl.when(pl.program_id(1) == nc - 1)

    @first_chunk
    def _init():
        full_sc[lo:CONV_PAD, :] = conv0_ref[...]
        c_sc[...] = c0_ref[...]
        n_sc[...] = n0_ref[...]
        m_sc[...] = m0_ref[...]

    full_sc[CONV_PAD:CONV_PAD + L, :] = qk_ref[...]
    ext = full_sc[...]
    y = bconv_ref[...] + ext[CONV_PAD:, :] * wconv_ref[CONV_W - 1:CONV_W, :]
    for r in range(1, CONV_W):
        y = y + pltpu.roll(ext, r, 0)[CONV_PAD:, :] * wconv_ref[CONV_W - 1 - r:CONV_W - r, :]
    qk = y * jax.nn.sigmoid(y)
    tail = full_sc[lo + L:CONV_PAD + L, :]
    full_sc[lo:CONV_PAD, :] = tail
    yield

    LP = max(L, LANES)
    pad_rows = lambda a: a if L == LP else jnp.concatenate([a, jnp.zeros((LP - L, a.shape[1]), a.dtype)], axis=0)
    gs_t = pad_rows(misc_ref[...] + bgate_ref[...]).T
    g8 = gs_t[MISC_IG:MISC_IG + 2 * LSTM_HEADS, :]
    lf8 = jnp.minimum(g8, 0.0) - jnp.log1p(jnp.exp(-jnp.abs(g8)))
    s_idx = lax.broadcasted_iota(jnp.int32, (LP, LP), 0)
    t_idx = lax.broadcasted_iota(jnp.int32, (LP, LP), 1)
    causal = s_idx <= t_idx
    triu = jnp.where(causal, 1.0, 0.0).astype(BF16)
    lf_hi = lf8.astype(BF16)
    lf_mid = (lf8 - lf_hi.astype(F32)).astype(BF16)
    lf_lo = (lf8 - lf_hi.astype(F32) - lf_mid.astype(F32)).astype(BF16)
    b8 = (jnp.dot(lf_hi, triu, preferred_element_type=F32)
          + jnp.dot(lf_mid, triu, preferred_element_type=F32)
          + jnp.dot(lf_lo, triu, preferred_element_type=F32))
    yield
    c4 = g8[:LSTM_HEADS, :] - b8[LSTM_HEADS:, :]
    c_cols = jnp.concatenate([c4, jnp.zeros((LANES - LSTM_HEADS, LP), F32)], axis=0).T
    lane_t = lax.broadcasted_iota(jnp.int32, (1, LP), 1)

    def head(h):
        dk = slice(h * LSTM_DK, (h + 1) * LSTM_DK)
        dv = slice(h * LSTM_DV, (h + 1) * LSTM_DV)
        ig_row = g8[h:h + 1, :]
        b_row = b8[LSTM_HEADS + h:LSTM_HEADS + h + 1, :]
        c_col = c_cols[:, h:h + 1]
        m_prev = m_sc[:, h:h + 1]

        log_d = jnp.where(causal, c_col + b_row, -jnp.inf)
        inter = b_row + m_prev
        m_t = jnp.maximum(inter, jnp.max(log_d, axis=0, keepdims=True))
        decay = jnp.exp(log_d - m_t)
        inter_scale = jnp.exp(inter - m_t)

        qb = pad_rows(qk[:, dk]).astype(BF16)
        kb = pad_rows(qk[:, LSTM_HEADS * LSTM_DK + h * LSTM_DK:LSTM_HEADS * LSTM_DK + (h + 1) * LSTM_DK]
                      * (LSTM_DK ** -0.5)).astype(BF16)
        v_t = pad_rows(v_ref[:, dv]).T
        c_prev = c_sc[h]
        n_prev = n_sc[h:h + 1, :]
        nt = (((1,), (1,)), ((), ()))
        wgt = lax.dot_general(kb, qb, nt, preferred_element_type=F32) * decay
        cq = lax.dot_general(c_prev.astype(BF16), qb, nt, preferred_element_type=F32)
        nq = lax.dot_general(jnp.broadcast_to(n_prev, (SUBLANES, LSTM_DK)).astype(BF16), qb, nt,
                             preferred_element_type=F32)[:1, :]
        yield
        num = jnp.dot(v_t.astype(BF16), wgt.astype(BF16), preferred_element_type=F32) + inter_scale * cq
        den = jnp.sum(wgt, axis=0, keepdims=True) + inter_scale * nq
        hid = num / jnp.maximum(jnp.abs(den), jnp.exp(-m_t))
        yield

        m_new = m_t[:, L - 1:L]
        b_last = b_row[:, L - 1:L]
        carry = jnp.exp(b_last + m_prev - m_new)
        d_end = jnp.exp(b_last - b_row + ig_row - m_new)
        if L < LP:
            d_end = jnp.where(lane_t < L, d_end, 0.0)
        upd = jnp.dot((v_t * d_end).astype(BF16), kb, preferred_element_type=F32)
        nk = jnp.dot(jnp.broadcast_to(d_end, (SUBLANES, LP)).astype(BF16), kb,
                     preferred_element_type=F32)[:1, :]
        c_sc[h] = carry * c_prev + upd
        n_sc[h:h + 1, :] = carry * n_prev + nk
        m_sc[:, h:h + 1] = m_new

        g_col = jnp.concatenate([glstm_ref[dv, :]] * (LP // LANES), axis=1)
        hn = hid * lax.rsqrt(jnp.mean(hid * hid, axis=0, keepdims=True) + EPS) * g_col
        h_out[:, dv] = (hn.T[:L, :] * jax.nn.sigmoid(o_ref[:, dv])).astype(h_out.dtype)
        yield

    for _ in zip(*[head(h) for h in range(LSTM_HEADS)]):
        yield

    @last_chunk
    def _finish():
        conv_out[...] = tail
        c_out[...] = c_sc[...]
        n_out[...] = n_sc[...]
        m_out[...] = m_sc[...]

    yield


def _mlstm(qk_raw, v_raw, o_raw, misc, conv0, c0, n0, m0, layer, w, L, G):
    b = conv0.shape[1]
    t = qk_raw.shape[0]
    nc = t // (b * L)
    assert nc * b * L == t and b % G == 0 and (G == 1 or nc == 1)
    hist = CONV_W - 1
    tok = lambda cdim: pl.BlockSpec((G * L, cdim), lambda bi, ci: (bi * nc + ci, 0))
    m0 = m0.reshape(m0.shape[0], b, 1, LSTM_HEADS)
    state0 = lambda *dims: pl.BlockSpec((None, G) + dims, lambda bi, ci: (layer, bi) + (0,) * len(dims))
    weights = [w["w_conv"], w["b_conv"], w["bgate"], w["g_lstm_out"]]
    out_shape = (
        jax.ShapeDtypeStruct((t, LSTM_WIDTH), BF16),
        jax.ShapeDtypeStruct((b, hist, LSTM_QK), F32),
        jax.ShapeDtypeStruct((b, LSTM_HEADS, LSTM_DV, LSTM_DK), F32),
        jax.ShapeDtypeStruct((b, LSTM_HEADS, LSTM_DK), F32),
        jax.ShapeDtypeStruct((b, 1, LSTM_HEADS), F32),
    )
    state = lambda *dims: pl.BlockSpec((G,) + dims, lambda bi, ci: (bi,) + (0,) * len(dims))
    h, conv_new, c_new, n_new, m_new = pl.pallas_call(
        functools.partial(_mlstm_kernel, L=L, G=G, nc=nc),
        grid=(b // G, nc),
        in_specs=[tok(LSTM_QK), tok(LSTM_WIDTH), tok(LSTM_WIDTH), tok(LANES),
                  state0(hist, LSTM_QK), state0(LSTM_HEADS, LSTM_DV, LSTM_DK), state0(LSTM_HEADS, LSTM_DK),
                  state0(1, LSTM_HEADS)] + [_full_spec(a) for a in weights],
        out_specs=(tok(LSTM_WIDTH), state(hist, LSTM_QK), state(LSTM_HEADS, LSTM_DV, LSTM_DK),
                   state(LSTM_HEADS, LSTM_DK), state(1, LSTM_HEADS)),
        out_shape=out_shape,
        scratch_shapes=[pltpu.VMEM((G, CONV_PAD + L, LSTM_QK), F32),
                        pltpu.VMEM((G, LSTM_HEADS, LSTM_DV, LSTM_DK), F32),
                        pltpu.VMEM((G, LSTM_HEADS, LSTM_DK), F32),
                        pltpu.VMEM((G, 1, LSTM_HEADS), F32)],
        compiler_params=pltpu.CompilerParams(dimension_semantics=("parallel", "arbitrary"),
                                             vmem_limit_bytes=VMEM_LIMIT),
        name="mlstm",
    )(qk_raw, v_raw, o_raw, misc, conv0, c0, n0, m0, *weights)
    return h, conv_new, c_new, n_new, m_new.reshape(b, LSTM_HEADS)


FF_CHUNK = 1024


def _out_ffn_kernel(x_ref, attn_ref, lstm_ref, w_out_ref, g_ffn_ref, w_up_ref, w_down_ref, y_ref):
    mix = jnp.concatenate([attn_ref[...], lstm_ref[...]], axis=1)
    x1 = x_ref[...] + jnp.dot(mix, w_out_ref[...], preferred_element_type=F32)
    xb = _rms(x1, g_ffn_ref[...]).astype(BF16)
    d_ff = w_up_ref.shape[1]
    acc = x1
    for f in range(d_ff // FF_CHUNK):
        cols = slice(f * FF_CHUNK, (f + 1) * FF_CHUNK)
        u = jnp.maximum(jnp.dot(xb, w_up_ref[:, cols], preferred_element_type=F32), 0.0)
        acc = acc + jnp.dot((u * u).astype(BF16), w_down_ref[cols, :], preferred_element_type=F32)
    y_ref[...] = acc


def _out_ffn(x2, attn, lstm, ffn_weights, layer, tm):
    t, d = x2.shape
    assert t % tm == 0
    row = lambda c: pl.BlockSpec((tm, c), lambda i: (i, 0))
    const = lambda a: pl.BlockSpec((None,) + a.shape[1:], lambda i: (layer, 0, 0), pipeline_mode=pl.Buffered(1))
    weights = list(ffn_weights)
    return pl.pallas_call(
        _out_ffn_kernel,
        grid=(t // tm,),
        in_specs=[row(d), row(MLA_WIDTH), row(LSTM_WIDTH)] + [const(a) for a in weights],
        out_specs=row(d),
        out_shape=jax.ShapeDtypeStruct((t, d), F32),
        compiler_params=pltpu.CompilerParams(dimension_semantics=("parallel",),
                                             vmem_limit_bytes=VMEM_LIMIT),
        name="out_ffn",
    )(x2, attn, lstm, *weights)


def _pack_layer(l, g_mix, w_in, g_q_lat, w_uq, g_q_nope, g_q_rope, g_kv_lat, g_k_rope, w_uk, g_k_nope, w_uv,
                w_conv, b_conv, b_igate, b_fgate, g_lstm_out):
    d = w_in.shape[1]
    wi = w_in[l]
    kpe = wi[:, W_IN_LAT[1]:W_IN_LAT[1] + ROPE_DIM]
    gates = wi[:, W_IN_LSTM[1]:W_IN_LSTM[1] + 2 * LSTM_HEADS]
    misc = jnp.concatenate([kpe, _rot_half(kpe), gates,
                            jnp.zeros((d, LANES - MISC_FG - LSTM_HEADS), F32)], axis=1)

    uq = w_uq[l].reshape(Q_LORA, MLA_HEADS, QK_DIM)
    uq = jnp.concatenate([uq, _rot_half(uq[..., NOPE_DIM:])], axis=-1).reshape(Q_LORA, MLA_HEADS * LANES)
    gq_head = jnp.concatenate([g_q_nope[l], g_q_rope[l], _rot_half(g_q_rope[l])])
    uk = w_uk[l].reshape(KV_LORA, MLA_HEADS, NOPE_DIM)
    uk = jnp.concatenate([uk, jnp.zeros_like(uk)], axis=-1).reshape(KV_LORA, MLA_HEADS * LANES)
    gkn_head = jnp.concatenate([g_k_nope[l], jnp.zeros((LANES - NOPE_DIM,), F32)])
    uv = w_uv[l].reshape(KV_LORA, MLA_HEADS, V_DIM)
    uv_t = jnp.concatenate([uv, jnp.zeros((KV_LORA, MLA_HEADS, VT_ROWS - V_DIM), F32)], axis=-1)
    uv_t = uv_t.reshape(KV_LORA, MLA_HEADS * VT_ROWS).T
    vones_head = jnp.zeros((VT_ROWS,), F32).at[V_DIM].set(1.0)
    gkm = jnp.concatenate([g_k_rope[l], _rot_half(g_k_rope[l]), jnp.zeros((LANES - 2 * ROPE_DIM,), F32)])
    bgate = jnp.concatenate([jnp.zeros((MISC_IG,), F32), b_igate[l], b_fgate[l],
                             jnp.zeros((LANES - MISC_FG - LSTM_HEADS,), F32)])

    i = jnp.arange(HEAD_PAIR)
    same = (i[:, None] // LANES) == (i[None, :] // LANES)
    li, lj = i[:, None] % LANES, i[None, :] % LANES
    mseg = jnp.where(same & (li < NOPE_DIM) & (lj < NOPE_DIM), 1.0 / NOPE_DIM,
                     jnp.where(same & (li >= NOPE_DIM) & (li < QK_DIM) & (lj >= NOPE_DIM), 1.0 / ROPE_DIM, 0.0))
    a = jnp.arange(LANES)
    mmisc = jnp.where((a[:, None] < ROPE_DIM) & (a[None, :] < 2 * ROPE_DIM), 1.0 / ROPE_DIM, 0.0)

    row = lambda v: v.reshape(1, -1).astype(F32)
    return {
        "g_mix": row(g_mix[l]), "w_lat": wi[:, W_IN_LAT[0]:W_IN_LAT[1]].astype(BF16),
        "w_lstm": wi[:, W_IN_LSTM[0]:W_IN_LSTM[1]].astype(BF16), "w_misc": misc.astype(BF16),
        "g_q_lat": row(g_q_lat[l]), "w_uq": uq.astype(BF16),
        "gq": row(jnp.tile(gq_head, MLA_HEADS)), "g_kv_lat": row(g_kv_lat[l]), "gkm": row(gkm),
        "w_uk": uk.astype(BF16), "gkn": row(jnp.tile(gkn_head, MLA_HEADS)), "w_uv": w_uv[l].astype(BF16),
        "w_uv_t": uv_t.astype(BF16), "vones": jnp.tile(vones_head, MLA_HEADS).reshape(-1, 1),
        "mseg": mseg.astype(BF16), "mmisc": mmisc.astype(BF16),
        "w_conv": w_conv[l], "b_conv": row(b_conv[l]), "bgate": row(bgate), "g_lstm_out": jnp.broadcast_to(g_lstm_out[l][:, None], (LSTM_WIDTH, LANES)),
    }


def _rope_table(first_pos, n):
    half = ROPE_DIM // 2
    blk = min(n, LANES)
    assert n % blk == 0
    lane = jnp.arange(LANES, dtype=jnp.int32)
    inv = ROPE_BASE ** (-(lane % half).astype(F32) / half)
    ang_a = (first_pos + blk * jnp.arange(n // blk, dtype=jnp.int32)).astype(F32)[:, None] * inv[None, :]
    ang_b = jnp.arange(blk, dtype=jnp.int32).astype(F32)[:, None] * inv[None, :]
    ca, sa = jnp.cos(ang_a)[:, None, :], jnp.sin(ang_a)[:, None, :]
    cb, sb = jnp.cos(ang_b)[None, :, :], jnp.sin(ang_b)[None, :, :]
    cos = ca * cb - sa * sb
    sin = sa * cb + ca * sb
    kind = (lane // half) % 4
    tab = jnp.where(kind < 2, cos, jnp.where(kind == 2, -sin, sin))
    return tab.reshape(n, LANES)


def kernel(x_prompt, x_sample, cache_kv_latent, cache_k_rope, state_conv, state_C, state_n, state_m,
           g_mix, w_in, g_q_lat, w_uq, g_q_nope, g_q_rope, g_kv_lat, g_k_rope, w_uk, g_k_nope, w_uv,
           w_conv, b_conv, b_igate, b_fgate, g_lstm_out, w_out, g_ffn, w_up, w_down):
    depth = w_in.shape[0]
    bp, sp, d = x_prompt.shape
    bs, ls, _ = x_sample.shape
    past = cache_kv_latent.shape[2]
    hist = CONV_W - 1

    tabs_p = jnp.tile(_rope_table(0, sp), (bp, 1))
    tabs_s = jnp.tile(_rope_table(past, ls), (bs, 1))
    cache_kr_t = jnp.swapaxes(cache_k_rope, 2, 3)
    xp = x_prompt.reshape(bp * sp, d)
    xs = x_sample.reshape(bs * ls, d)
    zero_conv = jnp.zeros((1, bp, hist, LSTM_QK), F32)
    zero_c = jnp.zeros((1, bp, LSTM_HEADS, LSTM_DV, LSTM_DK), F32)
    zero_n = jnp.zeros((1, bp, LSTM_HEADS, LSTM_DK), F32)
    zero_m = jnp.zeros((1, bp, LSTM_HEADS), F32)
    ffn_weights = (w_out.astype(BF16), g_ffn.reshape(depth, 1, d), w_up.astype(BF16), w_down.astype(BF16))

    outs = {k: [] for k in ("p_lat", "p_kr", "p_conv", "p_c", "p_n", "p_m",
                            "s_lat", "s_kr", "s_conv", "s_c", "s_n", "s_m")}
    for l in range(depth):
        w = _pack_layer(l, g_mix, w_in, g_q_lat, w_uq, g_q_nope, g_q_rope, g_kv_lat, g_k_rope, w_uk, g_k_nope,
                        w_uv, w_conv, b_conv, b_igate, b_fgate, g_lstm_out)
        q, k, v, lat, kr, qk_raw, v_raw, o_raw, misc = _projection(xp, tabs_p, w, tm=512)
        attn = _attention_prompt(q.reshape(bp, sp, -1), k.reshape(bp, sp, -1), v,
                                 tq=ATTN_TQ, tk=ATTN_TK)
        h, conv_new, c_new, n_new, m_new = _mlstm(qk_raw, v_raw, o_raw, misc, zero_conv, zero_c, zero_n, zero_m,
                                                  0, w, L=MLSTM_TILE, G=1)
        xp = _out_ffn(xp, attn.reshape(bp * sp, -1), h, ffn_weights, l, tm=512)
        outs["p_lat"].append(lat.reshape(bp, sp, KV_LORA))
        outs["p_kr"].append(kr.reshape(ROPE_DIM, bp, sp))
        outs["p_conv"].append(conv_new)
        outs["p_c"].append(c_new)
        outs["p_n"].append(n_new)
        outs["p_m"].append(m_new)
        q, k, v, lat, kr, qk_raw, v_raw, o_raw, misc = _projection(xs, tabs_s, w, tm=bs * ls)
        attn = _attention_sample(q, k, lat, cache_kv_latent, cache_kr_t, l, w, n_new=ls, G=ATTN_SAMPLE_GROUP)
        h, conv_new, c_new, n_new, m_new = _mlstm(qk_raw, v_raw, o_raw, misc, state_conv, state_C,
                                                  state_n, state_m, l, w, L=ls, G=MLSTM_SAMPLE_GROUP)
        xs = _out_ffn(xs, attn, h, ffn_weights, l, tm=bs * ls)
        outs["s_lat"].append(lat.reshape(bs, ls, KV_LORA))
        outs["s_kr"].append(kr.reshape(ROPE_DIM, bs, ls))
        outs["s_conv"].append(conv_new)
        outs["s_c"].append(c_new)
        outs["s_n"].append(n_new)
        outs["s_m"].append(m_new)

    st = lambda key: jnp.stack(outs[key])
    kr_st = lambda key: jnp.transpose(jnp.stack(outs[key]), (0, 2, 3, 1))
    return (xp.reshape(bp, sp, d), xs.reshape(bs, ls, d),
            st("p_lat"), kr_st("p_kr"), st("p_conv"), st("p_c"), st("p_n"), st("p_m"),
            st("s_lat"), kr_st("s_kr"), st("s_conv"), st("s_c"), st("s_n"), st("s_m"))
```

```python
import functools

import jax
import jax.numpy as jnp
from jax import lax
from jax.experimental import pallas as pl
from jax.experimental.pallas import tpu as pltpu

F32 = jnp.float32
BF16 = jnp.bfloat16

EPS = 1e-6
CHUNK = 64
MLA_HEADS = 8
NOPE_DIM = 64
ROPE_DIM = 32
QK_DIM = NOPE_DIM + ROPE_DIM
V_DIM = 64
Q_LORA = 256
KV_LORA = 128
ROPE_BASE = 10000.0
LSTM_HEADS = 4
LSTM_DK = 128
LSTM_DV = 128
CONV_W = 4
LSTM_WIDTH = LSTM_HEADS * LSTM_DV
LSTM_QK = 2 * LSTM_HEADS * LSTM_DK
MLA_WIDTH = MLA_HEADS * V_DIM

LANES = 128
SUBLANES = 8
HEAD_PAIR = 2 * LANES
VMEM_LIMIT = 52 * 1024 * 1024
Q_LOG2_SCALE = (QK_DIM ** -0.5) * 1.4426950408889634
ATTN_TQ = 512
ATTN_TK = 512
PAIRS_PER_TRIP = 4
MLSTM_TILE = 256
ATTN_SAMPLE_GROUP = 4
MLSTM_SAMPLE_GROUP = 4
VT_ROWS = 80

MISC_KPE = 0
MISC_KPE_ROT = ROPE_DIM
MISC_IG = 2 * ROPE_DIM
MISC_FG = MISC_IG + LSTM_HEADS

COL_CQ = ("lat", 0, Q_LORA)
COL_CKV = ("lat", Q_LORA, Q_LORA + KV_LORA)
COL_QK = ("lstm", 0, LSTM_QK)
COL_V = ("lstm", LSTM_QK, LSTM_QK + LSTM_WIDTH)
COL_O = ("lstm", LSTM_QK + LSTM_WIDTH, LSTM_QK + 2 * LSTM_WIDTH)
COL_MISC = ("misc", 0, LANES)
W_IN_LAT = (0, Q_LORA + KV_LORA)
W_IN_LSTM = (Q_LORA + KV_LORA + ROPE_DIM, Q_LORA + KV_LORA + ROPE_DIM + LSTM_QK + 2 * LSTM_WIDTH)


def _rot_half(a):
    half = ROPE_DIM // 2
    return jnp.concatenate([a[..., half:], a[..., :half]], axis=-1)


def _rms(x, g):
    return x * lax.rsqrt(jnp.mean(x * x, axis=-1, keepdims=True) + EPS) * g


def _in_turn(*stage_generators):
    for _ in zip(*stage_generators):
        pass


def _segment_mean_sq(y, m_ref):
    return jnp.dot((y * y).astype(BF16), m_ref[...], preferred_element_type=F32)


def _proj_kernel(x_ref, rope_ref, g_mix_ref, w_lat_ref, w_lstm_ref, w_misc_ref, g_qlat_ref, w_uq_ref,
                 gq_ref, g_kvlat_ref, gkm_ref, w_uk_ref, gkn_ref, w_uv_ref, vones_ref, mseg_ref, mmisc_ref,
                 q_out, k_out, v_out, lat_out, kr_out, qk_out, vl_out, o_out, misc_out):
    x = x_ref[...]
    xb = _rms(x, g_mix_ref[...]).astype(BF16)

    blocks = {"lat": w_lat_ref, "lstm": w_lstm_ref, "misc": w_misc_ref}

    def proj(col):
        return jnp.dot(xb, blocks[col[0]][:, col[1]:col[2]], preferred_element_type=F32)

    c_q = proj(COL_CQ)
    c_kv = proj(COL_CKV)
    misc = proj(COL_MISC)
    vl_out[...] = proj(COL_V)
    misc_out[...] = misc

    tab = rope_ref[...]
    tab_next = pltpu.roll(tab, LANES - ROPE_DIM, 1)
    lane = lax.broadcasted_iota(jnp.int32, tab.shape, 1)
    in_rope = (lane >= NOPE_DIM) & (lane < QK_DIM)
    cq_tab = jnp.where(lane < NOPE_DIM, 1.0, jnp.where(in_rope, tab, 0.0))
    sq_tab = jnp.where(in_rope, tab_next, 0.0)
    ck_tab = jnp.where(lane < ROPE_DIM, tab, 0.0)
    sk_tab = jnp.where(lane < ROPE_DIM, tab_next, 0.0)

    cqn = _rms(c_q, g_qlat_ref[...]).astype(BF16)
    lat = _rms(c_kv, g_kvlat_ref[...])
    lat_out[...] = lat
    latb = lat.astype(BF16)
    lat_t = lat.T.astype(BF16)
    pairs = range(MLA_HEADS // 2)
    cols = [slice(p * HEAD_PAIR, (p + 1) * HEAD_PAIR) for p in pairs]

    q_raw = [jnp.dot(cqn, w_uq_ref[:, cols[p]], preferred_element_type=F32) for p in pairs]
    misc_mean = _segment_mean_sq(misc, mmisc_ref)
    k_raw = [jnp.dot(latb, w_uk_ref[:, cols[p]], preferred_element_type=F32) for p in pairs]
    q_mean = [_segment_mean_sq(q_raw[p], mseg_ref) for p in pairs]
    v_out[...] = (jnp.dot(w_uv_ref[...], lat_t, preferred_element_type=F32) + vones_ref[...]).astype(BF16)
    k_mean = [_segment_mean_sq(k_raw[p], mseg_ref) for p in pairs]

    for p in pairs:
        y = q_raw[p] * lax.rsqrt(q_mean[p] + EPS) * gq_ref[:, cols[p]]
        for j in range(2):
            yh = y[:, j * LANES:(j + 1) * LANES]
            qh = yh * cq_tab + pltpu.roll(yh, LANES - ROPE_DIM, 1) * sq_tab
            h = 2 * p + j
            q_out[:, h * LANES:(h + 1) * LANES] = (qh * Q_LOG2_SCALE).astype(BF16)

    ms = misc * lax.rsqrt(misc_mean + EPS) * gkm_ref[...]
    kr = ms * ck_tab + pltpu.roll(ms, LANES - ROPE_DIM, 1) * sk_tab
    kr_out[...] = kr.T[:ROPE_DIM, :]
    kr_placed = pltpu.roll(kr, NOPE_DIM, 1)
    kr_pair = jnp.concatenate([kr_placed, kr_placed], axis=1)
    for p in pairs:
        y = k_raw[p] * lax.rsqrt(k_mean[p] + EPS) * gkn_ref[:, cols[p]]
        k_out[:, cols[p]] = (y + kr_pair).astype(BF16)

    qk_out[...] = proj(COL_QK)
    o_out[...] = proj(COL_O)


def _full_spec(a):
    nd = a.ndim
    return pl.BlockSpec(a.shape, lambda *_: (0,) * nd)


def _projection(x2, rope_tab, w, tm):
    t = x2.shape[0]
    assert t % tm == 0
    row = lambda c: pl.BlockSpec((tm, c), lambda i: (i, 0))
    weights = [w["g_mix"], w["w_lat"], w["w_lstm"], w["w_misc"], w["g_q_lat"], w["w_uq"], w["gq"], w["g_kv_lat"],
               w["gkm"], w["w_uk"], w["gkn"], w["w_uv_t"], w["vones"], w["mseg"], w["mmisc"]]
    out_shape = (
        jax.ShapeDtypeStruct((t, MLA_HEADS * LANES), BF16),
        jax.ShapeDtypeStruct((t, MLA_HEADS * LANES), BF16),
        jax.ShapeDtypeStruct((MLA_HEADS * VT_ROWS, t), BF16),
        jax.ShapeDtypeStruct((t, KV_LORA), F32),
        jax.ShapeDtypeStruct((ROPE_DIM, t), F32),
        jax.ShapeDtypeStruct((t, LSTM_QK), F32),
        jax.ShapeDtypeStruct((t, LSTM_WIDTH), F32),
        jax.ShapeDtypeStruct((t, LSTM_WIDTH), F32),
        jax.ShapeDtypeStruct((t, LANES), F32),
    )
    return pl.pallas_call(
        _proj_kernel,
        grid=(t // tm,),
        in_specs=[row(x2.shape[1]), row(LANES)] + [_full_spec(a) for a in weights],
        out_specs=tuple(pl.BlockSpec((s.shape[0], tm), lambda i: (0, i)) if n in (2, 4) else row(s.shape[1])
                        for n, s in enumerate(out_shape)),
        out_shape=out_shape,
        compiler_params=pltpu.CompilerParams(dimension_semantics=("parallel",),
                                             vmem_limit_bytes=VMEM_LIMIT),
        name="projection",
    )(x2, rope_tab, *weights)


def _attn_prompt_kernel(q_ref, qnext_ref, k_ref, vt_ref, o_ref, qt_sc, m_sc, acc_sc, sa_sc, sb_sc, xa_sc, xb_sc,
                        *, tq, tk):
    qi = pl.program_id(2)
    m_sc[...] = jnp.full(m_sc.shape, -1e30, F32)
    acc_sc[...] = jnp.zeros(acc_sc.shape, F32)
    q_col0 = qi * tq
    for j in range(2):
        qt_sc[j] = q_ref[:, j * LANES:(j + 1) * LANES].astype(F32).T.astype(BF16)
    buf_a = (sa_sc, xa_sc)
    buf_b = (sb_sc, xb_sc)

    def put_scores(j, k0, qt, buf):
        s = jnp.dot(k_ref[pl.ds(k0, tk), j * LANES:(j + 1) * LANES], qt, preferred_element_type=F32)
        buf[0][j] = s
        buf[1][j] = jnp.max(s, axis=0, keepdims=True)

    def scores_head(j, k0, buf):
        put_scores(j, k0, qt_sc[j], buf)

    def scores(k0, buf):
        for j in range(2):
            scores_head(j, k0, buf)

    def next_tile_scores(j):
        put_scores(j, 0, qnext_ref[:, j * LANES:(j + 1) * LANES].astype(F32).T.astype(BF16), buf_a)

    def consume_head(j, k0, buf, masked):
        vt = vt_ref[j * VT_ROWS:(j + 1) * VT_ROWS, pl.ds(k0, tk)]
        s = buf[0][j]
        if masked:
            kc = (k0 + lax.broadcasted_iota(jnp.int32, (tk, tq), 0)) // CHUNK
            qc = (q_col0 + lax.broadcasted_iota(jnp.int32, (tk, tq), 1)) // CHUNK
            s = jnp.where(kc <= qc, s, -1e30)
            s_max = jnp.max(s, axis=0, keepdims=True)
        else:
            s_max = buf[1][j]
        m_prev = m_sc[j]
        m_next = jnp.maximum(m_prev, s_max)
        alpha = jnp.exp2(m_prev - m_next)
        p = jnp.exp2(s - m_next)
        m_sc[j] = m_next
        acc_sc[j] = alpha * acc_sc[j] + jnp.dot(vt, p.astype(BF16), preferred_element_type=F32)

    def consume(k0, buf, masked):
        for j in range(2):
            consume_head(j, k0, buf, masked)

    n_full = q_col0 // tk
    at = lambda t: pl.multiple_of(t * tk, tk)

    @pl.when(qi == 0)
    def _first_tile_of_pair():
        scores(0, buf_a)

    def pair(t):
        scores(at(t + 1), buf_b)
        for j in range(2):
            consume_head(j, at(t), buf_a, False)
            scores_head(j, at(t + 2), buf_a)
        consume(at(t + 1), buf_b, False)

    def body(i, carry):
        for u in range(PAIRS_PER_TRIP):
            pair(2 * PAIRS_PER_TRIP * i + 2 * u)
        return carry

    trips = n_full // (2 * PAIRS_PER_TRIP)
    lax.fori_loop(0, trips, body, 0)

    def rest(i, carry):
        pair(2 * PAIRS_PER_TRIP * trips + 2 * i)
        return carry

    lax.fori_loop(0, (n_full % (2 * PAIRS_PER_TRIP)) // 2, rest, 0)

    t0 = 2 * (n_full // 2)

    @pl.when(n_full % 2 == 1)
    def _odd_tail():
        scores(at(t0 + 1), buf_b)
        for j in range(2):
            consume_head(j, at(t0), buf_a, False)
            next_tile_scores(j)
        consume(at(t0 + 1), buf_b, True)

    @pl.when(n_full % 2 == 0)
    def _even_tail():
        for j in range(2):
            consume_head(j, at(t0), buf_a, True)
            next_tile_scores(j)

    outs = []
    for j in range(2):
        acc = acc_sc[j]
        outs.append(acc[:V_DIM, :] / acc[V_DIM:V_DIM + 1, :])
    o_ref[...] = jnp.concatenate(outs, axis=0).T.astype(o_ref.dtype)


def _attention_prompt(q, k, vt, tq, tk):
    b, s, _ = q.shape
    assert tq == tk and s % tq == 0 and tq % CHUNK == 0
    pairs = MLA_HEADS // 2
    return pl.pallas_call(
        functools.partial(_attn_prompt_kernel, tq=tq, tk=tk),
        grid=(b, pairs, s // tq),
        in_specs=[
            pl.BlockSpec((None, tq, HEAD_PAIR), lambda bi, p, i: (bi, i, p)),
            pl.BlockSpec((None, tq, HEAD_PAIR), lambda bi, p, i: (bi, jnp.minimum(i + 1, s // tq - 1), p)),
            pl.BlockSpec((None, s, HEAD_PAIR), lambda bi, p, i: (bi, 0, p)),
            pl.BlockSpec((2 * VT_ROWS, s), lambda bi, p, i: (p, bi)),
        ],
        out_specs=pl.BlockSpec((None, tq, LANES), lambda bi, p, i: (bi, i, p)),
        out_shape=jax.ShapeDtypeStruct((b, s, MLA_WIDTH), BF16),
        scratch_shapes=[pltpu.VMEM((2, LANES, tq), BF16),
                        pltpu.VMEM((2, 1, tq), F32),
                        pltpu.VMEM((2, VT_ROWS, tq), F32),
                        pltpu.VMEM((2, tk, tq), F32),
                        pltpu.VMEM((2, tk, tq), F32),
                        pltpu.VMEM((2, 1, tq), F32),
                        pltpu.VMEM((2, 1, tq), F32)],
        compiler_params=pltpu.CompilerParams(
            dimension_semantics=("parallel", "parallel", "arbitrary"),
            vmem_limit_bytes=VMEM_LIMIT),
        name="attention_prompt",
    )(q, q, k, vt)


def _attn_sample_kernel(q_ref, kn_ref, latn_ref, lat_ref, kr_ref, w_uk_ref, gkn_ref, w_uv_ref, mseg_ref,
                        o_ref, k_sc, *, n_new, G):
    def sequence(g):
        tok = lambda ref: ref.at[g * n_new:(g + 1) * n_new]
        return _attn_sample_sequence(tok(q_ref), tok(kn_ref), tok(latn_ref), lat_ref.at[g], kr_ref.at[g],
                                     w_uk_ref, gkn_ref, w_uv_ref, mseg_ref, tok(o_ref), k_sc.at[g], n_new=n_new)

    _in_turn(*[sequence(g) for g in range(G)])


def _attn_sample_sequence(q_ref, kn_ref, latn_ref, lat_ref, kr_ref, w_uk_ref, gkn_ref, w_uv_ref, mseg_ref,
                          o_ref, k_sc, *, n_new):
    latb = lat_ref[...].astype(BF16)
    kr_t = kr_ref[...]
    zeros = lambda r: jnp.zeros((r, kr_t.shape[1]), F32)
    kr_placed = jnp.concatenate([zeros(NOPE_DIM), kr_t, zeros(LANES - QK_DIM)], axis=0).T
    kr_pair = jnp.concatenate([kr_placed, kr_placed], axis=1)
    pairs = range(MLA_HEADS // 2)
    cols = [slice(p * HEAD_PAIR, (p + 1) * HEAD_PAIR) for p in pairs]
    raw = [jnp.dot(latb, w_uk_ref[:, cols[p]], preferred_element_type=F32) for p in pairs]
    yield
    mean_sq = [_segment_mean_sq(raw[p], mseg_ref) for p in pairs]
    yield
    for p in pairs:
        y = raw[p] * lax.rsqrt(mean_sq[p] + EPS) * gkn_ref[:, cols[p]]
        k_sc[:, cols[p]] = (y + kr_pair).astype(BF16)

    q = q_ref[...]
    qt = jnp.concatenate([q] * MLA_HEADS, axis=0)
    r_head = lax.broadcasted_iota(jnp.int32, qt.shape, 0) // n_new
    c_head = lax.broadcasted_iota(jnp.int32, qt.shape, 1) // LANES
    qm = jnp.where(r_head == c_head, qt, jnp.zeros_like(qt))

    nt = (((1,), (1,)), ((), ()))
    s_old = lax.dot_general(k_sc[...], qm, nt, preferred_element_type=F32)
    s_new = lax.dot_general(kn_ref[...], qm, nt, preferred_element_type=F32)
    yield
    mx = jnp.maximum(jnp.max(s_old, axis=0, keepdims=True), jnp.max(s_new, axis=0, keepdims=True))
    p_old = jnp.exp2(s_old - mx)
    p_new = jnp.exp2(s_new - mx)
    inv = 1.0 / (jnp.sum(p_old, axis=0, keepdims=True) + jnp.sum(p_new, axis=0, keepdims=True))
    p_old = (p_old * inv).astype(BF16)
    p_new = (p_new * inv).astype(BF16)
    tn = (((0,), (0,)), ((), ()))
    ctx = (lax.dot_general(p_old, latb, tn, preferred_element_type=F32)
           + lax.dot_general(p_new, latn_ref[...].astype(BF16), tn, preferred_element_type=F32))
    yield
    full = jnp.dot(ctx.astype(BF16), w_uv_ref[...], preferred_element_type=F32)
    v_head = lax.broadcasted_iota(jnp.int32, (n_new, MLA_WIDTH), 1) // V_DIM
    out = jnp.zeros((n_new, MLA_WIDTH), F32)
    for h in range(MLA_HEADS):
        out = out + jnp.where(v_head == h, full[h * n_new:(h + 1) * n_new, :], 0.0)
    o_ref[...] = out.astype(o_ref.dtype)
    yield


def _attention_sample(q, k_new, lat_new, cache_lat, cache_kr, layer, w, n_new, G):
    _, b, past, _ = cache_lat.shape
    assert b % G == 0
    weights = [w["w_uk"], w["gkn"], w["w_uv"], w["mseg"]]
    tok = lambda c: pl.BlockSpec((G * n_new, c), lambda i: (i, 0))
    return pl.pallas_call(
        functools.partial(_attn_sample_kernel, n_new=n_new, G=G),
        grid=(b // G,),
        in_specs=[tok(MLA_HEADS * LANES), tok(MLA_HEADS * LANES), tok(KV_LORA),
                  pl.BlockSpec((None, G, past, KV_LORA), lambda i: (layer, i, 0, 0)),
                  pl.BlockSpec((None, G, ROPE_DIM, past), lambda i: (layer, i, 0, 0))]
                 + [_full_spec(a) for a in weights],
        out_specs=tok(MLA_WIDTH),
        out_shape=jax.ShapeDtypeStruct((b * n_new, MLA_WIDTH), BF16),
        scratch_shapes=[pltpu.VMEM((G, past, MLA_HEADS * LANES), BF16)],
        compiler_params=pltpu.CompilerParams(dimension_semantics=("parallel",),
                                             vmem_limit_bytes=VMEM_LIMIT),
        name="attention_sample",
    )(q, k_new, lat_new, cache_lat, cache_kr, *weights)


CONV_PAD = SUBLANES


def _mlstm_kernel(qk_ref, v_ref, o_ref, misc_ref, conv0_ref, c0_ref, n0_ref, m0_ref, wconv_ref, bconv_ref,
                  bgate_ref, glstm_ref,
                  h_out, conv_out, c_out, n_out, m_out,
                  full_sc, c_sc, n_sc, m_sc, *, L, G, nc):
    def sequence(g):
        tok = lambda ref: ref.at[g * L:(g + 1) * L]
        return _mlstm_sequence(tok(qk_ref), tok(v_ref), tok(o_ref), tok(misc_ref), conv0_ref.at[g], c0_ref.at[g],
                               n0_ref.at[g], m0_ref.at[g], wconv_ref, bconv_ref, bgate_ref, glstm_ref,
                               tok(h_out), conv_out.at[g], c_out.at[g], n_out.at[g], m_out.at[g],
                               full_sc.at[g], c_sc.at[g], n_sc.at[g], m_sc.at[g], L=L, nc=nc)

    _in_turn(*[sequence(g) for g in range(G)])


def _mlstm_sequence(qk_ref, v_ref, o_ref, misc_ref, conv0_ref, c0_ref, n0_ref, m0_ref, wconv_ref, bconv_ref,
                    bgate_ref, glstm_ref,
                    h_out, conv_out, c_out, n_out, m_out,
                    full_sc, c_sc, n_sc, m_sc, *, L, nc):
    hist = CONV_W - 1
    lo = CONV_PAD - hist
    first_chunk = (lambda f: f()) if nc == 1 else pl.when(pl.program_id(1) == 0)
    last_chunk = (lambda f: f()) if nc == 1 else pl.when(pl.program_id(1) == nc - 1)

    @first_chunk
    def _init():
        full_sc[lo:CONV_PAD, :] = conv0_ref[...]
        c_sc[...] = c0_ref[...]
        n_sc[...] = n0_ref[...]
        m_sc[...] = m0_ref[...]

    full_sc[CONV_PAD:CONV_PAD + L, :] = qk_ref[...]
    ext = full_sc[...]
    y = bconv_ref[...] + ext[CONV_PAD:, :] * wconv_ref[CONV_W - 1:CONV_W, :]
    for r in range(1, CONV_W):
        y = y + pltpu.roll(ext, r, 0)[CONV_PAD:, :] * wconv_ref[CONV_W - 1 - r:CONV_W - r, :]
    qk = y * jax.nn.sigmoid(y)
    tail = full_sc[lo + L:CONV_PAD + L, :]
    full_sc[lo:CONV_PAD, :] = tail
    yield

    LP = max(L, LANES)
    pad_rows = lambda a: a if L == LP else jnp.concatenate([a, jnp.zeros((LP - L, a.shape[1]), a.dtype)], axis=0)
    gs_t = pad_rows(misc_ref[...] + bgate_ref[...]).T
    g8 = gs_t[MISC_IG:MISC_IG + 2 * LSTM_HEADS, :]
    lf8 = jnp.minimum(g8, 0.0) - jnp.log1p(jnp.exp(-jnp.abs(g8)))
    s_idx = lax.broadcasted_iota(jnp.int32, (LP, LP), 0)
    t_idx = lax.broadcasted_iota(jnp.int32, (LP, LP), 1)
    causal = s_idx <= t_idx
    triu = jnp.where(causal, 1.0, 0.0).astype(BF16)
    lf_hi = lf8.astype(BF16)
    lf_mid = (lf8 - lf_hi.astype(F32)).astype(BF16)
    lf_lo = (lf8 - lf_hi.astype(F32) - lf_mid.astype(F32)).astype(BF16)
    b8 = (jnp.dot(lf_hi, triu, preferred_element_type=F32)
          + jnp.dot(lf_mid, triu, preferred_element_type=F32)
          + jnp.dot(lf_lo, triu, preferred_element_type=F32))
    yield
    c4 = g8[:LSTM_HEADS, :] - b8[LSTM_HEADS:, :]
    c_cols = jnp.concatenate([c4, jnp.zeros((LANES - LSTM_HEADS, LP), F32)], axis=0).T
    lane_t = lax.broadcasted_iota(jnp.int32, (1, LP), 1)

    def head(h):
        dk = slice(h * LSTM_DK, (h + 1) * LSTM_DK)
        dv = slice(h * LSTM_DV, (h + 1) * LSTM_DV)
        ig_row = g8[h:h + 1, :]
        b_row = b8[LSTM_HEADS + h:LSTM_HEADS + h + 1, :]
        c_col = c_cols[:, h:h + 1]
        m_prev = m_sc[:, h:h + 1]

        log_d = jnp.where(causal, c_col + b_row, -jnp.inf)
        inter = b_row + m_prev
        m_t = jnp.maximum(inter, jnp.max(log_d, axis=0, keepdims=True))
        decay = jnp.exp(log_d - m_t)
        inter_scale = jnp.exp(inter - m_t)

        qb = pad_rows(qk[:, dk]).astype(BF16)
        kb = pad_rows(qk[:, LSTM_HEADS * LSTM_DK + h * LSTM_DK:LSTM_HEADS * LSTM_DK + (h + 1) * LSTM_DK]
                      * (LSTM_DK ** -0.5)).astype(BF16)
        v_t = pad_rows(v_ref[:, dv]).T
        c_prev = c_sc[h]
        n_prev = n_sc[h:h + 1, :]
        nt = (((1,), (1,)), ((), ()))
        wgt = lax.dot_general(kb, qb, nt, preferred_element_type=F32) * decay
        cq = lax.dot_general(c_prev.astype(BF16), qb, nt, preferred_element_type=F32)
        nq = lax.dot_general(jnp.broadcast_to(n_prev, (SUBLANES, LSTM_DK)).astype(BF16), qb, nt,
                             preferred_element_type=F32)[:1, :]
        yield
        num = jnp.dot(v_t.astype(BF16), wgt.astype(BF16), preferred_element_type=F32) + inter_scale * cq
        den = jnp.sum(wgt, axis=0, keepdims=True) + inter_scale * nq
        hid = num / jnp.maximum(jnp.abs(den), jnp.exp(-m_t))
        yield

        m_new = m_t[:, L - 1:L]
        b_last = b_row[:, L - 1:L]
        carry = jnp.exp(b_last + m_prev - m_new)
        d_end = jnp.exp(b_last - b_row + ig_row - m_new)
        if L < LP:
            d_end = jnp.where(lane_t < L, d_end, 0.0)
        upd = jnp.dot((v_t * d_end).astype(BF16), kb, preferred_element_type=F32)
        nk = jnp.dot(jnp.broadcast_to(d_end, (SUBLANES, LP)).astype(BF16), kb,
                     preferred_element_type=F32)[:1, :]
        c_sc[h] = carry * c_prev + upd
        n_sc[h:h + 1, :] = carry * n_prev + nk
        m_sc[:, h:h + 1] = m_new

        g_col = jnp.concatenate([glstm_ref[dv, :]] * (LP // LANES), axis=1)
        hn = hid * lax.rsqrt(jnp.mean(hid * hid, axis=0, keepdims=True) + EPS) * g_col
        h_out[:, dv] = (hn.T[:L, :] * jax.nn.sigmoid(o_ref[:, dv])).astype(h_out.dtype)
        yield

    for _ in zip(*[head(h) for h in range(LSTM_HEADS)]):
        yield

    @last_chunk
    def _finish():
        conv_out[...] = tail
        c_out[...] = c_sc[...]
        n_out[...] = n_sc[...]
        m_out[...] = m_sc[...]

    yield


def _mlstm(qk_raw, v_raw, o_raw, misc, conv0, c0, n0, m0, layer, w, L, G):
    b = conv0.shape[1]
    t = qk_raw.shape[0]
    nc = t // (b * L)
    assert nc * b * L == t and b % G == 0 and (G == 1 or nc == 1)
    hist = CONV_W - 1
    tok = lambda cdim: pl.BlockSpec((G * L, cdim), lambda bi, ci: (bi * nc + ci, 0))
    m0 = m0.reshape(m0.shape[0], b, 1, LSTM_HEADS)
    state0 = lambda *dims: pl.BlockSpec((None, G) + dims, lambda bi, ci: (layer, bi) + (0,) * len(dims))
    weights = [w["w_conv"], w["b_conv"], w["bgate"], w["g_lstm_out"]]
    out_shape = (
        jax.ShapeDtypeStruct((t, LSTM_WIDTH), BF16),
        jax.ShapeDtypeStruct((b, hist, LSTM_QK), F32),
        jax.ShapeDtypeStruct((b, LSTM_HEADS, LSTM_DV, LSTM_DK), F32),
        jax.ShapeDtypeStruct((b, LSTM_HEADS, LSTM_DK), F32),
        jax.ShapeDtypeStruct((b, 1, LSTM_HEADS), F32),
    )
    state = lambda *dims: pl.BlockSpec((G,) + dims, lambda bi, ci: (bi,) + (0,) * len(dims))
    h, conv_new, c_new, n_new, m_new = pl.pallas_call(
        functools.partial(_mlstm_kernel, L=L, G=G, nc=nc),
        grid=(b // G, nc),
        in_specs=[tok(LSTM_QK), tok(LSTM_WIDTH), tok(LSTM_WIDTH), tok(LANES),
                  state0(hist, LSTM_QK), state0(LSTM_HEADS, LSTM_DV, LSTM_DK), state0(LSTM_HEADS, LSTM_DK),
                  state0(1, LSTM_HEADS)] + [_full_spec(a) for a in weights],
        out_specs=(tok(LSTM_WIDTH), state(hist, LSTM_QK), state(LSTM_HEADS, LSTM_DV, LSTM_DK),
                   state(LSTM_HEADS, LSTM_DK), state(1, LSTM_HEADS)),
        out_shape=out_shape,
        scratch_shapes=[pltpu.VMEM((G, CONV_PAD + L, LSTM_QK), F32),
                        pltpu.VMEM((G, LSTM_HEADS, LSTM_DV, LSTM_DK), F32),
                        pltpu.VMEM((G, LSTM_HEADS, LSTM_DK), F32),
                        pltpu.VMEM((G, 1, LSTM_HEADS), F32)],
        compiler_params=pltpu.CompilerParams(dimension_semantics=("parallel", "arbitrary"),
                                             vmem_limit_bytes=VMEM_LIMIT),
        name="mlstm",
    )(qk_raw, v_raw, o_raw, misc, conv0, c0, n0, m0, *weights)
    return h, conv_new, c_new, n_new, m_new.reshape(b, LSTM_HEADS)


FF_CHUNK = 1024


def _out_ffn_kernel(x_ref, attn_ref, lstm_ref, w_out_ref, g_ffn_ref, w_up_ref, w_down_ref, y_ref):
    mix = jnp.concatenate([attn_ref[...], lstm_ref[...]], axis=1)
    x1 = x_ref[...] + jnp.dot(mix, w_out_ref[...], preferred_element_type=F32)
    xb = _rms(x1, g_ffn_ref[...]).astype(BF16)
    d_ff = w_up_ref.shape[1]
    acc = x1
    for f in range(d_ff // FF_CHUNK):
        cols = slice(f * FF_CHUNK, (f + 1) * FF_CHUNK)
        u = jnp.maximum(jnp.dot(xb, w_up_ref[:, cols], preferred_element_type=F32), 0.0)
        acc = acc + jnp.dot((u * u).astype(BF16), w_down_ref[cols, :], preferred_element_type=F32)
    y_ref[...] = acc


def _out_ffn(x2, attn, lstm, ffn_weights, layer, tm):
    t, d = x2.shape
    assert t % tm == 0
    row = lambda c: pl.BlockSpec((tm, c), lambda i: (i, 0))
    const = lambda a: pl.BlockSpec((None,) + a.shape[1:], lambda i: (layer, 0, 0), pipeline_mode=pl.Buffered(1))
    weights = list(ffn_weights)
    return pl.pallas_call(
        _out_ffn_kernel,
        grid=(t // tm,),
        in_specs=[row(d), row(MLA_WIDTH), row(LSTM_WIDTH)] + [const(a) for a in weights],
        out_specs=row(d),
        out_shape=jax.ShapeDtypeStruct((t, d), F32),
        compiler_params=pltpu.CompilerParams(dimension_semantics=("parallel",),
                                             vmem_limit_bytes=VMEM_LIMIT),
        name="out_ffn",
    )(x2, attn, lstm, *weights)


def _pack_layer(l, g_mix, w_in, g_q_lat, w_uq, g_q_nope, g_q_rope, g_kv_lat, g_k_rope, w_uk, g_k_nope, w_uv,
                w_conv, b_conv, b_igate, b_fgate, g_lstm_out):
    d = w_in.shape[1]
    wi = w_in[l]
    kpe = wi[:, W_IN_LAT[1]:W_IN_LAT[1] + ROPE_DIM]
    gates = wi[:, W_IN_LSTM[1]:W_IN_LSTM[1] + 2 * LSTM_HEADS]
    misc = jnp.concatenate([kpe, _rot_half(kpe), gates,
                            jnp.zeros((d, LANES - MISC_FG - LSTM_HEADS), F32)], axis=1)

    uq = w_uq[l].reshape(Q_LORA, MLA_HEADS, QK_DIM)
    uq = jnp.concatenate([uq, _rot_half(uq[..., NOPE_DIM:])], axis=-1).reshape(Q_LORA, MLA_HEADS * LANES)
    gq_head = jnp.concatenate([g_q_nope[l], g_q_rope[l], _rot_half(g_q_rope[l])])
    uk = w_uk[l].reshape(KV_LORA, MLA_HEADS, NOPE_DIM)
    uk = jnp.concatenate([uk, jnp.zeros_like(uk)], axis=-1).reshape(KV_LORA, MLA_HEADS * LANES)
    gkn_head = jnp.concatenate([g_k_nope[l], jnp.zeros((LANES - NOPE_DIM,), F32)])
    uv = w_uv[l].reshape(KV_LORA, MLA_HEADS, V_DIM)
    uv_t = jnp.concatenate([uv, jnp.zeros((KV_LORA, MLA_HEADS, VT_ROWS - V_DIM), F32)], axis=-1)
    uv_t = uv_t.reshape(KV_LORA, MLA_HEADS * VT_ROWS).T
    vones_head = jnp.zeros((VT_ROWS,), F32).at[V_DIM].set(1.0)
    gkm = jnp.concatenate([g_k_rope[l], _rot_half(g_k_rope[l]), jnp.zeros((LANES - 2 * ROPE_DIM,), F32)])
    bgate = jnp.concatenate([jnp.zeros((MISC_IG,), F32), b_igate[l], b_fgate[l],
                             jnp.zeros((LANES - MISC_FG - LSTM_HEADS,), F32)])

    i = jnp.arange(HEAD_PAIR)
    same = (i[:, None] // LANES) == (i[None, :] // LANES)
    li, lj = i[:, None] % LANES, i[None, :] % LANES
    mseg = jnp.where(same & (li < NOPE_DIM) & (lj < NOPE_DIM), 1.0 / NOPE_DIM,
                     jnp.where(same & (li >= NOPE_DIM) & (li < QK_DIM) & (lj >= NOPE_DIM), 1.0 / ROPE_DIM, 0.0))
    a = jnp.arange(LANES)
    mmisc = jnp.where((a[:, None] < ROPE_DIM) & (a[None, :] < 2 * ROPE_DIM), 1.0 / ROPE_DIM, 0.0)

    row = lambda v: v.reshape(1, -1).astype(F32)
    return {
        "g_mix": row(g_mix[l]), "w_lat": wi[:, W_IN_LAT[0]:W_IN_LAT[1]].astype(BF16),
        "w_lstm": wi[:, W_IN_LSTM[0]:W_IN_LSTM[1]].astype(BF16), "w_misc": misc.astype(BF16),
        "g_q_lat": row(g_q_lat[l]), "w_uq": uq.astype(BF16),
        "gq": row(jnp.tile(gq_head, MLA_HEADS)), "g_kv_lat": row(g_kv_lat[l]), "gkm": row(gkm),
        "w_uk": uk.astype(BF16), "gkn": row(jnp.tile(gkn_head, MLA_HEADS)), "w_uv": w_uv[l].astype(BF16),
        "w_uv_t": uv_t.astype(BF16), "vones": jnp.tile(vones_head, MLA_HEADS).reshape(-1, 1),
        "mseg": mseg.astype(BF16), "mmisc": mmisc.astype(BF16),
        "w_conv": w_conv[l], "b_conv": row(b_conv[l]), "bgate": row(bgate), "g_lstm_out": jnp.broadcast_to(g_lstm_out[l][:, None], (LSTM_WIDTH, LANES)),
    }


def _rope_table(first_pos, n):
    half = ROPE_DIM // 2
    blk = min(n, LANES)
    assert n % blk == 0
    lane = jnp.arange(LANES, dtype=jnp.int32)
    inv = ROPE_BASE ** (-(lane % half).astype(F32) / half)
    ang_a = (first_pos + blk * jnp.arange(n // blk, dtype=jnp.int32)).astype(F32)[:, None] * inv[None, :]
    ang_b = jnp.arange(blk, dtype=jnp.int32).astype(F32)[:, None] * inv[None, :]
    ca, sa = jnp.cos(ang_a)[:, None, :], jnp.sin(ang_a)[:, None, :]
    cb, sb = jnp.cos(ang_b)[None, :, :], jnp.sin(ang_b)[None, :, :]
    cos = ca * cb - sa * sb
    sin = sa * cb + ca * sb
    kind = (lane // half) % 4
    tab = jnp.where(kind < 2, cos, jnp.where(kind == 2, -sin, sin))
    return tab.reshape(n, LANES)


def kernel(x_prompt, x_sample, cache_kv_latent, cache_k_rope, state_conv, state_C, state_n, state_m,
           g_mix, w_in, g_q_lat, w_uq, g_q_nope, g_q_rope, g_kv_lat, g_k_rope, w_uk, g_k_nope, w_uv,
           w_conv, b_conv, b_igate, b_fgate, g_lstm_out, w_out, g_ffn, w_up, w_down):
    depth = w_in.shape[0]
    bp, sp, d = x_prompt.shape
    bs, ls, _ = x_sample.shape
    past = cache_kv_latent.shape[2]
    hist = CONV_W - 1

    tabs_p = jnp.tile(_rope_table(0, sp), (bp, 1))
    tabs_s = jnp.tile(_rope_table(past, ls), (bs, 1))
    cache_kr_t = jnp.swapaxes(cache_k_rope, 2, 3)
    xp = x_prompt.reshape(bp * sp, d)
    xs = x_sample.reshape(bs * ls, d)
    zero_conv = jnp.zeros((1, bp, hist, LSTM_QK), F32)
    zero_c = jnp.zeros((1, bp, LSTM_HEADS, LSTM_DV, LSTM_DK), F32)
    zero_n = jnp.zeros((1, bp, LSTM_HEADS, LSTM_DK), F32)
    zero_m = jnp.zeros((1, bp, LSTM_HEADS), F32)
    ffn_weights = (w_out.astype(BF16), g_ffn.reshape(depth, 1, d), w_up.astype(BF16), w_down.astype(BF16))

    outs = {k: [] for k in ("p_lat", "p_kr", "p_conv", "p_c", "p_n", "p_m",
                            "s_lat", "s_kr", "s_conv", "s_c", "s_n", "s_m")}
    for l in range(depth):
        w = _pack_layer(l, g_mix, w_in, g_q_lat, w_uq, g_q_nope, g_q_rope, g_kv_lat, g_k_rope, w_uk, g_k_nope,
                        w_uv, w_conv, b_conv, b_igate, b_fgate, g_lstm_out)
        q, k, v, lat, kr, qk_raw, v_raw, o_raw, misc = _projection(xp, tabs_p, w, tm=512)
        attn = _attention_prompt(q.reshape(bp, sp, -1), k.reshape(bp, sp, -1), v,
                                 tq=ATTN_TQ, tk=ATTN_TK)
        h, conv_new, c_new, n_new, m_new = _mlstm(qk_raw, v_raw, o_raw, misc, zero_conv, zero_c, zero_n, zero_m,
                                                  0, w, L=MLSTM_TILE, G=1)
        xp = _out_ffn(xp, attn.reshape(bp * sp, -1), h, ffn_weights, l, tm=512)
        outs["p_lat"].append(lat.reshape(bp, sp, KV_LORA))
        outs["p_kr"].append(kr.reshape(ROPE_DIM, bp, sp))
        outs["p_conv"].append(conv_new)
        outs["p_c"].append(c_new)
        outs["p_n"].append(n_new)
        outs["p_m"].append(m_new)
        q, k, v, lat, kr, qk_raw, v_raw, o_raw, misc = _projection(xs, tabs_s, w, tm=bs * ls)
        attn = _attention_sample(q, k, lat, cache_kv_latent, cache_kr_t, l, w, n_new=ls, G=ATTN_SAMPLE_GROUP)
        h, conv_new, c_new, n_new, m_new = _mlstm(qk_raw, v_raw, o_raw, misc, state_conv, state_C,
                                                  state_n, state_m, l, w, L=ls, G=MLSTM_SAMPLE_GROUP)
        xs = _out_ffn(xs, attn, h, ffn_weights, l, tm=bs * ls)
        outs["s_lat"].append(lat.reshape(bs, ls, KV_LORA))
        outs["s_kr"].append(kr.reshape(ROPE_DIM, bs, ls))
        outs["s_conv"].append(conv_new)
        outs["s_c"].append(c_new)
        outs["s_n"].append(n_new)
        outs["s_m"].append(m_new)

    st = lambda key: jnp.stack(outs[key])
    kr_st = lambda key: jnp.transpose(jnp.stack(outs[key]), (0, 2, 3, 1))
    return (xp.reshape(bp, sp, d), xs.reshape(bs, ls, d),
            st("p_lat"), kr_st("p_kr"), st("p_conv"), st("p_c"), st("p_n"), st("p_m"),
            st("s_lat"), kr_st("s_kr"), st("s_conv"), st("s_c"), st("s_n"), st("s_m"))
```
